```python
import math
import jax
import jax.numpy as jnp
from jax import lax
import numpy as np

D_MODEL = 1024
BATCH = 4
SEQ = 4096
DEPTH = 2
DEC_BATCH = 32
DEC_SEQ = 4
PAST_LEN = 16384
PAGE_SIZE = 128

HEAD_DIM = 64
ROT_DIM = HEAD_DIM // 4
ROPE_THETA = 500000.0
NSA_HEADS = D_MODEL // 2 // HEAD_DIM
NSA_KV_HEADS = max(1, NSA_HEADS // 4)
NSA_GROUP = NSA_HEADS // NSA_KV_HEADS
CMP_STRIDE = 16
CMP_BLOCK = 2 * CMP_STRIDE
SEL_BLOCK = 64
N_SEL = 16
WINDOW = 512
DIFF_HEADS = D_MODEL // 4 // HEAD_DIM
DIFF_HALF = HEAD_DIM // 2
DIFF_ROT = DIFF_HALF // 4
FOX_HEADS = D_MODEL // 4 // HEAD_DIM
MEM_TOKENS = 256
MEM_HEADS = 4
MEM_HEAD_DIM = D_MODEL // MEM_HEADS
PEER_HEADS = 8
PEER_KEYS = 128
PEER_EXPERTS = PEER_KEYS * PEER_KEYS
PEER_TOPK = 16
PEER_QDIM = 256
PEER_HALF = PEER_QDIM // 2
QBLK = 128
PEER_CHUNK = 256
EPS = 1e-6
NEG_INF = -1e30
FORCE_SCORE = 1e4
IN_SIZES = (NSA_HEADS * HEAD_DIM,) + (NSA_KV_HEADS * HEAD_DIM,) * 6 + (NSA_HEADS * 3,) + (DIFF_HEADS * HEAD_DIM,) * 3 + (FOX_HEADS * HEAD_DIM,) * 3 + (FOX_HEADS,)
D_IN = sum(IN_SIZES)

kernel_name = 'hymba_nsa_diff_fox_peer_step'


def rmsnorm(x, g):
    xf = x.astype(jnp.float32)
    y = xf * lax.rsqrt(jnp.mean(xf * xf, axis=-1, keepdims=True) + EPS)
    return (y * g.astype(jnp.float32)).astype(x.dtype)


def rope(x, pos, rot_dim):
    half = rot_dim // 2
    inv = jnp.power(jnp.float32(ROPE_THETA), -jnp.arange(half, dtype=jnp.float32) / half)
    ang = pos.astype(jnp.float32)[:, None] * inv[None, :]
    shape = (1, pos.shape[0]) + (1,) * (x.ndim - 3) + (half,)
    cos = jnp.cos(ang).reshape(shape)
    sin = jnp.sin(ang).reshape(shape)
    xr = x[..., :rot_dim].astype(jnp.float32)
    x1, x2 = xr[..., :half], xr[..., half:]
    rot = jnp.concatenate([x1 * cos - x2 * sin, x2 * cos + x1 * sin], axis=-1)
    return jnp.concatenate([rot.astype(x.dtype), x[..., rot_dim:]], axis=-1)


def masked_softmax(s, mask):
    s = jnp.where(mask, s.astype(jnp.float32), NEG_INF)
    e = jnp.exp(s - jnp.max(s, axis=-1, keepdims=True)) * mask
    return e / jnp.maximum(jnp.sum(e, axis=-1, keepdims=True), 1e-30)


def sweep_queries(fn, q_arrays, q_pos):
    t = q_pos.shape[0]
    if t <= QBLK or t % QBLK != 0:
        return fn(*q_arrays, q_pos)
    nb = t // QBLK

    def split(a):
        return jnp.swapaxes(a.reshape((a.shape[0], nb, QBLK) + a.shape[2:]), 0, 1)

    out = lax.map(lambda args: fn(*args[0], args[1]), (tuple(split(a) for a in q_arrays), q_pos.reshape(nb, QBLK)))
    out = jnp.swapaxes(out, 0, 1)
    return out.reshape((out.shape[0], t) + out.shape[3:])


def nsa_compress(k, pe, w):
    b, l, g, d = k.shape
    nseg = l // CMP_STRIDE
    seg = k[:, :nseg * CMP_STRIDE].reshape(b, nseg, CMP_STRIDE, g, d)
    w = w.reshape(CMP_BLOCK, d, d)
    w1, w2 = w[:CMP_STRIDE], w[CMP_STRIDE:]
    first = jnp.einsum('bnpgd,pde->bnge', seg, w1) + jnp.einsum('pd,pde->e', pe[:CMP_STRIDE], w1)
    second = jnp.einsum('bnpgd,pde->bnge', seg, w2) + jnp.einsum('pd,pde->e', pe[CMP_STRIDE:], w2)
    return first[:, :-1] + second[:, 1:]


def block_importance(p, nb):
    r = SEL_BLOCK // CMP_STRIDE
    n_cmp = p.shape[-1]
    pp = jnp.pad(p, [(0, 0)] * (p.ndim - 1) + [(1, r * nb + r - 1 - n_cmp)])
    cur = pp[..., :r * nb].reshape(p.shape[:-1] + (nb, r))
    nxt = pp[..., r:r * nb + r].reshape(p.shape[:-1] + (nb, r))[..., 0]
    wts = jnp.array([1.0] + [2.0] * (r - 1), dtype=p.dtype)
    return jnp.einsum('...jr,r->...j', cur, wts) + nxt


def nsa_block(q_plain, q_rot, gate, q_pos, kcmp, vcmp, ks_b, vs_b, win_kv):
    b, tq = q_plain.shape[:2]
    scale = HEAD_DIM ** -0.5
    qp = q_plain.reshape(b, tq, NSA_KV_HEADS, NSA_GROUP, HEAD_DIM)
    qr = q_rot.reshape(b, tq, NSA_KV_HEADS, NSA_GROUP, HEAD_DIM)
    n_cmp = kcmp.shape[1]
    cmp_end = jnp.arange(n_cmp) * CMP_STRIDE + (CMP_BLOCK - 1)
    m_c = (cmp_end[None, :] <= q_pos[:, None])[None, :, None, None, :]
    p_c = masked_softmax(jnp.einsum('btghd,bngd->btghn', qp, kcmp) * scale, m_c)
    o_c = jnp.einsum('btghn,bngd->btghd', p_c.astype(vcmp.dtype), vcmp)
    nb = ks_b.shape[2]
    imp = block_importance(jnp.sum(p_c, axis=3), nb)
    blk = jnp.arange(nb)[None, :]
    cur = (q_pos // SEL_BLOCK)[:, None]
    forced = ((blk == 0) | (blk == cur) | (blk == cur - 1))[None, :, None, :]
    avail = (blk <= cur)[None, :, None, :]
    score = jnp.where(forced, FORCE_SCORE, jnp.where(avail, imp, NEG_INF))
    _, idx = lax.top_k(score, min(N_SEL, nb))
    idx_g = jnp.transpose(idx, (0, 2, 1, 3))
    take = jax.vmap(jax.vmap(lambda tab, i: tab[i]))
    n_k = idx.shape[-1] * SEL_BLOCK
    gk = take(ks_b, idx_g).reshape(b, NSA_KV_HEADS, tq, n_k, HEAD_DIM)
    gv = take(vs_b, idx_g).reshape(b, NSA_KV_HEADS, tq, n_k, HEAD_DIM)
    kpos = (idx_g[..., None] * SEL_BLOCK + jnp.arange(SEL_BLOCK)).reshape(b, NSA_KV_HEADS, tq, n_k)
    m_s = jnp.transpose(kpos <= q_pos[None, None, :, None], (0, 2, 1, 3))[:, :, :, None, :]
    p_s = masked_softmax(jnp.einsum('btghd,bgtnd->btghn', qr, gk) * scale, m_s)
    o_s = jnp.einsum('btghn,bgtnd->btghd', p_s.astype(gv.dtype), gv)
    kw, vw, kpos_w = win_kv(q_pos)
    rel = kpos_w[None, :]
    m_w = ((rel <= q_pos[:, None]) & (rel > q_pos[:, None] - WINDOW) & (rel >= 0))[None, :, None, None, :]
    p_w = masked_softmax(jnp.einsum('btghd,bsgd->btghs', qr, kw) * scale, m_w)
    o_w = jnp.einsum('btghs,bsgd->btghd', p_w.astype(vw.dtype), vw)
    g = gate.reshape(b, tq, NSA_KV_HEADS, NSA_GROUP, 3)
    o = g[..., 0:1] * o_c + g[..., 1:2] * o_s + g[..., 2:3] * o_w
    return o.reshape(b, tq, NSA_HEADS, HEAD_DIM)


def diff_block(q, q_pos, k, v, lam, gain, lam_init):
    s = jnp.einsum('bthcd,bshcd->bhcts', q, k).astype(jnp.float32) * DIFF_HALF ** -0.5
    mask = jnp.arange(k.shape[1])[None, :] <= q_pos[:, None]
    a = masked_softmax(s, mask)
    w = a[:, :, 0] - lam * a[:, :, 1]
    o = jnp.einsum('bhts,bshd->bthd', w.astype(v.dtype), v)
    return rmsnorm(o, gain) * (1.0 - lam_init)


def fox_block(q, cq, q_pos, k, v, ck):
    s = jnp.einsum('bthd,bshd->bhts', q, k).astype(jnp.float32) * HEAD_DIM ** -0.5
    s = s + jnp.transpose(cq, (0, 2, 1))[:, :, :, None] - jnp.transpose(ck, (0, 2, 1))[:, :, None, :]
    mask = jnp.arange(k.shape[1])[None, :] <= q_pos[:, None]
    p = masked_softmax(s, mask)
    return jnp.einsum('bhts,bshd->bthd', p.astype(v.dtype), v)


def project_mixers(h, w_in, b_in, pos):
    b, t, _ = h.shape
    z = jnp.einsum('btd,de->bte', h, w_in) + b_in
    pts = np.cumsum(np.array(IN_SIZES))[:-1].tolist()
    (nq, kc, vc, ks, vs, kw, vw, ng, dq, dk, dv, fq, fk, fv, ff) = jnp.split(z, pts, axis=-1)
    hd = lambda a, n: a.reshape(b, t, n, HEAD_DIM)
    nq = hd(nq, NSA_HEADS)
    nq_rot = rope(nq, pos, ROT_DIM)
    kc, vc, vs, vw = hd(kc, NSA_KV_HEADS), hd(vc, NSA_KV_HEADS), hd(vs, NSA_KV_HEADS), hd(vw, NSA_KV_HEADS)
    ks = rope(hd(ks, NSA_KV_HEADS), pos, ROT_DIM)
    kw = rope(hd(kw, NSA_KV_HEADS), pos, ROT_DIM)
    gate = jax.nn.sigmoid(ng).reshape(b, t, NSA_HEADS, 3)
    dq = rope(dq.reshape(b, t, DIFF_HEADS, 2, DIFF_HALF), pos, DIFF_ROT)
    dk = rope(dk.reshape(b, t, DIFF_HEADS, 2, DIFF_HALF), pos, DIFF_ROT)
    dv = hd(dv, DIFF_HEADS)
    fq, fk, fv = hd(fq, FOX_HEADS), hd(fk, FOX_HEADS), hd(fv, FOX_HEADS)
    logf = jax.nn.log_sigmoid(ff.astype(jnp.float32))
    return (nq, nq_rot, gate, kc, vc, ks, vs, kw, vw, dq, dk, dv, fq, fk, fv, logf)


def mix_heads(nq, nq_rot, gate, dq, fq, pos, kc, vc, ks, vs, win_kv, dk, dv, fk, fv, logf,
              pe_k, pe_v, w_ck, w_cv, diff_lam, diff_gain, lam_init):
    b, t = nq.shape[:2]
    l = kc.shape[1]
    kcmp = nsa_compress(kc, pe_k, w_ck)
    vcmp = nsa_compress(vc, pe_v, w_cv)
    nb = -(-l // SEL_BLOCK)

    def to_blocks(a):
        a = jnp.pad(a, ((0, 0), (0, nb * SEL_BLOCK - l), (0, 0), (0, 0)))
        return jnp.transpose(a.reshape(b, nb, SEL_BLOCK, NSA_KV_HEADS, HEAD_DIM), (0, 3, 1, 2, 4))

    ks_b, vs_b = to_blocks(ks), to_blocks(vs)
    o_nsa = sweep_queries(lambda a1, a2, a3, p: nsa_block(a1, a2, a3, p, kcmp, vcmp, ks_b, vs_b, win_kv), (nq, nq_rot, gate), pos)
    dl = diff_lam.astype(jnp.float32)
    lam = jnp.exp(jnp.sum(dl[0] * dl[1])) - jnp.exp(jnp.sum(dl[2] * dl[3])) + lam_init
    o_diff = sweep_queries(lambda q, p: diff_block(q, p, dk, dv, lam, diff_gain, lam_init), (dq,), pos)
    c = jnp.cumsum(logf.astype(jnp.float32), axis=1)
    o_fox = sweep_queries(lambda q, cq, p: fox_block(q, cq, p, fk, fv, c), (fq, c[:, l - t:]), pos)
    return jnp.concatenate([o_nsa.reshape(b, t, -1), o_diff.reshape(b, t, -1), o_fox.reshape(b, t, -1)], axis=-1)


def hybrid_sublayer(x, pos, past, win_buf, g_attn, w_in, b_in, pe_k, pe_v, w_ck, w_cv, diff_lam, diff_gain, lam_init, w_o):
    b, t, _ = x.shape
    h = rmsnorm(x, g_attn)
    (nq, nq_rot, gate, kc, vc, ks, vs, kw, vw, dq, dk, dv, fq, fk, fv, logf) = project_mixers(h, w_in, b_in, pos)
    rows = (jnp.stack([kc, vc], axis=2), jnp.stack([ks, vs], axis=2),
            jnp.stack([dk.reshape(b, t, DIFF_HEADS, HEAD_DIM), dv], axis=2),
            jnp.stack([fk, fv], axis=2), logf)
    win_rows = jnp.stack([kw, vw], axis=2)
    if past is None:
        kc_f, vc_f, ks_f, vs_f, dk_f, dv_f, fk_f, fv_f, logf_f = kc, vc, ks, vs, dk, dv, fk, fv, logf
        kw_pad = jnp.pad(kw, ((0, 0), (WINDOW, 0), (0, 0), (0, 0)))
        vw_pad = jnp.pad(vw, ((0, 0), (WINDOW, 0), (0, 0), (0, 0)))

        def win_kv(q_pos):
            n = q_pos.shape[0] + WINDOW
            start = q_pos[0]
            return (lax.dynamic_slice_in_dim(kw_pad, start, n, axis=1),
                    lax.dynamic_slice_in_dim(vw_pad, start, n, axis=1),
                    start - WINDOW + jnp.arange(n))

        new_win = win_rows[:, -min(WINDOW, t):]
    else:
        p_cmp, p_sel, p_diff, p_fox, p_logf = past
        n_past = p_cmp.shape[1]
        cat = lambda a1, a2: jnp.concatenate([a1, a2], axis=1)
        kc_f, vc_f = cat(p_cmp[:, :, 0], kc), cat(p_cmp[:, :, 1], vc)
        ks_f, vs_f = cat(p_sel[:, :, 0], ks), cat(p_sel[:, :, 1], vs)
        dk_f = cat(p_diff[:, :, 0].reshape(b, n_past, DIFF_HEADS, 2, DIFF_HALF), dk)
        dv_f = cat(p_diff[:, :, 1], dv)
        fk_f, fv_f = cat(p_fox[:, :, 0], fk), cat(p_fox[:, :, 1], fv)
        logf_f = cat(p_logf.astype(jnp.float32), logf)
        win_all = cat(win_buf, win_rows)
        n_w = win_all.shape[1]
        kpos_w = pos[0] - win_buf.shape[1] + jnp.arange(n_w)
        win_kv = lambda q_pos: (win_all[:, :, 0], win_all[:, :, 1], kpos_w)
        new_win = win_all[:, -min(WINDOW, n_w):]
    mixed = mix_heads(nq, nq_rot, gate, dq, fq, pos, kc_f, vc_f, ks_f, vs_f, win_kv, dk_f, dv_f, fk_f, fv_f, logf_f,
                      pe_k, pe_v, w_ck, w_cv, diff_lam, diff_gain, lam_init)
    return x + jnp.einsum('btm,md->btd', mixed, w_o), rows, new_win


def mem_project(mem, w_mkv):
    b, m, _ = mem.shape
    return jnp.einsum('bmd,de->bme', mem, w_mkv).reshape(b, m, 2, MEM_HEADS, MEM_HEAD_DIM)


def mem_sublayer(x, mem_kv, g, w_mq, w_mo):
    b, t, _ = x.shape
    q = jnp.einsum('btd,de->bte', rmsnorm(x, g), w_mq).reshape(b, t, MEM_HEADS, MEM_HEAD_DIM)
    s = jnp.einsum('bthd,bmhd->bhtm', q, mem_kv[:, :, 0]).astype(jnp.float32) * MEM_HEAD_DIM ** -0.5
    p = jax.nn.softmax(s, axis=-1).astype(mem_kv.dtype)
    o = jnp.einsum('bhtm,bmhd->bthd', p, mem_kv[:, :, 1]).reshape(b, t, D_MODEL)
    return x + jnp.einsum('btd,de->bte', o, w_mo)


def peer_ffn(h, wq, keys, u, v):
    b, t, d = h.shape
    n = b * t
    x = h.reshape(n, d)

    def chunk(xc):
        m = xc.shape[0]
        q = (xc @ wq).reshape(m, PEER_HEADS, 2, PEER_HALF)
        s = jnp.einsum('nhcd,hckd->nhck', q, keys).astype(jnp.float32)
        s1, i1 = lax.top_k(s[:, :, 0], PEER_TOPK)
        s2, i2 = lax.top_k(s[:, :, 1], PEER_TOPK)
        cand = (s1[..., :, None] + s2[..., None, :]).reshape(m, PEER_HEADS, PEER_TOPK * PEER_TOPK)
        sc, ci = lax.top_k(cand, PEER_TOPK)
        e = jnp.take_along_axis(i1, ci // PEER_TOPK, axis=-1) * PEER_KEYS + jnp.take_along_axis(i2, ci % PEER_TOPK, axis=-1)
        g = jax.nn.softmax(sc, axis=-1)
        a = jax.nn.gelu(jnp.einsum('nd,nhkd->nhk', xc, u[e]).astype(jnp.float32), approximate=False)
        return jnp.einsum('nhk,nhkd->nd', (g * a).astype(v.dtype), v[e])

    if n > PEER_CHUNK and n % PEER_CHUNK == 0:
        out = lax.map(chunk, x.reshape(n // PEER_CHUNK, PEER_CHUNK, d)).reshape(n, d)
    else:
        out = chunk(x)
    return out.reshape(b, t, d)


def paged_past(pool, page_table):
    rows = pool[page_table]
    return rows.reshape((rows.shape[0], rows.shape[1] * rows.shape[2]) + rows.shape[3:])


def setup_inputs(seed: int = 0) -> dict:
    key = jax.random.key(seed)
    keys = iter(jax.random.split(key, 40))
    nrm = lambda shape, scale: jax.random.normal(next(keys), shape, jnp.float32) * scale
    n_pages = PAST_LEN // PAGE_SIZE
    n_used = DEC_BATCH * n_pages
    n_pool = n_used + max(1, n_used // 4)
    win_len = min(WINDOW, PAST_LEN)
    d = D_MODEL
    x_prompt = nrm((BATCH, SEQ, d), 1.0)
    x_sample = nrm((DEC_BATCH, DEC_SEQ, d), 1.0)
    cache_nsa_cmp_kv = nrm((DEPTH, n_pool, PAGE_SIZE, 2, NSA_KV_HEADS, HEAD_DIM), 1.0)
    cache_nsa_sel_kv = nrm((DEPTH, n_pool, PAGE_SIZE, 2, NSA_KV_HEADS, HEAD_DIM), 1.0)
    cache_diff_kv = nrm((DEPTH, n_pool, PAGE_SIZE, 2, DIFF_HEADS, HEAD_DIM), 1.0)
    cache_fox_kv = nrm((DEPTH, n_pool, PAGE_SIZE, 2, FOX_HEADS, HEAD_DIM), 1.0)
    cache_fox_logf = jax.nn.log_sigmoid(nrm((DEPTH, n_pool, PAGE_SIZE, FOX_HEADS), 1.0))
    state_nsa_win_kv = nrm((DEPTH, DEC_BATCH, win_len, 2, NSA_KV_HEADS, HEAD_DIM), 1.0)
    cache_mem_kv = nrm((DEPTH, DEC_BATCH, MEM_TOKENS, 2, MEM_HEADS, MEM_HEAD_DIM), 1.0)
    page_table = jax.random.permutation(next(keys), n_pool)[:n_used].reshape(DEC_BATCH, n_pages).astype(jnp.int32)
    mem_prompt = nrm((BATCH, MEM_TOKENS, d), 1.0)
    return {
        'x_prompt': x_prompt, 'x_sample': x_sample,
        'cache_nsa_cmp_kv': cache_nsa_cmp_kv, 'cache_nsa_sel_kv': cache_nsa_sel_kv,
        'cache_diff_kv': cache_diff_kv, 'cache_fox_kv': cache_fox_kv, 'cache_fox_logf': cache_fox_logf,
        'state_nsa_win_kv': state_nsa_win_kv, 'cache_mem_kv': cache_mem_kv,
        'page_table': page_table, 'mem_prompt': mem_prompt,
        'g_attn': 1.0 + nrm((DEPTH, d), 0.01),
        'w_in': nrm((DEPTH, d, D_IN), d ** -0.5),
        'b_in': nrm((DEPTH, D_IN), 0.02),
        'nsa_pe_k': nrm((DEPTH, CMP_BLOCK, HEAD_DIM), 0.1),
        'nsa_pe_v': nrm((DEPTH, CMP_BLOCK, HEAD_DIM), 0.1),
        'nsa_w_ck': nrm((DEPTH, CMP_BLOCK * HEAD_DIM, HEAD_DIM), (CMP_BLOCK * HEAD_DIM) ** -0.5),
        'nsa_w_cv': nrm((DEPTH, CMP_BLOCK * HEAD_DIM, HEAD_DIM), (CMP_BLOCK * HEAD_DIM) ** -0.5),
        'diff_lambda': nrm((DEPTH, 4, DIFF_HALF), 0.1),
        'diff_gain': 1.0 + nrm((DEPTH, HEAD_DIM), 0.01),
        'w_o': nrm((DEPTH, d, d), d ** -0.5),
        'g_mem': 1.0 + nrm((DEPTH, d), 0.01),
        'w_mq': nrm((DEPTH, d, d), d ** -0.5),
        'w_mkv': nrm((DEPTH, d, 2 * d), d ** -0.5),
        'w_mo': nrm((DEPTH, d, d), d ** -0.5),
        'g_ffn': 1.0 + nrm((DEPTH, d), 0.01),
        'peer_wq': nrm((DEPTH, d, PEER_HEADS * PEER_QDIM), d ** -0.5),
        'peer_keys': nrm((DEPTH, PEER_HEADS, 2, PEER_KEYS, PEER_HALF), PEER_HALF ** -0.5),
        'peer_u': nrm((DEPTH, PEER_EXPERTS, d), d ** -0.5),
        'peer_v': nrm((DEPTH, PEER_EXPERTS, d), PEER_HEADS ** -0.5),
        'g_final': 1.0 + nrm((d,), 0.01),
    }


def reference(x_prompt, x_sample, cache_nsa_cmp_kv, cache_nsa_sel_kv, cache_diff_kv, cache_fox_kv, cache_fox_logf,
              state_nsa_win_kv, cache_mem_kv, page_table, mem_prompt, g_attn, w_in, b_in, nsa_pe_k, nsa_pe_v,
              nsa_w_ck, nsa_w_cv, diff_lambda, diff_gain, w_o, g_mem, w_mq, w_mkv, w_mo, g_ffn, peer_wq, peer_keys,
              peer_u, peer_v, g_final):
    t_p = x_prompt.shape[1]
    t_s = x_sample.shape[1]
    past_len = page_table.shape[1] * cache_nsa_cmp_kv.shape[2]
    pos_p = jnp.arange(t_p, dtype=jnp.int32)
    pos_s = past_len + jnp.arange(t_s, dtype=jnp.int32)
    xp, xs = x_prompt, x_sample
    rows_p = [[] for _ in range(5)]
    rows_s = [[] for _ in range(5)]
    win_p, win_s, mem_p = [], [], []
    for l in range(DEPTH):
        lam_init = 0.8 - 0.6 * math.exp(-0.3 * l)
        lw = (g_attn[l], w_in[l], b_in[l], nsa_pe_k[l], nsa_pe_v[l], nsa_w_ck[l], nsa_w_cv[l],
              diff_lambda[l], diff_gain[l], lam_init, w_o[l])
        xp, r_p, wn_p = hybrid_sublayer(xp, pos_p, None, None, *lw)
        mkv_p = mem_project(mem_prompt, w_mkv[l])
        xp = mem_sublayer(xp, mkv_p, g_mem[l], w_mq[l], w_mo[l])
        xp = xp + peer_ffn(rmsnorm(xp, g_ffn[l]), peer_wq[l], peer_keys[l], peer_u[l], peer_v[l])
        past = (paged_past(cache_nsa_cmp_kv[l], page_table), paged_past(cache_nsa_sel_kv[l], page_table),
                paged_past(cache_diff_kv[l], page_table), paged_past(cache_fox_kv[l], page_table),
                paged_past(cache_fox_logf[l], page_table))
        xs, r_s, wn_s = hybrid_sublayer(xs, pos_s, past, state_nsa_win_kv[l], *lw)
        xs = mem_sublayer(xs, cache_mem_kv[l], g_mem[l], w_mq[l], w_mo[l])
        xs = xs + peer_ffn(rmsnorm(xs, g_ffn[l]), peer_wq[l], peer_keys[l], peer_u[l], peer_v[l])
        for i in range(5):
            rows_p[i].append(r_p[i])
            rows_s[i].append(r_s[i])
        win_p.append(wn_p)
        win_s.append(wn_s)
        mem_p.append(mkv_p)
    y_prompt = rmsnorm(xp, g_final)
    y_sample = rmsnorm(xs, g_final)
    st = lambda a: jnp.stack(a, axis=0)
    return (y_prompt, y_sample,
            st(rows_p[0]), st(rows_s[0]), st(rows_p[1]), st(rows_s[1]),
            st(rows_p[2]), st(rows_s[2]), st(rows_p[3]), st(rows_s[3]),
            st(rows_p[4]), st(rows_s[4]), st(win_p), st(win_s), st(mem_p))
```

```python
import functools
import math

import jax
import jax.numpy as jnp
from jax import lax
from jax.experimental import pallas as pl
from jax.experimental.pallas import tpu as pltpu

F32 = jnp.float32
BF16 = jnp.bfloat16

HEAD_DIM = 64
ROT_DIM = HEAD_DIM // 4
ROPE_THETA = 500000.0
NSA_HEADS = 8
NSA_KV_HEADS = 2
NSA_GROUP = NSA_HEADS // NSA_KV_HEADS
CMP_STRIDE = 16
CMP_BLOCK = 32
SEL_BLOCK = 64
N_SEL = 16
WINDOW = 512
DIFF_HEADS = 4
DIFF_HALF = HEAD_DIM // 2
DIFF_ROT = DIFF_HALF // 4
FOX_HEADS = 4
MEM_HEADS = 4
PEER_HEADS = 8
PEER_KEYS = 128
PEER_TOPK = 16
PEER_HALF = 128
EPS = 1e-6
NEG_INF = -1e30
FORCE_SCORE = 1e4
LANES = 128
VMEM_LIMIT = 56 * 1024 * 1024


def _params(sem, vmem=VMEM_LIMIT):
    return pltpu.CompilerParams(dimension_semantics=sem, vmem_limit_bytes=vmem)


def _dot(a, b):
    return jnp.dot(a, b, preferred_element_type=F32)


def _dot_nt(a, b):
    return lax.dot_general(a, b, (((1,), (1,)), ((), ())), preferred_element_type=F32)


def _rms(x, g):
    return x * lax.rsqrt(jnp.mean(x * x, axis=-1, keepdims=True) + EPS) * g


def _linear_kernel(*refs, has_gain, has_bias, has_res):
    it = iter(refs)
    x_ref = next(it)
    g_ref = next(it) if has_gain else None
    w_ref = next(it)
    b_ref = next(it) if has_bias else None
    r_ref = next(it) if has_res else None
    o_ref = next(it)
    xb_ref = next(it)

    @pl.when(pl.program_id(1) == 0)
    def _():
        x = x_ref[...].astype(F32)
        if has_gain:
            x = _rms(x, g_ref[...])
        xb_ref[...] = x.astype(BF16)

    y = _dot(xb_ref[...], w_ref[...])
    if has_bias:
        y = y + b_ref[...]
    if has_res:
        y = y + r_ref[...]
    o_ref[...] = y.astype(o_ref.dtype)


def _linear(x, w, gain=None, bias=None, res=None, tm=512, tn=512, out_dtype=F32):
    m, k = x.shape
    n = w.shape[1]
    tm = min(tm, m)
    tn = min(tn, n)
    assert m % tm == 0 and n % tn == 0
    args = [x]
    specs = [pl.BlockSpec((tm, k), lambda i, j: (i, 0))]
    if gain is not None:
        args.append(gain.reshape(1, k))
        specs.append(pl.BlockSpec((1, k), lambda i, j: (0, 0)))
    args.append(w.astype(BF16))
    specs.append(pl.BlockSpec((k, tn), lambda i, j: (0, j)))
    if bias is not None:
        args.append(bias.reshape(1, n))
        specs.append(pl.BlockSpec((1, tn), lambda i, j: (0, j)))
    if res is not None:
        args.append(res)
        specs.append(pl.BlockSpec((tm, tn), lambda i, j: (i, j)))
    return pl.pallas_call(
        functools.partial(_linear_kernel, has_gain=gain is not None, has_bias=bias is not None,
                          has_res=res is not None),
        grid=(m // tm, n // tn),
        in_specs=specs,
        out_specs=pl.BlockSpec((tm, tn), lambda i, j: (i, j)),
        out_shape=jax.ShapeDtypeStruct((m, n), out_dtype),
        scratch_shapes=[pltpu.VMEM((tm, k), BF16)],
        compiler_params=_params(("parallel", "arbitrary")),
        name="linear",
    )(*args)


def _peer_scores_kernel(x_ref, g_ref, wq_ref, keys_ref, xn_ref, st_ref):
    hb = _rms(x_ref[...], g_ref[...]).astype(BF16)
    xn_ref[...] = hb
    qb = _dot(hb, wq_ref[...]).astype(BF16)
    for hc in range(2 * PEER_HEADS):
        st_ref[hc] = _dot_nt(keys_ref[hc], qb[:, hc * PEER_HALF:(hc + 1) * PEER_HALF])


def _peer_scores(x, g, wq, keys):
    n, d = x.shape
    tm = min(256, n)
    nq = wq.shape[1]
    hc = 2 * PEER_HEADS
    return pl.pallas_call(
        _peer_scores_kernel,
        grid=(n // tm,),
        in_specs=[pl.BlockSpec((tm, d), lambda i: (i, 0)),
                  pl.BlockSpec((1, d), lambda i: (0, 0)),
                  pl.BlockSpec((d, nq), lambda i: (0, 0)),
                  pl.BlockSpec((hc, PEER_KEYS, PEER_HALF), lambda i: (0, 0, 0))],
        out_specs=[pl.BlockSpec((tm, d), lambda i: (i, 0)),
                   pl.BlockSpec((hc, PEER_KEYS, tm), lambda i: (0, 0, i))],
        out_shape=[jax.ShapeDtypeStruct((n, d), BF16),
                   jax.ShapeDtypeStruct((hc, PEER_KEYS, n), F32)],
        compiler_params=_params(("parallel",)),
        name="peer_scores",
    )(x, g.reshape(1, d), wq.astype(BF16), keys.reshape(hc, PEER_KEYS, PEER_HALF).astype(BF16))


_PEER_PAIRS = [(a, b) for a in range(PEER_TOPK) for b in range(PEER_TOPK) if (a + 1) * (b + 1) <= PEER_TOPK]


def _peer_topk_kernel(st_ref, t_ref, r_ref, a_ref, c_ref, rk_ref, sv_ref, av_ref, cn_ref):
    tn = st_ref.shape[-1]
    row = lax.broadcasted_iota(jnp.int32, (PEER_KEYS, tn), 0).astype(F32)
    big = float(PEER_KEYS)

    for hc in range(2 * PEER_HEADS):
        h, c = divmod(hc, 2)

        def extract(t, carry, h=h, c=c):
            s, rank = carry
            m = jnp.max(s, axis=0, keepdims=True)
            idx = jnp.min(jnp.where(s == m, row, big), axis=0, keepdims=True)
            hit = row == idx
            sv_ref[c, t, h:h + 1, :] = m
            return jnp.where(hit, -jnp.inf, s), jnp.where(hit, t.astype(F32), rank)

        _, rank = lax.fori_loop(0, PEER_TOPK, extract, (st_ref[hc], jnp.full((PEER_KEYS, tn), big, F32)))
        rk_ref[hc] = rank

    cand = [sv_ref[0, a] + sv_ref[1, b] for a, b in _PEER_PAIRS]
    top = sv_ref[0, 0] + sv_ref[1, 0]
    sel = []
    for ia, (a, b) in enumerate(_PEER_PAIRS):
        cnt = jnp.zeros_like(top)
        for ib, (a2, b2) in enumerate(_PEER_PAIRS):
            if ib == ia:
                continue
            ahead = (cand[ib] >= cand[ia]) if (a2 * PEER_TOPK + b2) < (a * PEER_TOPK + b) else (cand[ib] > cand[ia])
            cnt = cnt + jnp.where(ahead, 1.0, 0.0)
        sel.append(jnp.where(cnt < float(PEER_TOPK), 1.0, 0.0))
    z = jnp.zeros_like(top)
    counts = [jnp.zeros_like(top) for _ in range(PEER_TOPK)]
    for ia, (a, b) in enumerate(_PEER_PAIRS):
        z = z + sel[ia] * jnp.exp(cand[ia] - top)
        counts[a] = counts[a] + sel[ia]
    for a in range(PEER_TOPK):
        av_ref[a] = jnp.exp(sv_ref[0, a] - sv_ref[0, 0]) / z
        cn_ref[a] = counts[a]

    for h in range(PEER_HEADS):
        rank1 = rk_ref[2 * h]
        rank2 = rk_ref[2 * h + 1]

        def scatter(a, carry, h=h, rank1=rank1):
            wa, ca = carry
            hit = rank1 == a.astype(F32)
            wa = jnp.where(hit, av_ref[a, h:h + 1, :], wa)
            ca = jnp.where(hit, cn_ref[a, h:h + 1, :], ca)
            return wa, ca

        zero = jnp.zeros((PEER_KEYS, tn), F32)
        wa, ca = lax.fori_loop(0, PEER_TOPK, scatter, (zero, zero))
        a_ref[h] = wa
        c_ref[h] = ca
        r_ref[h] = rank2
        t_ref[h] = jnp.where(rank2 < float(PEER_TOPK), jnp.exp(st_ref[2 * h + 1] - sv_ref[1, 0, h:h + 1, :]), 0.0)


def _peer_topk(st):
    hc, nk, n = st.shape
    tn = LANES
    spec_h = pl.BlockSpec((PEER_HEADS, nk, tn), lambda i: (0, 0, i))
    shape_h = jax.ShapeDtypeStruct((PEER_HEADS, nk, n), F32)
    return pl.pallas_call(
        _peer_topk_kernel,
        grid=(n // tn,),
        in_specs=[pl.BlockSpec((hc, nk, tn), lambda i: (0, 0, i))],
        out_specs=[spec_h] * 4,
        out_shape=[shape_h] * 4,
        scratch_shapes=[pltpu.VMEM((hc, nk, tn), F32),
                        pltpu.VMEM((2, PEER_TOPK, PEER_HEADS, tn), F32),
                        pltpu.VMEM((PEER_TOPK, PEER_HEADS, tn), F32),
                        pltpu.VMEM((PEER_TOPK, PEER_HEADS, tn), F32)],
        compiler_params=_params(("parallel",)),
        name="peer_topk",
    )(st)


def _gelu(a):
    return 0.5 * a * (1.0 + lax.erf(a * math.sqrt(0.5)))


def _peer_dense_kernel(xn_ref, u_ref, vt_ref, t_ref, r_ref, a_ref, c_ref, res_ref, o_ref, acc_ref, h_ref):
    e = pl.program_id(1)
    te = u_ref.shape[0]

    @pl.when(e == 0)
    def _():
        acc_ref[...] = jnp.zeros_like(acc_ref)

    at = _dot_nt(u_ref[...], xn_ref[...])
    for ib in range(te // PEER_KEYS):
        i = e * (te // PEER_KEYS) + ib
        g = jnp.zeros((PEER_KEYS, at.shape[1]), F32)
        for h in range(PEER_HEADS):
            cnt = c_ref[h, pl.ds(i, 1), :]
            wa = a_ref[h, pl.ds(i, 1), :]
            g = g + jnp.where(r_ref[h] < cnt, t_ref[h] * wa, 0.0)
        a = at[ib * PEER_KEYS:(ib + 1) * PEER_KEYS]
        h_ref[ib * PEER_KEYS:(ib + 1) * PEER_KEYS, :] = (g * _gelu(a)).astype(BF16)
    acc_ref[...] += _dot(vt_ref[...], h_ref[...])

    @pl.when(e == pl.num_programs(1) - 1)
    def _():
        o_ref[...] = res_ref[...] + acc_ref[...].T


def _peer_dense(xn, u, vt, t, r, a, c, res, te=256):
    n, d = xn.shape
    ne = u.shape[0]
    tm = min(512, n)
    spec_h = pl.BlockSpec((PEER_HEADS, PEER_KEYS, tm), lambda i, e: (0, 0, i))
    return pl.pallas_call(
        _peer_dense_kernel,
        grid=(n // tm, ne // te),
        in_specs=[pl.BlockSpec((tm, d), lambda i, e: (i, 0)),
                  pl.BlockSpec((te, d), lambda i, e: (e, 0)),
                  pl.BlockSpec((d, te), lambda i, e: (0, e)),
                  spec_h, spec_h, spec_h, spec_h,
                  pl.BlockSpec((tm, d), lambda i, e: (i, 0))],
        out_specs=pl.BlockSpec((tm, d), lambda i, e: (i, 0)),
        out_shape=jax.ShapeDtypeStruct((n, d), F32),
        scratch_shapes=[pltpu.VMEM((d, tm), F32), pltpu.VMEM((te, tm), BF16)],
        compiler_params=_params(("parallel", "arbitrary")),
        name="peer_dense",
    )(xn, u, vt, t, r, a, c, res)


def _peer(x, g, wq, keys, u_b, vt_b):
    xn, st = _peer_scores(x, g, wq, keys)
    t, r, a, c = _peer_topk(st)
    return _peer_dense(xn, u_b, vt_b, t, r, a, c, x)


_SEGS = (
    ("qn", 1024, "p" * 8),
    ("qr", 1024, "a" * 8),
    ("cmp", 256, "pp"),
    ("sel", 256, "ap"),
    ("win", 256, "ap"),
    ("dq", 1024, "b" * 8),
    ("diff", 512, "bbpp"),
    ("fq", 512, "pppp"),
    ("fox", 512, "pppp"),
    ("gate", 384, "ssl"),
)
_OFF = dict(nq=0, kc=512, vc=640, ks=768, vs=896, kw=1024, vw=1152, ng=1280, dq=1304, dk=1560, dv=1816,
            fq=2072, fk=2328, fv=2584, ff=2840)


def _inproj_columns():
    import numpy as np
    cols = []

    def blocks(n):
        return [np.full(LANES, -1, np.int64) for _ in range(n)]

    for base in ("nq", "nq"):
        bl = blocks(NSA_HEADS)
        for h in range(NSA_HEADS):
            g = h // NSA_GROUP
            bl[h][64 * g:64 * g + 64] = _OFF[base] + 64 * h + np.arange(64)
        cols += bl
    cols.append(np.arange(_OFF["kc"], _OFF["kc"] + 256))
    cols.append(np.arange(_OFF["ks"], _OFF["ks"] + 256))
    cols.append(np.arange(_OFF["kw"], _OFF["kw"] + 256))
    bl = blocks(2 * DIFF_HEADS)
    for h in range(DIFF_HEADS):
        for c in range(2):
            o = 32 * (2 * (h % 2) + c)
            bl[2 * h + c][o:o + 32] = _OFF["dq"] + 64 * h + 32 * c + np.arange(32)
    cols += bl
    cols.append(np.arange(_OFF["dk"], _OFF["dk"] + 512))
    bl = blocks(FOX_HEADS)
    for h in range(FOX_HEADS):
        o = 64 * (h % 2)
        bl[h][o:o + 64] = _OFF["fq"] + 64 * h + np.arange(64)
    cols += bl
    cols.append(np.arange(_OFF["fk"], _OFF["fk"] + 512))
    bl = blocks(3)
    for g in range(NSA_KV_HEADS):
        bl[g][0:3 * NSA_GROUP] = _OFF["ng"] + 3 * NSA_GROUP * g + np.arange(3 * NSA_GROUP)
    bl[2][0:FOX_HEADS] = _OFF["ff"] + np.arange(FOX_HEADS)
    cols += bl
    return np.concatenate(cols)


def _extend_inproj(w, b):
    import numpy as np
    idx = _inproj_columns()
    keep = jnp.asarray(idx >= 0)
    src = jnp.asarray(np.maximum(idx, 0))
    return (jnp.where(keep[None, :], w[:, src], 0.0).astype(BF16),
            jnp.where(keep, b[src], 0.0).reshape(1, -1))


def _rope_tables(pos, period, rot):
    half = rot // 2
    inv = jnp.power(jnp.float32(ROPE_THETA), -jnp.arange(half, dtype=F32) / half)
    ang = pos.astype(F32)[:, None] * inv[None, :]
    cos, sin = jnp.cos(ang), jnp.sin(ang)
    n = pos.shape[0]
    reps = LANES // period
    pad = jnp.zeros((n, period - rot), F32)
    c = jnp.concatenate([cos, cos, pad + 1.0], axis=1)
    s_up = jnp.concatenate([jnp.zeros((n, half), F32), sin, pad], axis=1)
    s_dn = jnp.concatenate([-sin, jnp.zeros((n, half), F32), pad], axis=1)
    return jnp.stack([jnp.tile(c, (1, reps)), jnp.tile(s_up, (1, reps)), jnp.tile(s_dn, (1, reps))])


def _log_sigmoid(x):
    return -(jnp.maximum(-x, 0.0) + jnp.log1p(jnp.exp(-jnp.abs(x))))


def _inproj_kernel(x_ref, g_ref, w_ref, b_ref, ta_ref, tb_ref, *out_refs):
    hb = _rms(x_ref[...], g_ref[...]).astype(BF16)
    c0 = 0
    for (name, width, kinds), o_ref in zip(_SEGS, out_refs):
        z = _dot(hb, w_ref[:, c0:c0 + width]) + b_ref[:, c0:c0 + width]
        for k, kind in enumerate(kinds):
            zk = z[:, k * LANES:(k + 1) * LANES]
            if kind == "a":
                zk = (zk * ta_ref[0] + pltpu.roll(zk, ROT_DIM // 2, 1) * ta_ref[1]
                      + pltpu.roll(zk, LANES - ROT_DIM // 2, 1) * ta_ref[2])
            elif kind == "b":
                zk = (zk * tb_ref[0] + pltpu.roll(zk, DIFF_ROT // 2, 1) * tb_ref[1]
                      + pltpu.roll(zk, LANES - DIFF_ROT // 2, 1) * tb_ref[2])
            elif kind == "s":
                zk = jax.nn.sigmoid(zk)
            elif kind == "l":
                zk = _log_sigmoid(zk)
            o_ref[:, k * LANES:(k + 1) * LANES] = zk
        c0 += width


def _inproj(x, g, w_ext, b_ext, tab_a, tab_b, tm):
    n, d = x.shape
    npos = tab_a.shape[1]
    tm = min(tm, npos)
    nt = npos // tm
    ctot = w_ext.shape[1]
    return pl.pallas_call(
        _inproj_kernel,
        grid=(n // tm,),
        in_specs=[pl.BlockSpec((tm, d), lambda i: (i, 0)),
                  pl.BlockSpec((1, d), lambda i: (0, 0)),
                  pl.BlockSpec((d, ctot), lambda i: (0, 0)),
                  pl.BlockSpec((1, ctot), lambda i: (0, 0)),
                  pl.BlockSpec((3, tm, LANES), lambda i: (0, i % nt, 0)),
                  pl.BlockSpec((3, tm, LANES), lambda i: (0, i % nt, 0))],
        out_specs=[pl.BlockSpec((tm, wd), lambda i: (i, 0)) for _, wd, _ in _SEGS],
        out_shape=[jax.ShapeDtypeStruct((n, wd), F32) for _, wd, _ in _SEGS],
        compiler_params=_params(("parallel",)),
        name="inproj",
    )(x, g.reshape(1, d), w_ext, b_ext, tab_a, tab_b)


def _softmax_masked(s, mask):
    s = jnp.where(mask, s, NEG_INF)
    e = jnp.where(mask, jnp.exp(s - jnp.max(s, axis=-1, keepdims=True)), 0.0)
    return e / jnp.maximum(jnp.sum(e, axis=-1, keepdims=True), 1e-30)


def _online_update(s, mask, v, carry):
    m, l, acc = carry
    r, tq, tk = s.shape
    s = jnp.where(mask, s, NEG_INF)
    m_new = jnp.maximum(m, jnp.max(s, axis=-1, keepdims=True))
    alpha = jnp.exp(m - m_new)
    p = jnp.where(mask, jnp.exp(s - m_new), 0.0)
    l = alpha * l + jnp.sum(p, axis=-1, keepdims=True)
    pv = _dot(p.reshape(r * tq, tk).astype(BF16), v).reshape(r, tq, v.shape[-1])
    return m_new, l, alpha * acc + pv


def _online_init(r, tq, dv=LANES):
    return (jnp.full((r, tq, 1), NEG_INF, F32), jnp.zeros((r, tq, 1), F32), jnp.zeros((r, tq, dv), F32))


def _online_final(carry):
    _, l, acc = carry
    return acc / jnp.maximum(l, 1e-30)


def _stack_rows(ref, nblk, scale):
    parts = [ref[0, :, i * LANES:(i + 1) * LANES] for i in range(nblk)]
    return (jnp.concatenate(parts, axis=0) * scale).astype(BF16)


def _split3(hi):
    a = hi.astype(BF16)
    r1 = hi - a.astype(F32)
    b = r1.astype(BF16)
    c = (r1 - b.astype(F32)).astype(BF16)
    return a, b, c


def _compress_weights(w_ck, w_cv, pe_k, pe_v):
    d = HEAD_DIM
    wk = w_ck.reshape(CMP_BLOCK, d, d)
    wv = w_cv.reshape(CMP_BLOCK, d, d)
    wst = jnp.stack([wk, wk, wv, wv])
    eye = jnp.eye(4, dtype=F32)
    big = jnp.einsum("cpde,cf->pcdfe", wst, eye).reshape(CMP_BLOCK, 4 * d, 4 * d)
    w1 = big[:CMP_STRIDE].reshape(CMP_STRIDE * 4 * d, 4 * d).astype(BF16)
    w2 = big[CMP_STRIDE:].reshape(CMP_STRIDE * 4 * d, 4 * d).astype(BF16)
    pst = jnp.stack([pe_k, pe_k, pe_v, pe_v], axis=1)
    pe1 = jnp.broadcast_to(pst[:CMP_STRIDE].reshape(1, -1), (8, CMP_STRIDE * 4 * d)).astype(BF16)
    pe2 = jnp.broadcast_to(pst[CMP_STRIDE:].reshape(1, -1), (8, CMP_STRIDE * 4 * d)).astype(BF16)
    return w1, w2, pe1, pe2


def _compress_kernel(a_ref, w1_ref, w2_ref, pe1_ref, pe2_ref, f_ref, s_ref, b_ref):
    a = a_ref[0].astype(BF16)
    f_ref[0] = _dot(a, w1_ref[...])
    s_ref[0] = _dot(a, w2_ref[...])
    b_ref[...] = _dot(pe1_ref[...], w1_ref[...]) + _dot(pe2_ref[...], w2_ref[...])


def _compress(a, w1, w2, pe1, pe2):
    b, nseg, ka = a.shape
    ts = min(256, nseg)
    wspec = pl.BlockSpec((ka, 256), lambda i, j: (0, 0))
    pspec = pl.BlockSpec((8, ka), lambda i, j: (0, 0))
    ospec = pl.BlockSpec((1, ts, 256), lambda i, j: (i, j, 0))
    return pl.pallas_call(
        _compress_kernel,
        grid=(b, nseg // ts),
        in_specs=[pl.BlockSpec((1, ts, ka), lambda i, j: (i, j, 0)), wspec, wspec, pspec, pspec],
        out_specs=[ospec, ospec, pl.BlockSpec((8, 256), lambda i, j: (0, 0))],
        out_shape=[jax.ShapeDtypeStruct((b, nseg, 256), F32), jax.ShapeDtypeStruct((b, nseg, 256), F32),
                   jax.ShapeDtypeStruct((8, 256), F32)],
        compiler_params=_params(("arbitrary", "arbitrary")),
        name="compress",
    )(a, w1, w2, pe1, pe2)


def _select_blocks(psum, qpos, nb, sc_ref):
    tq, ncmp = psum.shape
    nbp = sc_ref.shape[0]
    n_i = lax.broadcasted_iota(jnp.int32, (ncmp, nbp), 0)
    j_i = lax.broadcasted_iota(jnp.int32, (ncmp, nbp), 1)
    dlt = n_i - (SEL_BLOCK // CMP_STRIDE) * j_i
    wmat = jnp.where((dlt == -1) | (dlt == 3), 1.0, jnp.where((dlt >= 0) & (dlt <= 2), 2.0, 0.0)).astype(BF16)
    p_hi = psum.astype(BF16)
    p_lo = (psum - p_hi.astype(F32)).astype(BF16)
    imp = _dot(p_hi, wmat) + _dot(p_lo, wmat)
    blk = lax.broadcasted_iota(jnp.int32, (tq, nbp), 1)
    cur = qpos // SEL_BLOCK
    forced = (blk == 0) | (blk == cur) | (blk == cur - 1)
    score = jnp.where(forced, FORCE_SCORE, jnp.where(blk <= cur, imp, NEG_INF))
    sc = score.T
    sc_ref[...] = sc
    rowi = lax.broadcasted_iota(jnp.int32, (nbp, tq), 0)

    def body(j, cnt):
        sj = sc_ref[pl.ds(j, 1), :]
        ahead = jnp.where(sj > sc, 1.0, jnp.where(sj == sc, jnp.where(j < rowi, 1.0, 0.0), 0.0))
        return cnt + ahead

    cnt = lax.fori_loop(0, nb, body, jnp.zeros((nbp, tq), F32))
    return jnp.where(cnt < float(min(N_SEL, nb)), 1.0, 0.0).T


def _expand_blocks(sel, k0, tk):
    nbp = sel.shape[1]
    j_i = lax.broadcasted_iota(jnp.int32, (nbp, tk), 0)
    s_i = lax.broadcasted_iota(jnp.int32, (nbp, tk), 1) + k0
    e = jnp.where(lax.shift_right_logical(s_i, 6) == j_i, 1.0, 0.0).astype(BF16)
    return _dot(sel.astype(BF16), e) > 0.5


def _cmp_branch(qn, kcmp, vcmp, qpos, nseg, ncmp_valid, pos0=0):
    tq = qpos.shape[0]
    n_i = lax.broadcasted_iota(jnp.int32, (1, nseg), 1)
    m_c = ((n_i * CMP_STRIDE + (CMP_BLOCK - 1) + pos0) <= qpos) & (n_i < ncmp_valid)
    s_c = _dot_nt(qn, kcmp).reshape(NSA_GROUP, tq, nseg)
    p_c = _softmax_masked(s_c, m_c[None])
    o_c = _dot(p_c.reshape(NSA_GROUP * tq, nseg).astype(BF16), vcmp).reshape(NSA_GROUP, tq, LANES)
    return o_c, jnp.sum(p_c, axis=0)


def _nsa_prompt_kernel(qn_ref, qr_ref, f_ref, s_ref, cb_ref, sel_ref, win_ref, gate_ref, o_ref,
                       ks16, vs16, kw16, vw16, sc_ref, *, tq, tk, nseg, nb):
    qi = pl.program_id(1)
    q0 = qi * tq

    @pl.when(qi == 0)
    def _():
        ks16[...] = sel_ref[0, :, 0:LANES].astype(BF16)
        vs16[...] = sel_ref[0, :, LANES:2 * LANES].astype(BF16)
        kw16[...] = win_ref[0, :, 0:LANES].astype(BF16)
        vw16[...] = win_ref[0, :, LANES:2 * LANES].astype(BF16)

    scale = HEAD_DIM ** -0.5
    qpos = q0 + lax.broadcasted_iota(jnp.int32, (tq, 1), 0)
    kv = f_ref[0] + pltpu.roll(s_ref[0], nseg - 1, 0) + cb_ref[0:1, :]
    kcmp = kv[:, 0:LANES].astype(BF16)
    vcmp = kv[:, LANES:2 * LANES].astype(BF16)
    wspan = WINDOW + tq
    kstart = pl.multiple_of(jnp.maximum(q0 - WINDOW, 0), tq)
    nkt = (q0 + tq + tk - 1) // tk

    for g in range(NSA_KV_HEADS):
        qn = _stack_rows(qn_ref.at[:, :, g * NSA_GROUP * LANES:(g + 1) * NSA_GROUP * LANES], NSA_GROUP, scale)
        o_c, psum = _cmp_branch(qn, kcmp, vcmp, qpos, nseg, nseg - 1)
        sel = _select_blocks(psum, qpos, nb, sc_ref)
        qr = _stack_rows(qr_ref.at[:, :, g * NSA_GROUP * LANES:(g + 1) * NSA_GROUP * LANES], NSA_GROUP, scale)

        def body(kt, carry, qr=qr, sel=sel):
            k0 = pl.multiple_of(kt * tk, tk)
            s = _dot_nt(qr, ks16[pl.ds(k0, tk), :]).reshape(NSA_GROUP, tq, tk)
            kpos = k0 + lax.broadcasted_iota(jnp.int32, (1, tk), 1)
            mask = _expand_blocks(sel, k0, tk) & (kpos <= qpos)
            return _online_update(s, mask[None], vs16[pl.ds(k0, tk), :], carry)

        o_s = _online_final(lax.fori_loop(0, nkt, body, _online_init(NSA_GROUP, tq)))

        kpos = kstart + lax.broadcasted_iota(jnp.int32, (1, wspan), 1)
        m_w = (kpos <= qpos) & (kpos > qpos - WINDOW)
        s_w = _dot_nt(qr, kw16[pl.ds(kstart, wspan), :]).reshape(NSA_GROUP, tq, wspan)
        p_w = _softmax_masked(s_w, m_w[None])
        o_w = _dot(p_w.reshape(NSA_GROUP * tq, wspan).astype(BF16), vw16[pl.ds(kstart, wspan), :])
        o_w = o_w.reshape(NSA_GROUP, tq, LANES)

        gates = gate_ref[0, :, g * LANES:(g + 1) * LANES]
        for i in range(NSA_GROUP):
            o = (gates[:, 3 * i:3 * i + 1] * o_c[i] + gates[:, 3 * i + 1:3 * i + 2] * o_s[i]
                 + gates[:, 3 * i + 2:3 * i + 3] * o_w[i])
            hblk = g * NSA_GROUP + i
            o_ref[0, :, hblk * LANES:(hblk + 1) * LANES] = o.astype(o_ref.dtype)


def _nsa_prompt(qn, qr, f, s, cb, sel, win, gate, tq=128, tk=512):
    b, t, _ = qn.shape
    nseg = f.shape[1]
    nb = -(-t // SEL_BLOCK)
    nbp = -(-nb // LANES) * LANES
    tk = min(tk, t)
    assert t % tk == 0 and t % tq == 0 and t >= WINDOW + tq
    qspec = pl.BlockSpec((1, tq, NSA_HEADS * LANES), lambda i, j: (i, j, 0))
    fspec = pl.BlockSpec((1, nseg, 256), lambda i, j: (i, 0, 0))
    kspec = pl.BlockSpec((1, t, 256), lambda i, j: (i, 0, 0))
    return pl.pallas_call(
        functools.partial(_nsa_prompt_kernel, tq=tq, tk=tk, nseg=nseg, nb=nb),
        grid=(b, t // tq),
        in_specs=[qspec, qspec, fspec, fspec, pl.BlockSpec((8, 256), lambda i, j: (0, 0)), kspec, kspec,
                  pl.BlockSpec((1, tq, 3 * LANES), lambda i, j: (i, j, 0))],
        out_specs=qspec,
        out_shape=jax.ShapeDtypeStruct((b, t, NSA_HEADS * LANES), BF16),
        scratch_shapes=[pltpu.VMEM((t, LANES), BF16)] * 4 + [pltpu.VMEM((nbp, tq), F32)],
        compiler_params=_params(("arbitrary", "arbitrary")),
        name="nsa_prompt",
    )(qn, qr, f, s, cb, sel, win, gate)


def _cumsum_kernel(lf_ref, col_ref, row_ref, carry_ref, *, tm):
    j = pl.program_id(1)

    @pl.when(j == 0)
    def _():
        carry_ref[...] = jnp.zeros_like(carry_ref)

    r_i = lax.broadcasted_iota(jnp.int32, (tm, tm), 0)
    c_i = lax.broadcasted_iota(jnp.int32, (tm, tm), 1)
    tri = jnp.where(c_i <= r_i, 1.0, 0.0).astype(BF16)
    a, b, c = _split3(lf_ref[0])
    cs = _dot(tri, a) + _dot(tri, b) + _dot(tri, c) + carry_ref[0:1, :]
    carry_ref[...] = jnp.broadcast_to(cs[tm - 1:tm, :], carry_ref.shape)
    sh = pltpu.roll(cs, LANES - 2, 1)
    col_ref[0, :, 0:LANES] = cs
    col_ref[0, :, LANES:2 * LANES] = sh
    row_ref[0, 0:8, :] = cs.T[0:8, :]
    row_ref[0, 8:16, :] = sh.T[0:8, :]


def _cumsum(gate, tm=512):
    b, t, _ = gate.shape
    tm = min(tm, t)
    return pl.pallas_call(
        functools.partial(_cumsum_kernel, tm=tm),
        grid=(b, t // tm),
        in_specs=[pl.BlockSpec((1, tm, LANES), lambda i, j: (i, j, 2))],
        out_specs=[pl.BlockSpec((1, tm, 2 * LANES), lambda i, j: (i, j, 0)),
                   pl.BlockSpec((1, 16, tm), lambda i, j: (i, 0, j))],
        out_shape=[jax.ShapeDtypeStruct((b, t, 2 * LANES), F32), jax.ShapeDtypeStruct((b, 16, t), F32)],
        scratch_shapes=[pltpu.VMEM((8, LANES), F32)],
        compiler_params=_params(("arbitrary", "arbitrary")),
        name="cumsum",
    )(gate)


def _diff_lambda(dl_ref, lam_init):
    dl = dl_ref[...]
    a = jnp.sum(dl[0:1] * dl[1:2], axis=-1, keepdims=True)
    b = jnp.sum(dl[2:3] * dl[3:4], axis=-1, keepdims=True)
    return jnp.exp(a) - jnp.exp(b) + lam_init


def _diff_finish(o, lam, gain_ref, lam_init, o_ref):
    lane = lax.broadcasted_iota(jnp.int32, (1, LANES), 1)
    for hh in range(2):
        w = o[2 * hh] - lam * o[2 * hh + 1]
        keep = jnp.where((lane >= hh * HEAD_DIM) & (lane < (hh + 1) * HEAD_DIM), 1.0, 0.0)
        w = w * keep
        ms = jnp.sum(w * w, axis=-1, keepdims=True) * (1.0 / HEAD_DIM)
        y = w * lax.rsqrt(ms + EPS) * gain_ref[...] * (1.0 - lam_init)
        o_ref[0, :, hh * LANES:(hh + 1) * LANES] = y.astype(o_ref.dtype)


def _diff_prompt_kernel(q_ref, k_ref, v_ref, dl_ref, gain_ref, o_ref, k16, v16, *, tq, tk, lam_init):
    qi = pl.program_id(2)
    q0 = qi * tq

    @pl.when(qi == 0)
    def _():
        k16[...] = k_ref[0].astype(BF16)
        v16[...] = v_ref[0].astype(BF16)

    qpos = q0 + lax.broadcasted_iota(jnp.int32, (tq, 1), 0)
    q = _stack_rows(q_ref, 4, DIFF_HALF ** -0.5)

    def body(kt, carry):
        k0 = pl.multiple_of(kt * tk, tk)
        s = _dot_nt(q, k16[pl.ds(k0, tk), :]).reshape(4, tq, tk)
        kpos = k0 + lax.broadcasted_iota(jnp.int32, (1, tk), 1)
        return _online_update(s, (kpos <= qpos)[None], v16[pl.ds(k0, tk), :], carry)

    o = _online_final(lax.fori_loop(0, (q0 + tq + tk - 1) // tk, body, _online_init(4, tq)))
    _diff_finish(o, _diff_lambda(dl_ref, lam_init), gain_ref, lam_init, o_ref)


def _gain_lanes(gain):
    return jnp.tile(gain.reshape(1, HEAD_DIM), (1, LANES // HEAD_DIM))


def _diff_prompt(dq, rows, dl, gain, lam_init, tq=128, tk=512):
    b, t, _ = dq.shape
    tk = min(tk, t)
    return pl.pallas_call(
        functools.partial(_diff_prompt_kernel, tq=tq, tk=tk, lam_init=lam_init),
        grid=(b, 2, t // tq),
        in_specs=[pl.BlockSpec((1, tq, 4 * LANES), lambda i, kb, j: (i, j, kb)),
                  pl.BlockSpec((1, t, LANES), lambda i, kb, j: (i, 0, kb)),
                  pl.BlockSpec((1, t, LANES), lambda i, kb, j: (i, 0, 2 + kb)),
                  pl.BlockSpec((4, DIFF_HALF), lambda i, kb, j: (0, 0)),
                  pl.BlockSpec((1, LANES), lambda i, kb, j: (0, 0))],
        out_specs=pl.BlockSpec((1, tq, 2 * LANES), lambda i, kb, j: (i, j, kb)),
        out_shape=jax.ShapeDtypeStruct((b, t, 4 * LANES), BF16),
        scratch_shapes=[pltpu.VMEM((t, LANES), BF16)] * 2,
        compiler_params=_params(("arbitrary", "arbitrary", "arbitrary")),
        name="diff_prompt",
    )(dq, rows, rows, dl, _gain_lanes(gain))


def _fox_prompt_kernel(q_ref, k_ref, v_ref, cc_ref, cr_ref, o_ref, k16, v16, *, tq, tk):
    qi = pl.program_id(2)
    q0 = qi * tq

    @pl.when(qi == 0)
    def _():
        k16[...] = k_ref[0].astype(BF16)
        v16[...] = v_ref[0].astype(BF16)

    qpos = q0 + lax.broadcasted_iota(jnp.int32, (tq, 1), 0)
    q = _stack_rows(q_ref, 2, HEAD_DIM ** -0.5)
    cq = jnp.stack([cc_ref[0, :, 0:1], cc_ref[0, :, 1:2]])

    def body(kt, carry):
        k0 = pl.multiple_of(kt * tk, tk)
        ck = cr_ref[0, 0:2, pl.ds(k0, tk)][:, None, :]
        s = _dot_nt(q, k16[pl.ds(k0, tk), :]).reshape(2, tq, tk) + cq - ck
        kpos = k0 + lax.broadcasted_iota(jnp.int32, (1, tk), 1)
        return _online_update(s, (kpos <= qpos)[None], v16[pl.ds(k0, tk), :], carry)

    o = _online_final(lax.fori_loop(0, (q0 + tq + tk - 1) // tk, body, _online_init(2, tq)))
    for hh in range(2):
        o_ref[0, :, hh * LANES:(hh + 1) * LANES] = o[hh].astype(o_ref.dtype)


def _fox_prompt(fq, rows, ccol, crow, tq=128, tk=512):
    b, t, _ = fq.shape
    tk = min(tk, t)
    return pl.pallas_call(
        functools.partial(_fox_prompt_kernel, tq=tq, tk=tk),
        grid=(b, 2, t // tq),
        in_specs=[pl.BlockSpec((1, tq, 2 * LANES), lambda i, kb, j: (i, j, kb)),
                  pl.BlockSpec((1, t, LANES), lambda i, kb, j: (i, 0, kb)),
                  pl.BlockSpec((1, t, LANES), lambda i, kb, j: (i, 0, 2 + kb)),
                  pl.BlockSpec((1, tq, LANES), lambda i, kb, j: (i, j, kb)),
                  pl.BlockSpec((1, 8, t), lambda i, kb, j: (i, kb, 0))],
        out_specs=pl.BlockSpec((1, tq, 2 * LANES), lambda i, kb, j: (i, j, kb)),
        out_shape=jax.ShapeDtypeStruct((b, t, 4 * LANES), BF16),
        scratch_shapes=[pltpu.VMEM((t, LANES), BF16)] * 2,
        compiler_params=_params(("arbitrary", "arbitrary", "arbitrary")),
        name="fox_prompt",
    )(fq, rows, rows, ccol, crow)


def _mem_attn_kernel(q_ref, kv_ref, o_ref, *, dh):
    kv = kv_ref[0].astype(BF16)
    nh = q_ref.shape[-1] // dh
    for h in range(nh):
        q = (q_ref[0, :, h * dh:(h + 1) * dh] * dh ** -0.5).astype(BF16)
        s = _dot_nt(q, kv[:, h * dh:(h + 1) * dh])
        e = jnp.exp(s - jnp.max(s, axis=-1, keepdims=True))
        p = e / jnp.sum(e, axis=-1, keepdims=True)
        o = _dot(p.astype(BF16), kv[:, (nh + h) * dh:(nh + h + 1) * dh])
        o_ref[0, :, h * dh:(h + 1) * dh] = o.astype(o_ref.dtype)


def _mem_attn(q, kv, tq=512):
    b, t, d = q.shape
    m = kv.shape[1]
    tq = min(tq, t)
    return pl.pallas_call(
        functools.partial(_mem_attn_kernel, dh=d // MEM_HEADS),
        grid=(b, t // tq),
        in_specs=[pl.BlockSpec((1, tq, d), lambda i, j: (i, j, 0)),
                  pl.BlockSpec((1, m, 2 * d), lambda i, j: (i, 0, 0))],
        out_specs=pl.BlockSpec((1, tq, d), lambda i, j: (i, j, 0)),
        out_shape=jax.ShapeDtypeStruct((b, t, d), BF16),
        compiler_params=_params(("parallel", "arbitrary")),
        name="mem_attn",
    )(q, kv)


def _final_norm_kernel(x_ref, g_ref, o_ref):
    o_ref[...] = _rms(x_ref[...], g_ref[...])


def _final_norm(x, g):
    n, d = x.shape
    tm = min(512, n)
    return pl.pallas_call(
        _final_norm_kernel,
        grid=(n // tm,),
        in_specs=[pl.BlockSpec((tm, d), lambda i: (i, 0)), pl.BlockSpec((1, d), lambda i: (0, 0))],
        out_specs=pl.BlockSpec((tm, d), lambda i: (i, 0)),
        out_shape=jax.ShapeDtypeStruct((n, d), F32),
        compiler_params=_params(("parallel",)),
        name="final_norm",
    )(x, g.reshape(1, d))


PAGES = 16
TP = 8


def _page_specs(block, npg):
    def imap(i, j, pt, k):
        return (pt[i, j * npg + k],) + (0,) * len(block)
    return [pl.BlockSpec((1,) + block, functools.partial(imap, k=k)) for k in range(npg)]


def _paged_call(kern, pt, pools, pool_block, others, other_specs, out_specs, out_shape, scratch, name):
    b, n_pages = pt.shape
    npg = min(PAGES, n_pages)
    assert n_pages % npg == 0
    in_specs = []
    args = []
    for pool in pools:
        in_specs += _page_specs(pool_block, npg)
        args += [pool] * npg
    in_specs += other_specs
    args += others
    gs = pltpu.PrefetchScalarGridSpec(num_scalar_prefetch=1, grid=(b, n_pages // npg), in_specs=in_specs,
                                      out_specs=out_specs, scratch_shapes=scratch)
    return pl.pallas_call(functools.partial(kern, npg=npg), grid_spec=gs, out_shape=out_shape,
                          compiler_params=_params(("arbitrary", "arbitrary")), name=name)(pt, *args)


def _cumsum_paged_kernel(pt_ref, *refs, npg):
    pages = refs[:npg]
    row_ref, carry_ref = refs[npg], refs[npg + 1]
    j = pl.program_id(1)

    @pl.when(j == 0)
    def _():
        carry_ref[...] = jnp.zeros_like(carry_ref)

    psz = pages[0].shape[-1]
    r_i = lax.broadcasted_iota(jnp.int32, (psz, psz), 0)
    c_i = lax.broadcasted_iota(jnp.int32, (psz, psz), 1)
    tri = jnp.where(r_i <= c_i, 1.0, 0.0).astype(BF16)
    for k in range(npg):
        x = jnp.concatenate([pages[k][0], jnp.zeros((8 - FOX_HEADS, psz), F32)], axis=0)
        a, b, c = _split3(x)
        cs = _dot(a, tri) + _dot(b, tri) + _dot(c, tri) + carry_ref[...]
        carry_ref[...] = jnp.broadcast_to(cs[:, psz - 1:psz], carry_ref.shape)
        row_ref[0, :, k * psz:(k + 1) * psz] = cs


def _cumsum_paged(pt, logf_pool_t):
    b, n_pages = pt.shape
    psz = logf_pool_t.shape[-1]
    npg = min(PAGES, n_pages)
    return _paged_call(
        _cumsum_paged_kernel, pt, [logf_pool_t], (FOX_HEADS, psz), [], [],
        pl.BlockSpec((1, 8, npg * psz), lambda i, j, pt: (i, 0, j)),
        jax.ShapeDtypeStruct((b, 8, n_pages * psz), F32),
        [pltpu.VMEM((8, psz), F32)], "cumsum_paged")


def _compress_paged_kernel(pt_ref, *refs, npg):
    pages = refs[:npg]
    w1_ref, w2_ref, f_ref, s_ref, a_ref = refs[npg:npg + 5]
    psz = pages[0].shape[1]
    spp = psz // CMP_STRIDE
    r_i = lax.broadcasted_iota(jnp.int32, (psz, psz), 0)
    c_i = lax.broadcasted_iota(jnp.int32, (psz, psz), 1)
    perm = jnp.where(c_i == CMP_STRIDE * (r_i % spp) + r_i // spp, 1.0, 0.0).astype(BF16)
    for k in range(npg):
        xp = _dot(perm, pages[k][0].astype(BF16)).astype(BF16)
        for p in range(CMP_STRIDE):
            a_ref[k * spp:(k + 1) * spp, p * 256:(p + 1) * 256] = xp[p * spp:(p + 1) * spp, :]
    a = a_ref[...]
    f_ref[0] = _dot(a, w1_ref[...])
    s_ref[0] = _dot(a, w2_ref[...])


def _compress_paged(pt, pool, w1, w2):
    b, n_pages = pt.shape
    psz = pool.shape[1]
    spp = psz // CMP_STRIDE
    npg = min(PAGES, n_pages)
    nseg = n_pages * spp
    wspec = pl.BlockSpec(w1.shape, lambda i, j, pt: (0, 0))
    ospec = pl.BlockSpec((1, npg * spp, 256), lambda i, j, pt: (i, j, 0))
    return _paged_call(
        _compress_paged_kernel, pt, [pool], (psz, 256), [w1, w2], [wspec, wspec],
        [ospec, ospec],
        [jax.ShapeDtypeStruct((b, nseg, 256), F32)] * 2,
        [pltpu.VMEM((npg * spp, CMP_STRIDE * 256), BF16)], "compress_paged")


def _nsa_sample_cmp_kernel(qn_ref, f_ref, s_ref, cb_ref, oc_ref, ps_ref, *, nseg, pos0, n_new):
    scale = HEAD_DIM ** -0.5
    tq = qn_ref.shape[1]
    qpos = pos0 + jnp.minimum(lax.broadcasted_iota(jnp.int32, (tq, 1), 0), n_new - 1)
    kv = f_ref[0] + pltpu.roll(s_ref[0], nseg - 1, 0) + cb_ref[0:1, :]
    kcmp = kv[:, 0:LANES].astype(BF16)
    vcmp = kv[:, LANES:2 * LANES].astype(BF16)
    for g in range(NSA_KV_HEADS):
        qn = _stack_rows(qn_ref.at[:, :, g * NSA_GROUP * LANES:(g + 1) * NSA_GROUP * LANES], NSA_GROUP, scale)
        o_c, psum = _cmp_branch(qn, kcmp, vcmp, qpos, nseg, nseg - 1)
        for i in range(NSA_GROUP):
            hblk = g * NSA_GROUP + i
            oc_ref[0, :, hblk * LANES:(hblk + 1) * LANES] = o_c[i]
        ps_ref[0, g] = psum


def _nsa_sample_cmp(qn, f, s, cb, pos0, n_new):
    b, tq, _ = qn.shape
    nseg = f.shape[1]
    fspec = pl.BlockSpec((1, nseg, 256), lambda i: (i, 0, 0))
    return pl.pallas_call(
        functools.partial(_nsa_sample_cmp_kernel, nseg=nseg, pos0=pos0, n_new=n_new),
        grid=(b,),
        in_specs=[pl.BlockSpec((1, tq, NSA_HEADS * LANES), lambda i: (i, 0, 0)), fspec, fspec,
                  pl.BlockSpec((8, 256), lambda i: (0, 0))],
        out_specs=[pl.BlockSpec((1, tq, NSA_HEADS * LANES), lambda i: (i, 0, 0)),
                   pl.BlockSpec((1, NSA_KV_HEADS, tq, nseg), lambda i: (i, 0, 0, 0))],
        out_shape=[jax.ShapeDtypeStruct((b, tq, NSA_HEADS * LANES), F32),
                   jax.ShapeDtypeStruct((b, NSA_KV_HEADS, tq, nseg), F32)],
        compiler_params=_params(("parallel",)),
        name="nsa_sample_cmp",
    )(qn, f, s, cb)


def _select_rows_kernel(ps_ref, sel_ref, sc_ref, *, nb, pos0, tq, n_new):
    rows = ps_ref.shape[0]
    qpos = pos0 + jnp.minimum(lax.broadcasted_iota(jnp.int32, (rows, 1), 0) % tq, n_new - 1)
    sel_ref[...] = _select_blocks(ps_ref[...], qpos, nb, sc_ref)


def _select_rows(psum, nb, pos0, tq, n_new):
    rows, ncmp = psum.shape
    nbp = -(-nb // LANES) * LANES
    return pl.pallas_call(
        functools.partial(_select_rows_kernel, nb=nb, pos0=pos0, tq=tq, n_new=n_new),
        grid=(1,),
        in_specs=[pl.BlockSpec((rows, ncmp), lambda i: (0, 0))],
        out_specs=pl.BlockSpec((rows, nbp), lambda i: (0, 0)),
        out_shape=jax.ShapeDtypeStruct((rows, nbp), F32),
        scratch_shapes=[pltpu.VMEM((nbp, rows), F32)],
        compiler_params=_params(("arbitrary",)),
        name="select_rows",
    )(psum)


def _state_update(st_refs, g, carry_fn):
    m_ref, l_ref, acc_ref = st_refs
    m, l, acc = carry_fn((m_ref[g], l_ref[g], acc_ref[g]))
    m_ref[g] = m
    l_ref[g] = l
    acc_ref[g] = acc


def _state_init(st_refs):
    m_ref, l_ref, acc_ref = st_refs
    m_ref[...] = jnp.full(m_ref.shape, NEG_INF, F32)
    l_ref[...] = jnp.zeros(l_ref.shape, F32)
    acc_ref[...] = jnp.zeros(acc_ref.shape, F32)


def _state_scratch(groups, r, tq):
    return [pltpu.VMEM((groups, r, tq, 1), F32), pltpu.VMEM((groups, r, tq, 1), F32),
            pltpu.VMEM((groups, r, tq, LANES), F32)]


def _nsa_sample_kernel(pt_ref, *refs, npg, pos0, n_new):
    pages = refs[:npg]
    (qr_ref, sel_ref, new_ref, wst_ref, wnew_ref, oc_ref, gate_ref, o_ref,
     m_ref, l_ref, acc_ref, k16, v16) = refs[npg:]
    st = (m_ref, l_ref, acc_ref)
    j = pl.program_id(1)
    tq = qr_ref.shape[1]
    psz = pages[0].shape[1]
    tk = npg * psz
    scale = HEAD_DIM ** -0.5
    qpos = pos0 + jnp.minimum(lax.broadcasted_iota(jnp.int32, (tq, 1), 0), n_new - 1)

    @pl.when(j == 0)
    def _():
        _state_init(st)

    for k in range(npg):
        k16[k * psz:(k + 1) * psz, :] = pages[k][0, :, 0:LANES].astype(BF16)
        v16[k * psz:(k + 1) * psz, :] = pages[k][0, :, LANES:2 * LANES].astype(BF16)
    k0 = j * tk
    kpos = k0 + lax.broadcasted_iota(jnp.int32, (1, tk), 1)
    qrs = []
    for g in range(NSA_KV_HEADS):
        qr = _stack_rows(qr_ref.at[:, :, g * NSA_GROUP * LANES:(g + 1) * NSA_GROUP * LANES], NSA_GROUP, scale)
        qrs.append(qr)
        mask = _expand_blocks(sel_ref[0, g], k0, tk) & (kpos <= qpos)
        s = _dot_nt(qr, k16[...]).reshape(NSA_GROUP, tq, tk)
        _state_update(st, g, functools.partial(_online_update, s, mask[None], v16[...]))

    @pl.when(j == pl.num_programs(1) - 1)
    def _():
        nrow = new_ref.shape[1]
        r_i = lax.broadcasted_iota(jnp.int32, (1, nrow), 1)
        npos = pos0 + r_i
        m_new = (r_i < n_new) & (npos <= qpos)
        wlen = wst_ref.shape[1]
        wpos = pos0 - wlen + lax.broadcasted_iota(jnp.int32, (1, wlen), 1)
        m_old = (wpos > qpos - WINDOW) & (wpos >= 0)
        m_wnew = m_new & (npos > qpos - WINDOW)
        kn = new_ref[0, :, 0:LANES].astype(BF16)
        vn = new_ref[0, :, LANES:2 * LANES].astype(BF16)
        kwo = wst_ref[0, :, 0:LANES].astype(BF16)
        vwo = wst_ref[0, :, LANES:2 * LANES].astype(BF16)
        kwn = wnew_ref[0, :, 0:LANES].astype(BF16)
        vwn = wnew_ref[0, :, LANES:2 * LANES].astype(BF16)
        for g in range(NSA_KV_HEADS):
            qr = qrs[g]
            blk_ok = _expand_blocks(sel_ref[0, g], pos0, nrow)
            s = _dot_nt(qr, kn).reshape(NSA_GROUP, tq, nrow)
            _state_update(st, g, functools.partial(_online_update, s, (m_new & blk_ok)[None], vn))
            o_s = _online_final((m_ref[g], l_ref[g], acc_ref[g]))
            cw = _online_init(NSA_GROUP, tq)
            cw = _online_update(_dot_nt(qr, kwo).reshape(NSA_GROUP, tq, wlen), m_old[None], vwo, cw)
            cw = _online_update(_dot_nt(qr, kwn).reshape(NSA_GROUP, tq, nrow), m_wnew[None], vwn, cw)
            o_w = _online_final(cw)
            gates = gate_ref[0, :, g * LANES:(g + 1) * LANES]
            for i in range(NSA_GROUP):
                hblk = g * NSA_GROUP + i
                o_c = oc_ref[0, :, hblk * LANES:(hblk + 1) * LANES]
                o = (gates[:, 3 * i:3 * i + 1] * o_c + gates[:, 3 * i + 1:3 * i + 2] * o_s[i]
                     + gates[:, 3 * i + 2:3 * i + 3] * o_w[i])
                o_ref[0, :, hblk * LANES:(hblk + 1) * LANES] = o.astype(o_ref.dtype)


def _nsa_sample(pt, pool, qr, sel, new, wst, wnew, oc, gate, pos0, n_new):
    b, n_pages = pt.shape
    psz = pool.shape[1]
    npg = min(PAGES, n_pages)
    tq = qr.shape[1]
    nbp = sel.shape[-1]
    wlen = wst.shape[1]
    full = lambda shp: pl.BlockSpec((1,) + shp, lambda i, j, pt: (i,) + (0,) * len(shp))
    return _paged_call(
        functools.partial(_nsa_sample_kernel, pos0=pos0, n_new=n_new), pt, [pool], (psz, 256),
        [qr, sel, new, wst, wnew, oc, gate],
        [full((tq, NSA_HEADS * LANES)), full((NSA_KV_HEADS, tq, nbp)), full((tq, 256)), full((wlen, 256)),
         full((tq, 256)), full((tq, NSA_HEADS * LANES)), full((tq, 3 * LANES))],
        full((tq, NSA_HEADS * LANES)),
        jax.ShapeDtypeStruct((b, tq, NSA_HEADS * LANES), BF16),
        _state_scratch(NSA_KV_HEADS, NSA_GROUP, tq)
        + [pltpu.VMEM((npg * psz, LANES), BF16)] * 2, "nsa_sample")


def _diff_sample_kernel(pt_ref, *refs, npg, pos0, n_new, lam_init):
    pages = refs[:npg]
    q_ref, new_ref, dl_ref, gain_ref, o_ref, m_ref, l_ref, acc_ref, k16, v16 = refs[npg:]
    st = (m_ref, l_ref, acc_ref)
    j = pl.program_id(1)
    tq = q_ref.shape[1]
    psz = pages[0].shape[1]
    tk = npg * psz
    qpos = pos0 + jnp.minimum(lax.broadcasted_iota(jnp.int32, (tq, 1), 0), n_new - 1)

    @pl.when(j == 0)
    def _():
        _state_init(st)

    kpos = j * tk + lax.broadcasted_iota(jnp.int32, (1, tk), 1)
    mask = (kpos <= qpos)[None]
    qs = []
    for kb in range(2):
        for k in range(npg):
            k16[k * psz:(k + 1) * psz, :] = pages[k][0, :, kb * LANES:(kb + 1) * LANES].astype(BF16)
            v16[k * psz:(k + 1) * psz, :] = pages[k][0, :, (2 + kb) * LANES:(3 + kb) * LANES].astype(BF16)
        q = _stack_rows(q_ref.at[:, :, kb * 4 * LANES:(kb + 1) * 4 * LANES], 4, DIFF_HALF ** -0.5)
        qs.append(q)
        s = _dot_nt(q, k16[...]).reshape(4, tq, tk)
        _state_update(st, kb, functools.partial(_online_update, s, mask, v16[...]))

    @pl.when(j == pl.num_programs(1) - 1)
    def _():
        nrow = new_ref.shape[1]
        r_i = lax.broadcasted_iota(jnp.int32, (1, nrow), 1)
        m_new = ((r_i < n_new) & (pos0 + r_i <= qpos))[None]
        lam = _diff_lambda(dl_ref, lam_init)
        for kb in range(2):
            kn = new_ref[0, :, kb * LANES:(kb + 1) * LANES].astype(BF16)
            vn = new_ref[0, :, (2 + kb) * LANES:(3 + kb) * LANES].astype(BF16)
            s = _dot_nt(qs[kb], kn).reshape(4, tq, nrow)
            _state_update(st, kb, functools.partial(_online_update, s, m_new, vn))
            o = _online_final((m_ref[kb], l_ref[kb], acc_ref[kb]))
            _diff_finish(o, lam, gain_ref, lam_init, o_ref.at[:, :, kb * 2 * LANES:(kb + 1) * 2 * LANES])


def _diff_sample(pt, pool, dq, new, dl, gain, lam_init, pos0, n_new):
    b, n_pages = pt.shape
    psz = pool.shape[1]
    npg = min(PAGES, n_pages)
    tq = dq.shape[1]
    full = lambda shp: pl.BlockSpec((1,) + shp, lambda i, j, pt: (i,) + (0,) * len(shp))
    const = lambda shp: pl.BlockSpec(shp, lambda i, j, pt: (0,) * len(shp))
    return _paged_call(
        functools.partial(_diff_sample_kernel, pos0=pos0, n_new=n_new, lam_init=lam_init), pt, [pool], (psz, 512),
        [dq, new, dl, _gain_lanes(gain)],
        [full((tq, 8 * LANES)), full((tq, 512)), const((4, DIFF_HALF)), const((1, LANES))],
        full((tq, 4 * LANES)),
        jax.ShapeDtypeStruct((b, tq, 4 * LANES), BF16),
        _state_scratch(2, 4, tq) + [pltpu.VMEM((npg * psz, LANES), BF16)] * 2, "diff_sample")


def _fox_sample_kernel(pt_ref, *refs, npg, pos0, n_new):
    pages = refs[:npg]
    q_ref, new_ref, cr_ref, tot_ref, lf_ref, o_ref, m_ref, l_ref, acc_ref, k16, v16 = refs[npg:]
    st = (m_ref, l_ref, acc_ref)
    j = pl.program_id(1)
    tq = q_ref.shape[1]
    psz = pages[0].shape[1]
    tk = npg * psz
    qpos = pos0 + jnp.minimum(lax.broadcasted_iota(jnp.int32, (tq, 1), 0), n_new - 1)

    @pl.when(j == 0)
    def _():
        _state_init(st)

    lf = lf_ref[0]
    row = lax.broadcasted_iota(jnp.int32, (tq, 1), 0)
    cnew = jnp.zeros_like(lf)
    for t in range(n_new):
        cnew = cnew + jnp.where(row >= t, lf[t:t + 1, :], 0.0)
    psz_l = tot_ref.shape[-1]
    kpos = j * tk + lax.broadcasted_iota(jnp.int32, (1, tk), 1)
    mask = (kpos <= qpos)[None]
    qs = []
    for kb in range(2):
        for k in range(npg):
            k16[k * psz:(k + 1) * psz, :] = pages[k][0, :, kb * LANES:(kb + 1) * LANES].astype(BF16)
            v16[k * psz:(k + 1) * psz, :] = pages[k][0, :, (2 + kb) * LANES:(3 + kb) * LANES].astype(BF16)
        q = _stack_rows(q_ref.at[:, :, kb * 2 * LANES:(kb + 1) * 2 * LANES], 2, HEAD_DIM ** -0.5)
        qs.append(q)
        cq = jnp.stack([tot_ref[0, 2 * kb + hh:2 * kb + hh + 1, psz_l - 1:psz_l]
                        + cnew[:, 2 * kb + hh:2 * kb + hh + 1] for hh in range(2)])
        ck = cr_ref[0, 2 * kb:2 * kb + 2, :][:, None, :]
        s = _dot_nt(q, k16[...]).reshape(2, tq, tk) + cq - ck
        _state_update(st, kb, functools.partial(_online_update, s, mask, v16[...]))

    @pl.when(j == pl.num_programs(1) - 1)
    def _():
        nrow = new_ref.shape[1]
        r_i = lax.broadcasted_iota(jnp.int32, (1, nrow), 1)
        m_new = ((r_i < n_new) & (pos0 + r_i <= qpos))[None]
        for kb in range(2):
            kn = new_ref[0, :, kb * LANES:(kb + 1) * LANES].astype(BF16)
            vn = new_ref[0, :, (2 + kb) * LANES:(3 + kb) * LANES].astype(BF16)
            bias = []
            for hh in range(2):
                h = 2 * kb + hh
                d = jnp.zeros((tq, nrow), F32)
                for t in range(n_new):
                    d = d + jnp.where((row >= t) & (r_i < t), lf[t:t + 1, h:h + 1], 0.0)
                bias.append(d)
            s = _dot_nt(qs[kb], kn).reshape(2, tq, nrow) + jnp.stack(bias)
            _state_update(st, kb, functools.partial(_online_update, s, m_new, vn))
            o = _online_final((m_ref[kb], l_ref[kb], acc_ref[kb]))
            for hh in range(2):
                hblk = 2 * kb + hh
                o_ref[0, :, hblk * LANES:(hblk + 1) * LANES] = o[hh].astype(o_ref.dtype)


def _fox_sample(pt, pool, fq, new, crow, lf, pos0, n_new):
    b, n_pages = pt.shape
    psz = pool.shape[1]
    npg = min(PAGES, n_pages)
    tq = fq.shape[1]
    past = crow.shape[-1]
    full = lambda shp: pl.BlockSpec((1,) + shp, lambda i, j, pt: (i,) + (0,) * len(shp))
    return _paged_call(
        functools.partial(_fox_sample_kernel, pos0=pos0, n_new=n_new), pt, [pool], (psz, 512),
        [fq, new, crow, crow, lf],
        [full((tq, 4 * LANES)), full((tq, 512)),
         pl.BlockSpec((1, 8, npg * psz), lambda i, j, pt: (i, 0, j)),
         pl.BlockSpec((1, 8, LANES), lambda i, j, pt: (i, 0, past // LANES - 1)),
         pl.BlockSpec((1, tq, LANES), lambda i, j, pt: (i, 0, 2))],
        full((tq, 4 * LANES)),
        jax.ShapeDtypeStruct((b, tq, 4 * LANES), BF16),
        _state_scratch(2, 2, tq) + [pltpu.VMEM((npg * psz, LANES), BF16)] * 2, "fox_sample")


def _wo_rows():
    import numpy as np
    rows = []
    for h in range(NSA_HEADS):
        r = np.full(LANES, -1, np.int64)
        g = h // NSA_GROUP
        r[64 * g:64 * g + 64] = 64 * h + np.arange(64)
        rows.append(r)
    for base, nh in ((NSA_HEADS * HEAD_DIM, DIFF_HEADS), ((NSA_HEADS + DIFF_HEADS) * HEAD_DIM, FOX_HEADS)):
        for h in range(nh):
            r = np.full(LANES, -1, np.int64)
            r[64 * (h % 2):64 * (h % 2) + 64] = base + 64 * h + np.arange(64)
            rows.append(r)
    return np.concatenate(rows)


def _extend_wo(w):
    import numpy as np
    idx = _wo_rows()
    return jnp.where(jnp.asarray(idx >= 0)[:, None], w[jnp.asarray(np.maximum(idx, 0))], 0.0).astype(BF16)


def _mem_and_peer(x2, bsz, kv, lw):
    n, d = x2.shape
    q = _linear(x2, lw["w_mq"], gain=lw["g_mem"])
    t = n // bsz
    if t < TP:
        qp = jnp.pad(q.reshape(bsz, t, d), ((0, 0), (0, TP - t), (0, 0)))
        om = _mem_attn(qp, kv)[:, :t]
    else:
        om = _mem_attn(q.reshape(bsz, t, d), kv)
    x2 = _linear(om.reshape(n, d), lw["w_mo"], res=x2)
    return _peer(x2, lw["g_ffn"], lw["peer_wq"], lw["peer_keys"], lw["peer_u"], lw["peer_vt"])


def _prompt_layer(x2, bsz, mem_prompt, lw, tabs):
    n, d = x2.shape
    t = n // bsz
    outs = _inproj(x2, lw["g_attn"], lw["w_ext"], lw["b_ext"], tabs[0], tabs[1], 256)
    o = {name: v.reshape(bsz, t, -1) for (name, _, _), v in zip(_SEGS, outs)}
    f, s, cb = _compress(o["cmp"].reshape(bsz, t // CMP_STRIDE, CMP_STRIDE * 256), *lw["cmp_w"])
    o_nsa = _nsa_prompt(o["qn"], o["qr"], f, s, cb, o["sel"], o["win"], o["gate"])
    o_diff = _diff_prompt(o["dq"], o["diff"], lw["diff_lambda"], lw["diff_gain"], lw["lam_init"])
    ccol, crow = _cumsum(o["gate"])
    o_fox = _fox_prompt(o["fq"], o["fox"], ccol, crow)
    mixed = jnp.concatenate([o_nsa, o_diff, o_fox], axis=-1).reshape(n, -1)
    x2 = _linear(mixed, lw["w_o_ext"], res=x2)
    m = mem_prompt.shape[1]
    mkv = _linear(mem_prompt.reshape(bsz * m, d), lw["w_mkv"])
    x2 = _mem_and_peer(x2, bsz, mkv.reshape(bsz, m, 2 * d), lw)
    return x2, o, mkv, cb


def _sample_layer(x2, bsz, pt, pools, wstate, mem_kv, lw, tabs, cb, pos0):
    n, d = x2.shape
    t = n // bsz
    outs = _inproj(x2, lw["g_attn"], lw["w_ext"], lw["b_ext"], tabs[0], tabs[1], n)
    o = {name: v.reshape(bsz, t, -1) for (name, _, _), v in zip(_SEGS, outs)}
    op = {name: jnp.pad(v, ((0, 0), (0, TP - t), (0, 0))) for name, v in o.items()}
    cmp_pool, sel_pool, diff_pool, fox_pool, logf_pool_t = pools
    f, s = _compress_paged(pt, cmp_pool, lw["cmp_w"][0], lw["cmp_w"][1])
    nseg = f.shape[1]
    oc, ps = _nsa_sample_cmp(op["qn"], f, s, cb, pos0, t)
    nb = -(-(pos0 + t) // SEL_BLOCK)
    sel = _select_rows(ps.reshape(bsz * NSA_KV_HEADS * TP, nseg), nb, pos0, TP, t)
    sel = sel.reshape(bsz, NSA_KV_HEADS, TP, -1)
    o_nsa = _nsa_sample(pt, sel_pool, op["qr"], sel, op["sel"], wstate, op["win"], oc, op["gate"], pos0, t)
    o_diff = _diff_sample(pt, diff_pool, op["dq"], op["diff"], lw["diff_lambda"], lw["diff_gain"], lw["lam_init"],
                          pos0, t)
    crow = _cumsum_paged(pt, logf_pool_t)
    o_fox = _fox_sample(pt, fox_pool, op["fq"], op["fox"], crow, op["gate"], pos0, t)
    mixed = jnp.concatenate([o_nsa, o_diff, o_fox], axis=-1)[:, :t].reshape(n, -1)
    x2 = _linear(mixed, lw["w_o_ext"], res=x2)
    x2 = _mem_and_peer(x2, bsz, mem_kv, lw)
    return x2, o


def kernel(x_prompt, x_sample, cache_nsa_cmp_kv, cache_nsa_sel_kv, cache_diff_kv, cache_fox_kv, cache_fox_logf, state_nsa_win_kv, cache_mem_kv, page_table, mem_prompt, g_attn, w_in, b_in, nsa_pe_k, nsa_pe_v, nsa_w_ck, nsa_w_cv, diff_lambda, diff_gain, w_o, g_mem, w_mq, w_mkv, w_mo, g_ffn, peer_wq, peer_keys, peer_u, peer_v, g_final):
    bp, tp, d = x_prompt.shape
    bs, ts, _ = x_sample.shape
    depth = w_in.shape[0]
    n_pool, psz = cache_nsa_cmp_kv.shape[1:3]
    pos0 = page_table.shape[1] * psz
    assert pos0 % CMP_STRIDE == 0 and ts < CMP_STRIDE and ts <= TP
    pos_p = jnp.arange(tp, dtype=jnp.int32)
    pos_s = pos0 + (jnp.arange(bs * ts, dtype=jnp.int32) % ts)
    tabs_p = (_rope_tables(pos_p, HEAD_DIM, ROT_DIM), _rope_tables(pos_p, DIFF_HALF, DIFF_ROT))
    tabs_s = (_rope_tables(pos_s, HEAD_DIM, ROT_DIM), _rope_tables(pos_s, DIFF_HALF, DIFF_ROT))
    xp = x_prompt.reshape(bp * tp, d)
    xs = x_sample.reshape(bs * ts, d)
    names = ("cmp", "sel", "diff", "fox")
    rows_p = {k: [] for k in names + ("logf", "win", "mem")}
    rows_s = {k: [] for k in names + ("logf", "win")}
    for l in range(depth):
        w_ext, b_ext = _extend_inproj(w_in[l], b_in[l])
        lw = dict(
            g_attn=g_attn[l], w_ext=w_ext, b_ext=b_ext,
            cmp_w=_compress_weights(nsa_w_ck[l], nsa_w_cv[l], nsa_pe_k[l], nsa_pe_v[l]),
            diff_lambda=diff_lambda[l], diff_gain=diff_gain[l], lam_init=0.8 - 0.6 * math.exp(-0.3 * l),
            w_o_ext=_extend_wo(w_o[l]), g_mem=g_mem[l], w_mq=w_mq[l], w_mkv=w_mkv[l], w_mo=w_mo[l],
            g_ffn=g_ffn[l], peer_wq=peer_wq[l], peer_keys=peer_keys[l],
            peer_u=peer_u[l].astype(BF16), peer_vt=peer_v[l].T.astype(BF16))
        xp, o, mkv, cb = _prompt_layer(xp, bp, mem_prompt, lw, tabs_p)
        pools = (cache_nsa_cmp_kv[l].reshape(n_pool, psz, 256), cache_nsa_sel_kv[l].reshape(n_pool, psz, 256),
                 cache_diff_kv[l].reshape(n_pool, psz, 512), cache_fox_kv[l].reshape(n_pool, psz, 512),
                 jnp.swapaxes(cache_fox_logf[l], 1, 2))
        wstate = state_nsa_win_kv[l].reshape(bs, -1, 256)
        xs, os_ = _sample_layer(xs, bs, page_table, pools, wstate, cache_mem_kv[l].reshape(bs, -1, 2 * d), lw,
                                tabs_s, cb, pos0)
        for k in names:
            rows_p[k].append(o[k])
            rows_s[k].append(os_[k])
        rows_p["logf"].append(o["gate"][..., 2 * LANES:2 * LANES + FOX_HEADS])
        rows_s["logf"].append(os_["gate"][..., 2 * LANES:2 * LANES + FOX_HEADS])
        rows_p["win"].append(o["win"][:, -min(WINDOW, tp):])
        win_all = jnp.concatenate([wstate, os_["win"]], axis=1)
        rows_s["win"].append(win_all[:, -min(WINDOW, win_all.shape[1]):])
        rows_p["mem"].append(mkv)
    y_p = _final_norm(xp, g_final).reshape(bp, tp, d)
    y_s = _final_norm(xs, g_final).reshape(bs, ts, d)

    def st(lst, tail):
        a = jnp.stack(lst, axis=0)
        return a.reshape(a.shape[:3] + tail)

    kv2 = (2, NSA_KV_HEADS, HEAD_DIM)
    kv4 = (2, DIFF_HEADS, HEAD_DIM)
    return (y_p, y_s,
            st(rows_p["cmp"], kv2), st(rows_s["cmp"], kv2), st(rows_p["sel"], kv2), st(rows_s["sel"], kv2),
            st(rows_p["diff"], kv4), st(rows_s["diff"], kv4), st(rows_p["fox"], kv4), st(rows_s["fox"], kv4),
            st(rows_p["logf"], (FOX_HEADS,)), st(rows_s["logf"], (FOX_HEADS,)),
            st(rows_p["win"], kv2), st(rows_s["win"], kv2),
            jnp.stack(rows_p["mem"], 0).reshape(depth, bp, -1, 2, MEM_HEADS, d // MEM_HEADS))
```

```python
import functools
import math

import jax
import jax.numpy as jnp
from jax import lax
from jax.experimental import pallas as pl
from jax.experimental.pallas import tpu as pltpu

F32 = jnp.float32
BF16 = jnp.bfloat16

HEAD_DIM = 64
ROT_DIM = HEAD_DIM // 4
ROPE_THETA = 500000.0
NSA_HEADS = 8
NSA_KV_HEADS = 2
NSA_GROUP = NSA_HEADS // NSA_KV_HEADS
CMP_STRIDE = 16
CMP_BLOCK = 32
SEL_BLOCK = 64
N_SEL = 16
WINDOW = 512
DIFF_HEADS = 4
DIFF_HALF = HEAD_DIM // 2
DIFF_ROT = DIFF_HALF // 4
FOX_HEADS = 4
MEM_HEADS = 4
PEER_HEADS = 8
PEER_KEYS = 128
PEER_TOPK = 16
PEER_HALF = 128
EPS = 1e-6
NEG_INF = -1e30
FORCE_SCORE = 1e4
LANES = 128
VMEM_LIMIT = 56 * 1024 * 1024


def _params(sem, vmem=VMEM_LIMIT):
    return pltpu.CompilerParams(dimension_semantics=sem, vmem_limit_bytes=vmem)


def _dot(a, b):
    return jnp.dot(a, b, preferred_element_type=F32)


def _dot_nt(a, b):
    return lax.dot_general(a, b, (((1,), (1,)), ((), ())), preferred_element_type=F32)


def _rms(x, g):
    return x * lax.rsqrt(jnp.mean(x * x, axis=-1, keepdims=True) + EPS) * g


def _linear_kernel(*refs, has_gain, has_bias, has_res):
    it = iter(refs)
    x_ref = next(it)
    g_ref = next(it) if has_gain else None
    w_ref = next(it)
    b_ref = next(it) if has_bias else None
    r_ref = next(it) if has_res else None
    o_ref = next(it)
    xb_ref = next(it)

    @pl.when(pl.program_id(1) == 0)
    def _():
        x = x_ref[...].astype(F32)
        if has_gain:
            x = _rms(x, g_ref[...])
        xb_ref[...] = x.astype(BF16)

    y = _dot(xb_ref[...], w_ref[...])
    if has_bias:
        y = y + b_ref[...]
    if has_res:
        y = y + r_ref[...]
    o_ref[...] = y.astype(o_ref.dtype)


def _linear(x, w, gain=None, bias=None, res=None, tm=512, tn=512, out_dtype=F32):
    m, k = x.shape
    n = w.shape[1]
    tm = min(tm, m)
    tn = min(tn, n)
    assert m % tm == 0 and n % tn == 0
    args = [x]
    specs = [pl.BlockSpec((tm, k), lambda i, j: (i, 0))]
    if gain is not None:
        args.append(gain.reshape(1, k))
        specs.append(pl.BlockSpec((1, k), lambda i, j: (0, 0)))
    args.append(w.astype(BF16))
    specs.append(pl.BlockSpec((k, tn), lambda i, j: (0, j)))
    if bias is not None:
        args.append(bias.reshape(1, n))
        specs.append(pl.BlockSpec((1, tn), lambda i, j: (0, j)))
    if res is not None:
        args.append(res)
        specs.append(pl.BlockSpec((tm, tn), lambda i, j: (i, j)))
    return pl.pallas_call(
        functools.partial(_linear_kernel, has_gain=gain is not None, has_bias=bias is not None,
                          has_res=res is not None),
        grid=(m // tm, n // tn),
        in_specs=specs,
        out_specs=pl.BlockSpec((tm, tn), lambda i, j: (i, j)),
        out_shape=jax.ShapeDtypeStruct((m, n), out_dtype),
        scratch_shapes=[pltpu.VMEM((tm, k), BF16)],
        compiler_params=_params(("parallel", "arbitrary")),
        name="linear",
    )(*args)


def _peer_scores_kernel(x_ref, g_ref, wq_ref, keys_ref, xn_ref, st_ref):
    hb = _rms(x_ref[...], g_ref[...]).astype(BF16)
    xn_ref[...] = hb
    qb = _dot(hb, wq_ref[...]).astype(BF16)
    for hc in range(2 * PEER_HEADS):
        st_ref[hc] = _dot_nt(keys_ref[hc], qb[:, hc * PEER_HALF:(hc + 1) * PEER_HALF])


def _peer_scores(x, g, wq, keys):
    n, d = x.shape
    tm = min(256, n)
    nq = wq.shape[1]
    hc = 2 * PEER_HEADS
    return pl.pallas_call(
        _peer_scores_kernel,
        grid=(n // tm,),
        in_specs=[pl.BlockSpec((tm, d), lambda i: (i, 0)),
                  pl.BlockSpec((1, d), lambda i: (0, 0)),
                  pl.BlockSpec((d, nq), lambda i: (0, 0)),
                  pl.BlockSpec((hc, PEER_KEYS, PEER_HALF), lambda i: (0, 0, 0))],
        out_specs=[pl.BlockSpec((tm, d), lambda i: (i, 0)),
                   pl.BlockSpec((hc, PEER_KEYS, tm), lambda i: (0, 0, i))],
        out_shape=[jax.ShapeDtypeStruct((n, d), BF16),
                   jax.ShapeDtypeStruct((hc, PEER_KEYS, n), F32)],
        compiler_params=_params(("parallel",)),
        name="peer_scores",
    )(x, g.reshape(1, d), wq.astype(BF16), keys.reshape(hc, PEER_KEYS, PEER_HALF).astype(BF16))


_PEER_PAIRS = [(a, b) for a in range(PEER_TOPK) for b in range(PEER_TOPK) if (a + 1) * (b + 1) <= PEER_TOPK]


def _peer_topk_kernel(st_ref, t_ref, r_ref, a_ref, c_ref, rk_ref, sv_ref, av_ref, cn_ref):
    tn = st_ref.shape[-1]
    row = lax.broadcasted_iota(jnp.int32, (PEER_KEYS, tn), 0).astype(F32)
    big = float(PEER_KEYS)

    rk_ref[...] = jnp.full(rk_ref.shape, big, F32)
    for h in range(PEER_HEADS):

        def extract(t, carry, h=h):
            out = []
            for c, s in enumerate(carry):
                m = jnp.max(s, axis=0, keepdims=True)
                idx = jnp.min(jnp.where(s == m, row, big), axis=0, keepdims=True)
                hit = row == idx
                sv_ref[c, t, h:h + 1, :] = m
                rk_ref[2 * h + c] = jnp.where(hit, t.astype(F32), rk_ref[2 * h + c])
                out.append(jnp.where(hit, -jnp.inf, s))
            return tuple(out)

        lax.fori_loop(0, PEER_TOPK, extract, (st_ref[2 * h], st_ref[2 * h + 1]))

    cand = [sv_ref[0, a] + sv_ref[1, b] for a, b in _PEER_PAIRS]
    top = sv_ref[0, 0] + sv_ref[1, 0]
    sel = []
    for ia, (a, b) in enumerate(_PEER_PAIRS):
        cnt = jnp.zeros_like(top)
        for ib, (a2, b2) in enumerate(_PEER_PAIRS):
            if ib == ia:
                continue
            ahead = (cand[ib] >= cand[ia]) if (a2 * PEER_TOPK + b2) < (a * PEER_TOPK + b) else (cand[ib] > cand[ia])
            cnt = cnt + jnp.where(ahead, 1.0, 0.0)
        sel.append(jnp.where(cnt < float(PEER_TOPK), 1.0, 0.0))
    z = jnp.zeros_like(top)
    counts = [jnp.zeros_like(top) for _ in range(PEER_TOPK)]
    for ia, (a, b) in enumerate(_PEER_PAIRS):
        z = z + sel[ia] * jnp.exp(cand[ia] - top)
        counts[a] = counts[a] + sel[ia]
    for a in range(PEER_TOPK):
        av_ref[a] = jnp.exp(sv_ref[0, a] - sv_ref[0, 0]) / z
        cn_ref[a] = counts[a]

    for h in range(PEER_HEADS):
        rank1 = rk_ref[2 * h]
        rank2 = rk_ref[2 * h + 1]

        def scatter(a, carry, h=h, rank1=rank1):
            wa, ca = carry
            hit = rank1 == a.astype(F32)
            wa = jnp.where(hit, av_ref[a, h:h + 1, :], wa)
            ca = jnp.where(hit, cn_ref[a, h:h + 1, :], ca)
            return wa, ca

        zero = jnp.zeros((PEER_KEYS, tn), F32)
        wa, ca = lax.fori_loop(0, PEER_TOPK, scatter, (zero, zero))
        a_ref[h] = wa.astype(a_ref.dtype)
        c_ref[h] = ca.astype(c_ref.dtype)
        r_ref[h] = rank2.astype(r_ref.dtype)
        t_ref[h] = jnp.where(rank2 < float(PEER_TOPK), jnp.exp(st_ref[2 * h + 1] - sv_ref[1, 0, h:h + 1, :]),
                             0.0).astype(t_ref.dtype)


def _peer_topk(st):
    hc, nk, n = st.shape
    tn = LANES
    spec_h = pl.BlockSpec((PEER_HEADS, nk, tn), lambda i: (0, 0, i))
    shape_h = jax.ShapeDtypeStruct((PEER_HEADS, nk, n), F32)
    shape_b = jax.ShapeDtypeStruct((PEER_HEADS, nk, n), BF16)
    return pl.pallas_call(
        _peer_topk_kernel,
        grid=(n // tn,),
        in_specs=[pl.BlockSpec((hc, nk, tn), lambda i: (0, 0, i))],
        out_specs=[spec_h] * 4,
        out_shape=[shape_b, shape_b, shape_h, shape_h],
        scratch_shapes=[pltpu.VMEM((hc, nk, tn), F32),
                        pltpu.VMEM((2, PEER_TOPK, PEER_HEADS, tn), F32),
                        pltpu.VMEM((PEER_TOPK, PEER_HEADS, tn), F32),
                        pltpu.VMEM((PEER_TOPK, PEER_HEADS, tn), F32)],
        compiler_params=_params(("parallel",)),
        name="peer_topk",
    )(st)


def _gelu(a):
    return 0.5 * a * (1.0 + lax.erf(a * math.sqrt(0.5)))


def _peer_dense_kernel(xn_ref, u_ref, vt_ref, t_ref, r_ref, a_ref, c_ref, res_ref, o_ref, acc_ref, at_ref, h_ref,
                       *, n_tiles):
    e = pl.program_id(1)
    te = u_ref.shape[0]
    nib = te // PEER_KEYS
    tm = xn_ref.shape[0]

    @pl.when(e == 0)
    def _():
        acc_ref[...] = jnp.zeros_like(acc_ref)
        at_ref[...] = jnp.zeros_like(at_ref)
        h_ref[...] = jnp.zeros_like(h_ref)

    tile = jnp.clip(e - 1, 0, n_tiles - 1)

    def step(cur):
        prv = 1 - cur
        acc_ref[...] += _dot(vt_ref[...], h_ref[cur])
        for ib in range(nib):
            i = tile * nib + ib
            g = jnp.zeros((PEER_KEYS, tm), BF16)
            for h in range(PEER_HEADS):
                cnt = c_ref[h, pl.ds(i, 1), :].astype(BF16)
                wa = a_ref[h, pl.ds(i, 1), :].astype(BF16)
                g = g + jnp.where(r_ref[h] < cnt, t_ref[h] * wa, 0.0)
            a = at_ref[prv, ib * PEER_KEYS:(ib + 1) * PEER_KEYS, :]
            h_ref[prv, ib * PEER_KEYS:(ib + 1) * PEER_KEYS, :] = g * _gelu(a).astype(BF16)
        at_ref[cur] = _dot_nt(u_ref[...], xn_ref[...])

    @pl.when(e % 2 == 0)
    def _():
        step(0)

    @pl.when(e % 2 == 1)
    def _():
        step(1)

    @pl.when(e == pl.num_programs(1) - 1)
    def _():
        o_ref[...] = res_ref[...] + acc_ref[...].T


def _peer_dense(xn, u, vt, t, r, a, c, res, te=512):
    n, d = xn.shape
    ne = u.shape[0]
    tm = min(512, n)
    n_tiles = ne // te
    spec_h = pl.BlockSpec((PEER_HEADS, PEER_KEYS, tm), lambda i, e: (0, 0, i))
    return pl.pallas_call(
        functools.partial(_peer_dense_kernel, n_tiles=n_tiles),
        grid=(n // tm, n_tiles + 2),
        in_specs=[pl.BlockSpec((tm, d), lambda i, e: (i, 0)),
                  pl.BlockSpec((te, d), lambda i, e: (jnp.minimum(e, n_tiles - 1), 0)),
                  pl.BlockSpec((d, te), lambda i, e: (0, jnp.clip(e - 2, 0, n_tiles - 1))),
                  spec_h, spec_h, spec_h, spec_h,
                  pl.BlockSpec((tm, d), lambda i, e: (i, 0))],
        out_specs=pl.BlockSpec((tm, d), lambda i, e: (i, 0)),
        out_shape=jax.ShapeDtypeStruct((n, d), F32),
        scratch_shapes=[pltpu.VMEM((d, tm), F32), pltpu.VMEM((2, te, tm), F32), pltpu.VMEM((2, te, tm), BF16)],
        compiler_params=_params(("parallel", "arbitrary")),
        name="peer_dense",
    )(xn, u, vt, t, r, a, c, res)


def _peer(x, g, wq, keys, u_b, vt_b):
    xn, st = _peer_scores(x, g, wq, keys)
    t, r, a, c = _peer_topk(st)
    return _peer_dense(xn, u_b, vt_b, t, r, a, c, x)


_SEGS = (
    ("qn", 1024, "p" * 8),
    ("qr", 1024, "a" * 8),
    ("cmp", 256, "pp"),
    ("sel", 256, "ap"),
    ("win", 256, "ap"),
    ("dq", 1024, "b" * 8),
    ("diff", 512, "bbpp"),
    ("fq", 512, "pppp"),
    ("fox", 512, "pppp"),
    ("gate", 384, "ssl"),
)
_OFF = dict(nq=0, kc=512, vc=640, ks=768, vs=896, kw=1024, vw=1152, ng=1280, dq=1304, dk=1560, dv=1816,
            fq=2072, fk=2328, fv=2584, ff=2840)


def _inproj_columns():
    import numpy as np
    cols = []

    def blocks(n):
        return [np.full(LANES, -1, np.int64) for _ in range(n)]

    for base in ("nq", "nq"):
        bl = blocks(NSA_HEADS)
        for h in range(NSA_HEADS):
            g = h // NSA_GROUP
            bl[h][64 * g:64 * g + 64] = _OFF[base] + 64 * h + np.arange(64)
        cols += bl
    cols.append(np.arange(_OFF["kc"], _OFF["kc"] + 256))
    cols.append(np.arange(_OFF["ks"], _OFF["ks"] + 256))
    cols.append(np.arange(_OFF["kw"], _OFF["kw"] + 256))
    bl = blocks(2 * DIFF_HEADS)
    for h in range(DIFF_HEADS):
        for c in range(2):
            o = 32 * (2 * (h % 2) + c)
            bl[2 * h + c][o:o + 32] = _OFF["dq"] + 64 * h + 32 * c + np.arange(32)
    cols += bl
    cols.append(np.arange(_OFF["dk"], _OFF["dk"] + 512))
    bl = blocks(FOX_HEADS)
    for h in range(FOX_HEADS):
        o = 64 * (h % 2)
        bl[h][o:o + 64] = _OFF["fq"] + 64 * h + np.arange(64)
    cols += bl
    cols.append(np.arange(_OFF["fk"], _OFF["fk"] + 512))
    bl = blocks(3)
    for g in range(NSA_KV_HEADS):
        bl[g][0:3 * NSA_GROUP] = _OFF["ng"] + 3 * NSA_GROUP * g + np.arange(3 * NSA_GROUP)
    bl[2][0:FOX_HEADS] = _OFF["ff"] + np.arange(FOX_HEADS)
    cols += bl
    return np.concatenate(cols)


def _extend_inproj(w, b):
    import numpy as np
    idx = _inproj_columns()
    keep = jnp.asarray(idx >= 0)
    src = jnp.asarray(np.maximum(idx, 0))
    return (jnp.where(keep[None, :], w[:, src], 0.0).astype(BF16),
            jnp.where(keep, b[src], 0.0).reshape(1, -1))


def _rope_tables(pos, period, rot):
    half = rot // 2
    inv = jnp.power(jnp.float32(ROPE_THETA), -jnp.arange(half, dtype=F32) / half)
    ang = pos.astype(F32)[:, None] * inv[None, :]
    cos, sin = jnp.cos(ang), jnp.sin(ang)
    n = pos.shape[0]
    reps = LANES // period
    pad = jnp.zeros((n, period - rot), F32)
    c = jnp.concatenate([cos, cos, pad + 1.0], axis=1)
    s_up = jnp.concatenate([jnp.zeros((n, half), F32), sin, pad], axis=1)
    s_dn = jnp.concatenate([-sin, jnp.zeros((n, half), F32), pad], axis=1)
    return jnp.stack([jnp.tile(c, (1, reps)), jnp.tile(s_up, (1, reps)), jnp.tile(s_dn, (1, reps))])


def _log_sigmoid(x):
    return -(jnp.maximum(-x, 0.0) + jnp.log1p(jnp.exp(-jnp.abs(x))))


def _inproj_kernel(x_ref, g_ref, w_ref, b_ref, ta_ref, tb_ref, *out_refs):
    hb = _rms(x_ref[...], g_ref[...]).astype(BF16)
    c0 = 0
    for (name, width, kinds), o_ref in zip(_SEGS, out_refs):
        z = _dot(hb, w_ref[:, c0:c0 + width]) + b_ref[:, c0:c0 + width]
        for k, kind in enumerate(kinds):
            zk = z[:, k * LANES:(k + 1) * LANES]
            if kind == "a":
                zk = (zk * ta_ref[0] + pltpu.roll(zk, ROT_DIM // 2, 1) * ta_ref[1]
                      + pltpu.roll(zk, LANES - ROT_DIM // 2, 1) * ta_ref[2])
            elif kind == "b":
                zk = (zk * tb_ref[0] + pltpu.roll(zk, DIFF_ROT // 2, 1) * tb_ref[1]
                      + pltpu.roll(zk, LANES - DIFF_ROT // 2, 1) * tb_ref[2])
            elif kind == "s":
                zk = jax.nn.sigmoid(zk)
            elif kind == "l":
                zk = _log_sigmoid(zk)
            o_ref[:, k * LANES:(k + 1) * LANES] = zk
        c0 += width


def _inproj(x, g, w_ext, b_ext, tab_a, tab_b, tm):
    n, d = x.shape
    npos = tab_a.shape[1]
    tm = min(tm, npos)
    nt = npos // tm
    ctot = w_ext.shape[1]
    return pl.pallas_call(
        _inproj_kernel,
        grid=(n // tm,),
        in_specs=[pl.BlockSpec((tm, d), lambda i: (i, 0)),
                  pl.BlockSpec((1, d), lambda i: (0, 0)),
                  pl.BlockSpec((d, ctot), lambda i: (0, 0)),
                  pl.BlockSpec((1, ctot), lambda i: (0, 0)),
                  pl.BlockSpec((3, tm, LANES), lambda i: (0, i % nt, 0)),
                  pl.BlockSpec((3, tm, LANES), lambda i: (0, i % nt, 0))],
        out_specs=[pl.BlockSpec((tm, wd), lambda i: (i, 0)) for _, wd, _ in _SEGS],
        out_shape=[jax.ShapeDtypeStruct((n, wd), F32) for _, wd, _ in _SEGS],
        compiler_params=_params(("parallel",)),
        name="inproj",
    )(x, g.reshape(1, d), w_ext, b_ext, tab_a, tab_b)


def _softmax_masked(s, mask):
    s = jnp.where(mask, s, NEG_INF)
    e = jnp.where(mask, jnp.exp(s - jnp.max(s, axis=-1, keepdims=True)), 0.0)
    return e / jnp.maximum(jnp.sum(e, axis=-1, keepdims=True), 1e-30)


def _online_update(s, mask, v, carry, vt=False):
    m, l, acc = carry
    r, tq, tk = s.shape
    s = jnp.where(mask, s, NEG_INF)
    m_new = jnp.maximum(m, jnp.max(s, axis=-1, keepdims=True))
    alpha = jnp.exp(m - m_new)
    p = jnp.where(mask, jnp.exp(s - m_new), 0.0)
    l = alpha * l + jnp.sum(p, axis=-1, keepdims=True)
    pb = p.reshape(r * tq, tk).astype(BF16)
    pv = (_dot_nt(pb, v) if vt else _dot(pb, v)).reshape(r, tq, LANES)
    return m_new, l, alpha * acc + pv


def _online_init(r, tq, dv=LANES):
    return (jnp.full((r, tq, 1), NEG_INF, F32), jnp.zeros((r, tq, 1), F32), jnp.zeros((r, tq, dv), F32))


def _online_final(carry):
    _, l, acc = carry
    return acc / jnp.maximum(l, 1e-30)


def _stack_rows(ref, nblk, scale):
    parts = [ref[0, :, i * LANES:(i + 1) * LANES] for i in range(nblk)]
    return (jnp.concatenate(parts, axis=0) * scale).astype(BF16)


def _split3(hi):
    a = hi.astype(BF16)
    r1 = hi - a.astype(F32)
    b = r1.astype(BF16)
    c = (r1 - b.astype(F32)).astype(BF16)
    return a, b, c


def _compress_weights(w_ck, w_cv, pe_k, pe_v):
    d = HEAD_DIM
    wk = w_ck.reshape(CMP_BLOCK, d, d)
    wv = w_cv.reshape(CMP_BLOCK, d, d)
    wst = jnp.stack([wk, wk, wv, wv])
    eye = jnp.eye(4, dtype=F32)
    big = jnp.einsum("cpde,cf->pcdfe", wst, eye).reshape(CMP_BLOCK, 4 * d, 4 * d)
    w1 = big[:CMP_STRIDE].reshape(CMP_STRIDE * 4 * d, 4 * d).astype(BF16)
    w2 = big[CMP_STRIDE:].reshape(CMP_STRIDE * 4 * d, 4 * d).astype(BF16)
    pst = jnp.stack([pe_k, pe_k, pe_v, pe_v], axis=1)
    pe1 = jnp.broadcast_to(pst[:CMP_STRIDE].reshape(1, -1), (8, CMP_STRIDE * 4 * d)).astype(BF16)
    pe2 = jnp.broadcast_to(pst[CMP_STRIDE:].reshape(1, -1), (8, CMP_STRIDE * 4 * d)).astype(BF16)
    return w1, w2, pe1, pe2


def _compress_kernel(a_ref, w1_ref, w2_ref, pe1_ref, pe2_ref, f_ref, s_ref, b_ref):
    a = a_ref[0].astype(BF16)
    f_ref[0] = _dot(a, w1_ref[...])
    s_ref[0] = _dot(a, w2_ref[...])
    b_ref[...] = _dot(pe1_ref[...], w1_ref[...]) + _dot(pe2_ref[...], w2_ref[...])


def _compress(a, w1, w2, pe1, pe2):
    b, nseg, ka = a.shape
    ts = min(256, nseg)
    wspec = pl.BlockSpec((ka, 256), lambda i, j: (0, 0))
    pspec = pl.BlockSpec((8, ka), lambda i, j: (0, 0))
    ospec = pl.BlockSpec((1, ts, 256), lambda i, j: (i, j, 0))
    return pl.pallas_call(
        _compress_kernel,
        grid=(b, nseg // ts),
        in_specs=[pl.BlockSpec((1, ts, ka), lambda i, j: (i, j, 0)), wspec, wspec, pspec, pspec],
        out_specs=[ospec, ospec, pl.BlockSpec((8, 256), lambda i, j: (0, 0))],
        out_shape=[jax.ShapeDtypeStruct((b, nseg, 256), F32), jax.ShapeDtypeStruct((b, nseg, 256), F32),
                   jax.ShapeDtypeStruct((8, 256), F32)],
        compiler_params=_params(("arbitrary", "arbitrary")),
        name="compress",
    )(a, w1, w2, pe1, pe2)


def _select_blocks(psum, qpos, nb, sc_ref):
    tq, ncmp = psum.shape
    nbp = sc_ref.shape[0]
    n_i = lax.broadcasted_iota(jnp.int32, (ncmp, nbp), 0)
    j_i = lax.broadcasted_iota(jnp.int32, (ncmp, nbp), 1)
    dlt = n_i - (SEL_BLOCK // CMP_STRIDE) * j_i
    wmat = jnp.where((dlt == -1) | (dlt == 3), 1.0, jnp.where((dlt >= 0) & (dlt <= 2), 2.0, 0.0)).astype(BF16)
    p_hi = psum.astype(BF16)
    p_lo = (psum - p_hi.astype(F32)).astype(BF16)
    imp = _dot(p_hi, wmat) + _dot(p_lo, wmat)
    blk = lax.broadcasted_iota(jnp.int32, (tq, nbp), 1)
    cur = qpos // SEL_BLOCK
    forced = (blk == 0) | (blk == cur) | (blk == cur - 1)
    score = jnp.where(forced, FORCE_SCORE, jnp.where(blk <= cur, imp, NEG_INF))
    sc = score.T
    sc_ref[...] = sc
    rowi = lax.broadcasted_iota(jnp.int32, (nbp, tq), 0)

    def body(j, cnt):
        sj = sc_ref[pl.ds(j, 1), :]
        ahead = jnp.where(sj > sc, 1.0, jnp.where(sj == sc, jnp.where(j < rowi, 1.0, 0.0), 0.0))
        return cnt + ahead

    cnt = lax.fori_loop(0, nb, body, jnp.zeros((nbp, tq), F32))
    return jnp.where(cnt < float(min(N_SEL, nb)), 1.0, 0.0).T


def _expand_blocks(sel, k0, tk):
    nbp = sel.shape[1]
    j_i = lax.broadcasted_iota(jnp.int32, (nbp, tk), 0)
    s_i = lax.broadcasted_iota(jnp.int32, (nbp, tk), 1) + k0
    e = jnp.where(lax.shift_right_logical(s_i, 6) == j_i, 1.0, 0.0).astype(BF16)
    return _dot(sel.astype(BF16), e) > 0.5


def _cmp_branch(qn, kcmp, vcmp, qpos, nseg, ncmp_valid, pos0=0):
    tq = qpos.shape[0]
    n_i = lax.broadcasted_iota(jnp.int32, (1, nseg), 1)
    m_c = ((n_i * CMP_STRIDE + (CMP_BLOCK - 1) + pos0) <= qpos) & (n_i < ncmp_valid)
    s_c = _dot_nt(qn, kcmp).reshape(NSA_GROUP, tq, nseg)
    p_c = _softmax_masked(s_c, m_c[None])
    o_c = _dot(p_c.reshape(NSA_GROUP * tq, nseg).astype(BF16), vcmp).reshape(NSA_GROUP, tq, LANES)
    return o_c, jnp.sum(p_c, axis=0)


def _nsa_prompt_kernel(qn_ref, qr_ref, f_ref, s_ref, cb_ref, sel_ref, win_ref, gate_ref, o_ref,
                       ks16, vs16, kw16, vw16, sc_ref, *, tq, tk, nseg, nb):
    qi = pl.program_id(1)
    q0 = qi * tq

    @pl.when(qi == 0)
    def _():
        ks16[...] = sel_ref[0, :, 0:LANES].astype(BF16)
        vs16[...] = sel_ref[0, :, LANES:2 * LANES].astype(BF16)
        kw16[...] = win_ref[0, :, 0:LANES].astype(BF16)
        vw16[...] = win_ref[0, :, LANES:2 * LANES].astype(BF16)

    scale = HEAD_DIM ** -0.5
    qpos = q0 + lax.broadcasted_iota(jnp.int32, (tq, 1), 0)
    kv = f_ref[0] + pltpu.roll(s_ref[0], nseg - 1, 0) + cb_ref[0:1, :]
    kcmp = kv[:, 0:LANES].astype(BF16)
    vcmp = kv[:, LANES:2 * LANES].astype(BF16)
    wspan = WINDOW + tq
    kstart = pl.multiple_of(jnp.maximum(q0 - WINDOW, 0), tq)
    nkt = (q0 + tq + tk - 1) // tk

    for g in range(NSA_KV_HEADS):
        qn = _stack_rows(qn_ref.at[:, :, g * NSA_GROUP * LANES:(g + 1) * NSA_GROUP * LANES], NSA_GROUP, scale)
        o_c, psum = _cmp_branch(qn, kcmp, vcmp, qpos, nseg, nseg - 1)
        sel = _select_blocks(psum, qpos, nb, sc_ref)
        qr = _stack_rows(qr_ref.at[:, :, g * NSA_GROUP * LANES:(g + 1) * NSA_GROUP * LANES], NSA_GROUP, scale)

        def body(kt, carry, qr=qr, sel=sel):
            k0 = pl.multiple_of(kt * tk, tk)
            s = _dot_nt(qr, ks16[pl.ds(k0, tk), :]).reshape(NSA_GROUP, tq, tk)
            kpos = k0 + lax.broadcasted_iota(jnp.int32, (1, tk), 1)
            mask = _expand_blocks(sel, k0, tk) & (kpos <= qpos)
            return _online_update(s, mask[None], vs16[pl.ds(k0, tk), :], carry)

        o_s = _online_final(lax.fori_loop(0, nkt, body, _online_init(NSA_GROUP, tq)))

        kpos = kstart + lax.broadcasted_iota(jnp.int32, (1, wspan), 1)
        m_w = (kpos <= qpos) & (kpos > qpos - WINDOW)
        s_w = _dot_nt(qr, kw16[pl.ds(kstart, wspan), :]).reshape(NSA_GROUP, tq, wspan)
        p_w = _softmax_masked(s_w, m_w[None])
        o_w = _dot(p_w.reshape(NSA_GROUP * tq, wspan).astype(BF16), vw16[pl.ds(kstart, wspan), :])
        o_w = o_w.reshape(NSA_GROUP, tq, LANES)

        gates = gate_ref[0, :, g * LANES:(g + 1) * LANES]
        for i in range(NSA_GROUP):
            o = (gates[:, 3 * i:3 * i + 1] * o_c[i] + gates[:, 3 * i + 1:3 * i + 2] * o_s[i]
                 + gates[:, 3 * i + 2:3 * i + 3] * o_w[i])
            hblk = g * NSA_GROUP + i
            o_ref[0, :, hblk * LANES:(hblk + 1) * LANES] = o.astype(o_ref.dtype)


def _nsa_prompt(qn, qr, f, s, cb, sel, win, gate, tq=128, tk=512):
    b, t, _ = qn.shape
    nseg = f.shape[1]
    nb = -(-t // SEL_BLOCK)
    nbp = -(-nb // LANES) * LANES
    tk = min(tk, t)
    assert t % tk == 0 and t % tq == 0 and t >= WINDOW + tq
    qspec = pl.BlockSpec((1, tq, NSA_HEADS * LANES), lambda i, j: (i, j, 0))
    fspec = pl.BlockSpec((1, nseg, 256), lambda i, j: (i, 0, 0))
    kspec = pl.BlockSpec((1, t, 256), lambda i, j: (i, 0, 0))
    return pl.pallas_call(
        functools.partial(_nsa_prompt_kernel, tq=tq, tk=tk, nseg=nseg, nb=nb),
        grid=(b, t // tq),
        in_specs=[qspec, qspec, fspec, fspec, pl.BlockSpec((8, 256), lambda i, j: (0, 0)), kspec, kspec,
                  pl.BlockSpec((1, tq, 3 * LANES), lambda i, j: (i, j, 0))],
        out_specs=qspec,
        out_shape=jax.ShapeDtypeStruct((b, t, NSA_HEADS * LANES), BF16),
        scratch_shapes=[pltpu.VMEM((t, LANES), BF16)] * 4 + [pltpu.VMEM((nbp, tq), F32)],
        compiler_params=_params(("arbitrary", "arbitrary")),
        name="nsa_prompt",
    )(qn, qr, f, s, cb, sel, win, gate)


def _cumsum_kernel(lf_ref, col_ref, row_ref, carry_ref, *, tm):
    j = pl.program_id(1)

    @pl.when(j == 0)
    def _():
        carry_ref[...] = jnp.zeros_like(carry_ref)

    r_i = lax.broadcasted_iota(jnp.int32, (tm, tm), 0)
    c_i = lax.broadcasted_iota(jnp.int32, (tm, tm), 1)
    tri = jnp.where(c_i <= r_i, 1.0, 0.0).astype(BF16)
    a, b, c = _split3(lf_ref[0])
    cs = _dot(tri, a) + _dot(tri, b) + _dot(tri, c) + carry_ref[0:1, :]
    carry_ref[...] = jnp.broadcast_to(cs[tm - 1:tm, :], carry_ref.shape)
    sh = pltpu.roll(cs, LANES - 2, 1)
    col_ref[0, :, 0:LANES] = cs
    col_ref[0, :, LANES:2 * LANES] = sh
    row_ref[0, 0:8, :] = cs.T[0:8, :]
    row_ref[0, 8:16, :] = sh.T[0:8, :]


def _cumsum(gate, tm=512):
    b, t, _ = gate.shape
    tm = min(tm, t)
    return pl.pallas_call(
        functools.partial(_cumsum_kernel, tm=tm),
        grid=(b, t // tm),
        in_specs=[pl.BlockSpec((1, tm, LANES), lambda i, j: (i, j, 2))],
        out_specs=[pl.BlockSpec((1, tm, 2 * LANES), lambda i, j: (i, j, 0)),
                   pl.BlockSpec((1, 16, tm), lambda i, j: (i, 0, j))],
        out_shape=[jax.ShapeDtypeStruct((b, t, 2 * LANES), F32), jax.ShapeDtypeStruct((b, 16, t), F32)],
        scratch_shapes=[pltpu.VMEM((8, LANES), F32)],
        compiler_params=_params(("arbitrary", "arbitrary")),
        name="cumsum",
    )(gate)


def _diff_lambda(dl_ref, lam_init):
    dl = dl_ref[...]
    a = jnp.sum(dl[0:1] * dl[1:2], axis=-1, keepdims=True)
    b = jnp.sum(dl[2:3] * dl[3:4], axis=-1, keepdims=True)
    return jnp.exp(a) - jnp.exp(b) + lam_init


def _diff_finish(o, lam, gain_ref, lam_init, o_ref):
    lane = lax.broadcasted_iota(jnp.int32, (1, LANES), 1)
    for hh in range(2):
        w = o[2 * hh] - lam * o[2 * hh + 1]
        keep = jnp.where((lane >= hh * HEAD_DIM) & (lane < (hh + 1) * HEAD_DIM), 1.0, 0.0)
        w = w * keep
        ms = jnp.sum(w * w, axis=-1, keepdims=True) * (1.0 / HEAD_DIM)
        y = w * lax.rsqrt(ms + EPS) * gain_ref[...] * (1.0 - lam_init)
        o_ref[0, :, hh * LANES:(hh + 1) * LANES] = y.astype(o_ref.dtype)


def _diff_prompt_kernel(q_ref, k_ref, v_ref, dl_ref, gain_ref, o_ref, k16, v16, *, tq, tk, lam_init):
    qi = pl.program_id(2)
    q0 = qi * tq

    @pl.when(qi == 0)
    def _():
        k16[...] = k_ref[0].astype(BF16)
        v16[...] = v_ref[0].astype(BF16)

    qpos = q0 + lax.broadcasted_iota(jnp.int32, (tq, 1), 0)
    q = _stack_rows(q_ref, 4, DIFF_HALF ** -0.5)

    def body(kt, carry):
        k0 = pl.multiple_of(kt * tk, tk)
        s = _dot_nt(q, k16[pl.ds(k0, tk), :]).reshape(4, tq, tk)
        kpos = k0 + lax.broadcasted_iota(jnp.int32, (1, tk), 1)
        return _online_update(s, (kpos <= qpos)[None], v16[pl.ds(k0, tk), :], carry)

    o = _online_final(lax.fori_loop(0, (q0 + tq + tk - 1) // tk, body, _online_init(4, tq)))
    _diff_finish(o, _diff_lambda(dl_ref, lam_init), gain_ref, lam_init, o_ref)


def _gain_lanes(gain):
    return jnp.tile(gain.reshape(1, HEAD_DIM), (1, LANES // HEAD_DIM))


def _diff_prompt(dq, rows, dl, gain, lam_init, tq=128, tk=512):
    b, t, _ = dq.shape
    tk = min(tk, t)
    return pl.pallas_call(
        functools.partial(_diff_prompt_kernel, tq=tq, tk=tk, lam_init=lam_init),
        grid=(b, 2, t // tq),
        in_specs=[pl.BlockSpec((1, tq, 4 * LANES), lambda i, kb, j: (i, j, kb)),
                  pl.BlockSpec((1, t, LANES), lambda i, kb, j: (i, 0, kb)),
                  pl.BlockSpec((1, t, LANES), lambda i, kb, j: (i, 0, 2 + kb)),
                  pl.BlockSpec((4, DIFF_HALF), lambda i, kb, j: (0, 0)),
                  pl.BlockSpec((1, LANES), lambda i, kb, j: (0, 0))],
        out_specs=pl.BlockSpec((1, tq, 2 * LANES), lambda i, kb, j: (i, j, kb)),
        out_shape=jax.ShapeDtypeStruct((b, t, 4 * LANES), BF16),
        scratch_shapes=[pltpu.VMEM((t, LANES), BF16)] * 2,
        compiler_params=_params(("arbitrary", "arbitrary", "arbitrary")),
        name="diff_prompt",
    )(dq, rows, rows, dl, _gain_lanes(gain))


def _fox_prompt_kernel(q_ref, k_ref, v_ref, cc_ref, cr_ref, o_ref, k16, v16, *, tq, tk):
    qi = pl.program_id(2)
    q0 = qi * tq

    @pl.when(qi == 0)
    def _():
        k16[...] = k_ref[0].astype(BF16)
        v16[...] = v_ref[0].astype(BF16)

    qpos = q0 + lax.broadcasted_iota(jnp.int32, (tq, 1), 0)
    q = _stack_rows(q_ref, 2, HEAD_DIM ** -0.5)
    cq = jnp.stack([cc_ref[0, :, 0:1], cc_ref[0, :, 1:2]])

    def body(kt, carry):
        k0 = pl.multiple_of(kt * tk, tk)
        ck = cr_ref[0, 0:2, pl.ds(k0, tk)][:, None, :]
        s = _dot_nt(q, k16[pl.ds(k0, tk), :]).reshape(2, tq, tk) + cq - ck
        kpos = k0 + lax.broadcasted_iota(jnp.int32, (1, tk), 1)
        return _online_update(s, (kpos <= qpos)[None], v16[pl.ds(k0, tk), :], carry)

    o = _online_final(lax.fori_loop(0, (q0 + tq + tk - 1) // tk, body, _online_init(2, tq)))
    for hh in range(2):
        o_ref[0, :, hh * LANES:(hh + 1) * LANES] = o[hh].astype(o_ref.dtype)


def _fox_prompt(fq, rows, ccol, crow, tq=128, tk=512):
    b, t, _ = fq.shape
    tk = min(tk, t)
    return pl.pallas_call(
        functools.partial(_fox_prompt_kernel, tq=tq, tk=tk),
        grid=(b, 2, t // tq),
        in_specs=[pl.BlockSpec((1, tq, 2 * LANES), lambda i, kb, j: (i, j, kb)),
                  pl.BlockSpec((1, t, LANES), lambda i, kb, j: (i, 0, kb)),
                  pl.BlockSpec((1, t, LANES), lambda i, kb, j: (i, 0, 2 + kb)),
                  pl.BlockSpec((1, tq, LANES), lambda i, kb, j: (i, j, kb)),
                  pl.BlockSpec((1, 8, t), lambda i, kb, j: (i, kb, 0))],
        out_specs=pl.BlockSpec((1, tq, 2 * LANES), lambda i, kb, j: (i, j, kb)),
        out_shape=jax.ShapeDtypeStruct((b, t, 4 * LANES), BF16),
        scratch_shapes=[pltpu.VMEM((t, LANES), BF16)] * 2,
        compiler_params=_params(("arbitrary", "arbitrary", "arbitrary")),
        name="fox_prompt",
    )(fq, rows, rows, ccol, crow)


def _mem_attn_kernel(q_ref, kv_ref, o_ref, *, dh):
    kv = kv_ref[0].astype(BF16)
    nh = q_ref.shape[-1] // dh
    for h in range(nh):
        q = (q_ref[0, :, h * dh:(h + 1) * dh] * dh ** -0.5).astype(BF16)
        s = _dot_nt(q, kv[:, h * dh:(h + 1) * dh])
        e = jnp.exp(s - jnp.max(s, axis=-1, keepdims=True))
        p = e / jnp.sum(e, axis=-1, keepdims=True)
        o = _dot(p.astype(BF16), kv[:, (nh + h) * dh:(nh + h + 1) * dh])
        o_ref[0, :, h * dh:(h + 1) * dh] = o.astype(o_ref.dtype)


def _mem_attn(q, kv, tq=512):
    b, t, d = q.shape
    m = kv.shape[1]
    tq = min(tq, t)
    return pl.pallas_call(
        functools.partial(_mem_attn_kernel, dh=d // MEM_HEADS),
        grid=(b, t // tq),
        in_specs=[pl.BlockSpec((1, tq, d), lambda i, j: (i, j, 0)),
                  pl.BlockSpec((1, m, 2 * d), lambda i, j: (i, 0, 0))],
        out_specs=pl.BlockSpec((1, tq, d), lambda i, j: (i, j, 0)),
        out_shape=jax.ShapeDtypeStruct((b, t, d), BF16),
        compiler_params=_params(("parallel", "arbitrary")),
        name="mem_attn",
    )(q, kv)


def _final_norm_kernel(x_ref, g_ref, o_ref):
    o_ref[...] = _rms(x_ref[...], g_ref[...])


def _final_norm(x, g):
    n, d = x.shape
    tm = min(512, n)
    return pl.pallas_call(
        _final_norm_kernel,
        grid=(n // tm,),
        in_specs=[pl.BlockSpec((tm, d), lambda i: (i, 0)), pl.BlockSpec((1, d), lambda i: (0, 0))],
        out_specs=pl.BlockSpec((tm, d), lambda i: (i, 0)),
        out_shape=jax.ShapeDtypeStruct((n, d), F32),
        compiler_params=_params(("parallel",)),
        name="final_norm",
    )(x, g.reshape(1, d))


PAGES = 16
TP = 8


def _page_specs(block, npg, layer):
    def imap(i, j, pt, k):
        return (layer, pt[i, j * npg + k]) + (0,) * len(block)
    return [pl.BlockSpec((1, 1) + block, functools.partial(imap, k=k)) for k in range(npg)]


def _feature_major(cache):
    nd = cache.ndim
    t = jnp.transpose(cache, (0, 1) + tuple(range(3, nd)) + (2,))
    return t.reshape(t.shape[0], t.shape[1], -1, t.shape[-1])


def _paged_call(kern, pt, layer, pools, pool_block, others, other_specs, out_specs, out_shape, scratch, name):
    b, n_pages = pt.shape
    npg = min(PAGES, n_pages)
    assert n_pages % npg == 0
    in_specs = []
    args = []
    for pool in pools:
        in_specs += _page_specs(pool_block, npg, layer)
        args += [pool] * npg
    in_specs += other_specs
    args += others
    gs = pltpu.PrefetchScalarGridSpec(num_scalar_prefetch=1, grid=(b, n_pages // npg), in_specs=in_specs,
                                      out_specs=out_specs, scratch_shapes=scratch)
    return pl.pallas_call(functools.partial(kern, npg=npg), grid_spec=gs, out_shape=out_shape,
                          compiler_params=_params(("arbitrary", "arbitrary")), name=name)(pt, *args)


def _cumsum_paged_kernel(pt_ref, *refs, npg):
    pages = refs[:npg]
    row_ref, carry_ref = refs[npg], refs[npg + 1]
    j = pl.program_id(1)

    @pl.when(j == 0)
    def _():
        carry_ref[...] = jnp.zeros_like(carry_ref)

    psz = pages[0].shape[-1]
    r_i = lax.broadcasted_iota(jnp.int32, (psz, psz), 0)
    c_i = lax.broadcasted_iota(jnp.int32, (psz, psz), 1)
    tri = jnp.where(r_i <= c_i, 1.0, 0.0).astype(BF16)
    rows = 8 * npg
    zpad = jnp.zeros((8 - FOX_HEADS, psz), F32)
    x = jnp.concatenate([blk for k in range(npg) for blk in (pages[k][0, 0], zpad)], axis=0)
    a, b, c = _split3(x)
    local = _dot(a, tri) + _dot(b, tri) + _dot(c, tri)
    tot = jnp.broadcast_to(local[:, psz - 1:psz], (rows, psz))
    p_r = lax.broadcasted_iota(jnp.int32, (rows, rows), 0)
    p_c = lax.broadcasted_iota(jnp.int32, (rows, rows), 1)
    earlier = jnp.where((p_c < p_r) & ((p_r - p_c) % 8 == 0), 1.0, 0.0).astype(BF16)
    ta, tb, tc = _split3(tot)
    offs = _dot(earlier, ta) + _dot(earlier, tb) + _dot(earlier, tc)
    cs = local + offs + jnp.tile(carry_ref[...], (npg, 1))
    for k in range(npg):
        row_ref[0, :, k * psz:(k + 1) * psz] = cs[8 * k:8 * (k + 1), :]
    carry_ref[...] = jnp.broadcast_to(cs[rows - 8:rows, psz - 1:psz], carry_ref.shape)


def _cumsum_paged(pt, layer, logf_pool_t):
    b, n_pages = pt.shape
    psz = logf_pool_t.shape[-1]
    npg = min(PAGES, n_pages)
    return _paged_call(
        _cumsum_paged_kernel, pt, layer, [logf_pool_t], (FOX_HEADS, psz), [], [],
        pl.BlockSpec((1, 8, npg * psz), lambda i, j, pt: (i, 0, j)),
        jax.ShapeDtypeStruct((b, 8, n_pages * psz), F32),
        [pltpu.VMEM((8, psz), F32)], "cumsum_paged")


def _compress_paged_kernel(pt_ref, *refs, npg):
    pages = refs[:npg]
    w1_ref, w2_ref, f_ref, s_ref, a_ref = refs[npg:npg + 5]
    psz = pages[0].shape[-1]
    spp = psz // CMP_STRIDE
    r_i = lax.broadcasted_iota(jnp.int32, (psz, psz), 0)
    c_i = lax.broadcasted_iota(jnp.int32, (psz, psz), 1)
    perm = jnp.where(c_i == CMP_STRIDE * (r_i % spp) + r_i // spp, 1.0, 0.0).astype(BF16)
    for k in range(npg):
        xp = _dot_nt(perm, pages[k][0, 0].astype(BF16)).astype(BF16)
        for p in range(CMP_STRIDE):
            a_ref[k * spp:(k + 1) * spp, p * 256:(p + 1) * 256] = xp[p * spp:(p + 1) * spp, :]
    a = a_ref[...]
    f_ref[0] = _dot(a, w1_ref[...])
    s_ref[0] = _dot(a, w2_ref[...])


def _compress_paged(pt, layer, pool, w1, w2):
    b, n_pages = pt.shape
    psz = pool.shape[-1]
    spp = psz // CMP_STRIDE
    npg = min(PAGES, n_pages)
    nseg = n_pages * spp
    wspec = pl.BlockSpec(w1.shape, lambda i, j, pt: (0, 0))
    ospec = pl.BlockSpec((1, npg * spp, 256), lambda i, j, pt: (i, j, 0))
    return _paged_call(
        _compress_paged_kernel, pt, layer, [pool], (256, psz), [w1, w2], [wspec, wspec],
        [ospec, ospec],
        [jax.ShapeDtypeStruct((b, nseg, 256), F32)] * 2,
        [pltpu.VMEM((npg * spp, CMP_STRIDE * 256), BF16)], "compress_paged")


def _nsa_sample_cmp_kernel(qn_ref, f_ref, s_ref, cb_ref, oc_ref, ps_ref, *, nseg, pos0, n_new):
    scale = HEAD_DIM ** -0.5
    tq = qn_ref.shape[1]
    qpos = pos0 + jnp.minimum(lax.broadcasted_iota(jnp.int32, (tq, 1), 0), n_new - 1)
    kv = f_ref[0] + pltpu.roll(s_ref[0], nseg - 1, 0) + cb_ref[0:1, :]
    kcmp = kv[:, 0:LANES].astype(BF16)
    vcmp = kv[:, LANES:2 * LANES].astype(BF16)
    for g in range(NSA_KV_HEADS):
        qn = _stack_rows(qn_ref.at[:, :, g * NSA_GROUP * LANES:(g + 1) * NSA_GROUP * LANES], NSA_GROUP, scale)
        o_c, psum = _cmp_branch(qn, kcmp, vcmp, qpos, nseg, nseg - 1)
        for i in range(NSA_GROUP):
            hblk = g * NSA_GROUP + i
            oc_ref[0, :, hblk * LANES:(hblk + 1) * LANES] = o_c[i]
        ps_ref[0, g] = psum


def _nsa_sample_cmp(qn, f, s, cb, pos0, n_new):
    b, tq, _ = qn.shape
    nseg = f.shape[1]
    fspec = pl.BlockSpec((1, nseg, 256), lambda i: (i, 0, 0))
    return pl.pallas_call(
        functools.partial(_nsa_sample_cmp_kernel, nseg=nseg, pos0=pos0, n_new=n_new),
        grid=(b,),
        in_specs=[pl.BlockSpec((1, tq, NSA_HEADS * LANES), lambda i: (i, 0, 0)), fspec, fspec,
                  pl.BlockSpec((8, 256), lambda i: (0, 0))],
        out_specs=[pl.BlockSpec((1, tq, NSA_HEADS * LANES), lambda i: (i, 0, 0)),
                   pl.BlockSpec((1, NSA_KV_HEADS, tq, nseg), lambda i: (i, 0, 0, 0))],
        out_shape=[jax.ShapeDtypeStruct((b, tq, NSA_HEADS * LANES), F32),
                   jax.ShapeDtypeStruct((b, NSA_KV_HEADS, tq, nseg), F32)],
        compiler_params=_params(("parallel",)),
        name="nsa_sample_cmp",
    )(qn, f, s, cb)


def _select_rows_kernel(ps_ref, sel_ref, sc_ref, *, nb, pos0, tq, n_new):
    rows = ps_ref.shape[0]
    qpos = pos0 + jnp.minimum(lax.broadcasted_iota(jnp.int32, (rows, 1), 0) % tq, n_new - 1)
    sel_ref[...] = _select_blocks(ps_ref[...], qpos, nb, sc_ref)


def _select_rows(psum, nb, pos0, tq, n_new):
    rows, ncmp = psum.shape
    nbp = -(-nb // LANES) * LANES
    return pl.pallas_call(
        functools.partial(_select_rows_kernel, nb=nb, pos0=pos0, tq=tq, n_new=n_new),
        grid=(1,),
        in_specs=[pl.BlockSpec((rows, ncmp), lambda i: (0, 0))],
        out_specs=pl.BlockSpec((rows, nbp), lambda i: (0, 0)),
        out_shape=jax.ShapeDtypeStruct((rows, nbp), F32),
        scratch_shapes=[pltpu.VMEM((nbp, rows), F32)],
        compiler_params=_params(("arbitrary",)),
        name="select_rows",
    )(psum)


def _state_update(st_refs, g, carry_fn):
    m_ref, l_ref, acc_ref = st_refs
    m, l, acc = carry_fn((m_ref[g], l_ref[g], acc_ref[g]))
    m_ref[g] = m
    l_ref[g] = l
    acc_ref[g] = acc


def _state_init(st_refs):
    m_ref, l_ref, acc_ref = st_refs
    m_ref[...] = jnp.full(m_ref.shape, NEG_INF, F32)
    l_ref[...] = jnp.zeros(l_ref.shape, F32)
    acc_ref[...] = jnp.zeros(acc_ref.shape, F32)


def _state_scratch(groups, r, tq):
    return [pltpu.VMEM((groups, r, tq, 1), F32), pltpu.VMEM((groups, r, tq, 1), F32),
            pltpu.VMEM((groups, r, tq, LANES), F32)]


def _nsa_sample_kernel(pt_ref, *refs, npg, pos0, n_new):
    pages = refs[:npg]
    (qr_ref, sel_ref, new_ref, wst_ref, wnew_ref, oc_ref, gate_ref, o_ref,
     m_ref, l_ref, acc_ref, k16, v16) = refs[npg:]
    st = (m_ref, l_ref, acc_ref)
    j = pl.program_id(1)
    tq = qr_ref.shape[1]
    psz = pages[0].shape[-1]
    tk = npg * psz
    scale = HEAD_DIM ** -0.5
    qpos = pos0 + jnp.minimum(lax.broadcasted_iota(jnp.int32, (tq, 1), 0), n_new - 1)

    @pl.when(j == 0)
    def _():
        _state_init(st)

    for k in range(npg):
        k16[:, k * psz:(k + 1) * psz] = pages[k][0, 0, 0:LANES, :].astype(BF16)
        v16[:, k * psz:(k + 1) * psz] = pages[k][0, 0, LANES:2 * LANES, :].astype(BF16)
    k0 = j * tk
    kpos = k0 + lax.broadcasted_iota(jnp.int32, (1, tk), 1)
    qrs = []
    for g in range(NSA_KV_HEADS):
        qr = _stack_rows(qr_ref.at[:, :, g * NSA_GROUP * LANES:(g + 1) * NSA_GROUP * LANES], NSA_GROUP, scale)
        qrs.append(qr)
        mask = _expand_blocks(sel_ref[0, g], k0, tk) & (kpos <= qpos)
        s = _dot(qr, k16[...]).reshape(NSA_GROUP, tq, tk)
        _state_update(st, g, functools.partial(_online_update, s, mask[None], v16[...], vt=True))

    @pl.when(j == pl.num_programs(1) - 1)
    def _():
        nrow = new_ref.shape[1]
        r_i = lax.broadcasted_iota(jnp.int32, (1, nrow), 1)
        npos = pos0 + r_i
        m_new = (r_i < n_new) & (npos <= qpos)
        wlen = wst_ref.shape[-1]
        wpos = pos0 - wlen + lax.broadcasted_iota(jnp.int32, (1, wlen), 1)
        m_old = (wpos > qpos - WINDOW) & (wpos >= 0)
        m_wnew = m_new & (npos > qpos - WINDOW)
        kn = new_ref[0, :, 0:LANES].astype(BF16)
        vn = new_ref[0, :, LANES:2 * LANES].astype(BF16)
        kwo = wst_ref[0, 0, 0:LANES, :].astype(BF16)
        vwo = wst_ref[0, 0, LANES:2 * LANES, :].astype(BF16)
        kwn = wnew_ref[0, :, 0:LANES].astype(BF16)
        vwn = wnew_ref[0, :, LANES:2 * LANES].astype(BF16)
        for g in range(NSA_KV_HEADS):
            qr = qrs[g]
            blk_ok = _expand_blocks(sel_ref[0, g], pos0, nrow)
            s = _dot_nt(qr, kn).reshape(NSA_GROUP, tq, nrow)
            _state_update(st, g, functools.partial(_online_update, s, (m_new & blk_ok)[None], vn))
            o_s = _online_final((m_ref[g], l_ref[g], acc_ref[g]))
            cw = _online_init(NSA_GROUP, tq)
            cw = _online_update(_dot(qr, kwo).reshape(NSA_GROUP, tq, wlen), m_old[None], vwo, cw, vt=True)
            cw = _online_update(_dot_nt(qr, kwn).reshape(NSA_GROUP, tq, nrow), m_wnew[None], vwn, cw)
            o_w = _online_final(cw)
            gates = gate_ref[0, :, g * LANES:(g + 1) * LANES]
            for i in range(NSA_GROUP):
                hblk = g * NSA_GROUP + i
                o_c = oc_ref[0, :, hblk * LANES:(hblk + 1) * LANES]
                o = (gates[:, 3 * i:3 * i + 1] * o_c + gates[:, 3 * i + 1:3 * i + 2] * o_s[i]
                     + gates[:, 3 * i + 2:3 * i + 3] * o_w[i])
                o_ref[0, :, hblk * LANES:(hblk + 1) * LANES] = o.astype(o_ref.dtype)


def _nsa_sample(pt, layer, pool, qr, sel, new, wst, wnew, oc, gate, pos0, n_new):
    b, n_pages = pt.shape
    psz = pool.shape[-1]
    npg = min(PAGES, n_pages)
    tq = qr.shape[1]
    nbp = sel.shape[-1]
    wlen = wst.shape[-1]
    full = lambda shp: pl.BlockSpec((1,) + shp, lambda i, j, pt: (i,) + (0,) * len(shp))
    return _paged_call(
        functools.partial(_nsa_sample_kernel, pos0=pos0, n_new=n_new), pt, layer, [pool], (256, psz),
        [qr, sel, new, wst, wnew, oc, gate],
        [full((tq, NSA_HEADS * LANES)), full((NSA_KV_HEADS, tq, nbp)), full((tq, 256)),
         pl.BlockSpec((1, 1, 256, wlen), lambda i, j, pt: (layer, i, 0, 0)),
         full((tq, 256)), full((tq, NSA_HEADS * LANES)), full((tq, 3 * LANES))],
        full((tq, NSA_HEADS * LANES)),
        jax.ShapeDtypeStruct((b, tq, NSA_HEADS * LANES), BF16),
        _state_scratch(NSA_KV_HEADS, NSA_GROUP, tq)
        + [pltpu.VMEM((LANES, npg * psz), BF16)] * 2, "nsa_sample")


def _diff_sample_kernel(pt_ref, *refs, npg, pos0, n_new, lam_init):
    pages = refs[:npg]
    q_ref, new_ref, dl_ref, gain_ref, o_ref, m_ref, l_ref, acc_ref, k16, v16 = refs[npg:]
    st = (m_ref, l_ref, acc_ref)
    j = pl.program_id(1)
    tq = q_ref.shape[1]
    psz = pages[0].shape[-1]
    tk = npg * psz
    qpos = pos0 + jnp.minimum(lax.broadcasted_iota(jnp.int32, (tq, 1), 0), n_new - 1)

    @pl.when(j == 0)
    def _():
        _state_init(st)

    kpos = j * tk + lax.broadcasted_iota(jnp.int32, (1, tk), 1)
    mask = (kpos <= qpos)[None]
    qs = []
    for kb in range(2):
        for k in range(npg):
            k16[:, k * psz:(k + 1) * psz] = pages[k][0, 0, kb * LANES:(kb + 1) * LANES, :].astype(BF16)
            v16[:, k * psz:(k + 1) * psz] = pages[k][0, 0, (2 + kb) * LANES:(3 + kb) * LANES, :].astype(BF16)
        q = _stack_rows(q_ref.at[:, :, kb * 4 * LANES:(kb + 1) * 4 * LANES], 4, DIFF_HALF ** -0.5)
        qs.append(q)
        s = _dot(q, k16[...]).reshape(4, tq, tk)
        _state_update(st, kb, functools.partial(_online_update, s, mask, v16[...], vt=True))

    @pl.when(j == pl.num_programs(1) - 1)
    def _():
        nrow = new_ref.shape[1]
        r_i = lax.broadcasted_iota(jnp.int32, (1, nrow), 1)
        m_new = ((r_i < n_new) & (pos0 + r_i <= qpos))[None]
        lam = _diff_lambda(dl_ref, lam_init)
        for kb in range(2):
            kn = new_ref[0, :, kb * LANES:(kb + 1) * LANES].astype(BF16)
            vn = new_ref[0, :, (2 + kb) * LANES:(3 + kb) * LANES].astype(BF16)
            s = _dot_nt(qs[kb], kn).reshape(4, tq, nrow)
            _state_update(st, kb, functools.partial(_online_update, s, m_new, vn))
            o = _online_final((m_ref[kb], l_ref[kb], acc_ref[kb]))
            _diff_finish(o, lam, gain_ref, lam_init, o_ref.at[:, :, kb * 2 * LANES:(kb + 1) * 2 * LANES])


def _diff_sample(pt, layer, pool, dq, new, dl, gain, lam_init, pos0, n_new):
    b, n_pages = pt.shape
    psz = pool.shape[-1]
    npg = min(PAGES, n_pages)
    tq = dq.shape[1]
    full = lambda shp: pl.BlockSpec((1,) + shp, lambda i, j, pt: (i,) + (0,) * len(shp))
    const = lambda shp: pl.BlockSpec(shp, lambda i, j, pt: (0,) * len(shp))
    return _paged_call(
        functools.partial(_diff_sample_kernel, pos0=pos0, n_new=n_new, lam_init=lam_init), pt, layer, [pool],
        (512, psz), [dq, new, dl, _gain_lanes(gain)],
        [full((tq, 8 * LANES)), full((tq, 512)), const((4, DIFF_HALF)), const((1, LANES))],
        full((tq, 4 * LANES)),
        jax.ShapeDtypeStruct((b, tq, 4 * LANES), BF16),
        _state_scratch(2, 4, tq) + [pltpu.VMEM((LANES, npg * psz), BF16)] * 2, "diff_sample")


def _fox_sample_kernel(pt_ref, *refs, npg, pos0, n_new):
    pages = refs[:npg]
    q_ref, new_ref, cr_ref, tot_ref, lf_ref, o_ref, m_ref, l_ref, acc_ref, k16, v16 = refs[npg:]
    st = (m_ref, l_ref, acc_ref)
    j = pl.program_id(1)
    tq = q_ref.shape[1]
    psz = pages[0].shape[-1]
    tk = npg * psz
    qpos = pos0 + jnp.minimum(lax.broadcasted_iota(jnp.int32, (tq, 1), 0), n_new - 1)

    @pl.when(j == 0)
    def _():
        _state_init(st)

    lf = lf_ref[0]
    row = lax.broadcasted_iota(jnp.int32, (tq, 1), 0)
    cnew = jnp.zeros_like(lf)
    for t in range(n_new):
        cnew = cnew + jnp.where(row >= t, lf[t:t + 1, :], 0.0)
    psz_l = tot_ref.shape[-1]
    kpos = j * tk + lax.broadcasted_iota(jnp.int32, (1, tk), 1)
    mask = (kpos <= qpos)[None]
    qs = []
    for kb in range(2):
        for k in range(npg):
            k16[:, k * psz:(k + 1) * psz] = pages[k][0, 0, kb * LANES:(kb + 1) * LANES, :].astype(BF16)
            v16[:, k * psz:(k + 1) * psz] = pages[k][0, 0, (2 + kb) * LANES:(3 + kb) * LANES, :].astype(BF16)
        q = _stack_rows(q_ref.at[:, :, kb * 2 * LANES:(kb + 1) * 2 * LANES], 2, HEAD_DIM ** -0.5)
        qs.append(q)
        cq = jnp.stack([tot_ref[0, 2 * kb + hh:2 * kb + hh + 1, psz_l - 1:psz_l]
                        + cnew[:, 2 * kb + hh:2 * kb + hh + 1] for hh in range(2)])
        ck = cr_ref[0, 2 * kb:2 * kb + 2, :][:, None, :]
        s = _dot(q, k16[...]).reshape(2, tq, tk) + cq - ck
        _state_update(st, kb, functools.partial(_online_update, s, mask, v16[...], vt=True))

    @pl.when(j == pl.num_programs(1) - 1)
    def _():
        nrow = new_ref.shape[1]
        r_i = lax.broadcasted_iota(jnp.int32, (1, nrow), 1)
        m_new = ((r_i < n_new) & (pos0 + r_i <= qpos))[None]
        for kb in range(2):
            kn = new_ref[0, :, kb * LANES:(kb + 1) * LANES].astype(BF16)
            vn = new_ref[0, :, (2 + kb) * LANES:(3 + kb) * LANES].astype(BF16)
            bias = []
            for hh in range(2):
                h = 2 * kb + hh
                d = jnp.zeros((tq, nrow), F32)
                for t in range(n_new):
                    d = d + jnp.where((row >= t) & (r_i < t), lf[t:t + 1, h:h + 1], 0.0)
                bias.append(d)
            s = _dot_nt(qs[kb], kn).reshape(2, tq, nrow) + jnp.stack(bias)
            _state_update(st, kb, functools.partial(_online_update, s, m_new, vn))
            o = _online_final((m_ref[kb], l_ref[kb], acc_ref[kb]))
            for hh in range(2):
                hblk = 2 * kb + hh
                o_ref[0, :, hblk * LANES:(hblk + 1) * LANES] = o[hh].astype(o_ref.dtype)


def _fox_sample(pt, layer, pool, fq, new, crow, lf, pos0, n_new):
    b, n_pages = pt.shape
    psz = pool.shape[-1]
    npg = min(PAGES, n_pages)
    tq = fq.shape[1]
    past = crow.shape[-1]
    full = lambda shp: pl.BlockSpec((1,) + shp, lambda i, j, pt: (i,) + (0,) * len(shp))
    return _paged_call(
        functools.partial(_fox_sample_kernel, pos0=pos0, n_new=n_new), pt, layer, [pool], (512, psz),
        [fq, new, crow, crow, lf],
        [full((tq, 4 * LANES)), full((tq, 512)),
         pl.BlockSpec((1, 8, npg * psz), lambda i, j, pt: (i, 0, j)),
         pl.BlockSpec((1, 8, LANES), lambda i, j, pt: (i, 0, past // LANES - 1)),
         pl.BlockSpec((1, tq, LANES), lambda i, j, pt: (i, 0, 2))],
        full((tq, 4 * LANES)),
        jax.ShapeDtypeStruct((b, tq, 4 * LANES), BF16),
        _state_scratch(2, 2, tq) + [pltpu.VMEM((LANES, npg * psz), BF16)] * 2, "fox_sample")


def _wo_rows():
    import numpy as np
    rows = []
    for h in range(NSA_HEADS):
        r = np.full(LANES, -1, np.int64)
        g = h // NSA_GROUP
        r[64 * g:64 * g + 64] = 64 * h + np.arange(64)
        rows.append(r)
    for base, nh in ((NSA_HEADS * HEAD_DIM, DIFF_HEADS), ((NSA_HEADS + DIFF_HEADS) * HEAD_DIM, FOX_HEADS)):
        for h in range(nh):
            r = np.full(LANES, -1, np.int64)
            r[64 * (h % 2):64 * (h % 2) + 64] = base + 64 * h + np.arange(64)
            rows.append(r)
    return np.concatenate(rows)


def _extend_wo(w):
    import numpy as np
    idx = _wo_rows()
    return jnp.where(jnp.asarray(idx >= 0)[:, None], w[jnp.asarray(np.maximum(idx, 0))], 0.0).astype(BF16)


def _mem_and_peer(x2, bsz, kv, lw):
    n, d = x2.shape
    q = _linear(x2, lw["w_mq"], gain=lw["g_mem"])
    t = n // bsz
    if t < TP:
        qp = jnp.pad(q.reshape(bsz, t, d), ((0, 0), (0, TP - t), (0, 0)))
        om = _mem_attn(qp, kv)[:, :t]
    else:
        om = _mem_attn(q.reshape(bsz, t, d), kv)
    x2 = _linear(om.reshape(n, d), lw["w_mo"], res=x2)
    return _peer(x2, lw["g_ffn"], lw["peer_wq"], lw["peer_keys"], lw["peer_u"], lw["peer_vt"])


def _prompt_layer(x2, bsz, mem_prompt, lw, tabs):
    n, d = x2.shape
    t = n // bsz
    outs = _inproj(x2, lw["g_attn"], lw["w_ext"], lw["b_ext"], tabs[0], tabs[1], 256)
    o = {name: v.reshape(bsz, t, -1) for (name, _, _), v in zip(_SEGS, outs)}
    f, s, cb = _compress(o["cmp"].reshape(bsz, t // CMP_STRIDE, CMP_STRIDE * 256), *lw["cmp_w"])
    o_nsa = _nsa_prompt(o["qn"], o["qr"], f, s, cb, o["sel"], o["win"], o["gate"])
    o_diff = _diff_prompt(o["dq"], o["diff"], lw["diff_lambda"], lw["diff_gain"], lw["lam_init"])
    ccol, crow = _cumsum(o["gate"])
    o_fox = _fox_prompt(o["fq"], o["fox"], ccol, crow)
    mixed = jnp.concatenate([o_nsa, o_diff, o_fox], axis=-1).reshape(n, -1)
    x2 = _linear(mixed, lw["w_o_ext"], res=x2)
    m = mem_prompt.shape[1]
    mkv = _linear(mem_prompt.reshape(bsz * m, d), lw["w_mkv"])
    x2 = _mem_and_peer(x2, bsz, mkv.reshape(bsz, m, 2 * d), lw)
    return x2, o, mkv, cb


def _sample_layer(x2, bsz, pt, layer, pools, wstate, mem_kv, lw, tabs, cb, pos0):
    n, d = x2.shape
    t = n // bsz
    outs = _inproj(x2, lw["g_attn"], lw["w_ext"], lw["b_ext"], tabs[0], tabs[1], n)
    o = {name: v.reshape(bsz, t, -1) for (name, _, _), v in zip(_SEGS, outs)}
    op = {name: jnp.pad(v, ((0, 0), (0, TP - t), (0, 0))) for name, v in o.items()}
    cmp_pool, sel_pool, diff_pool, fox_pool, logf_pool_t = pools
    f, s = _compress_paged(pt, layer, cmp_pool, lw["cmp_w"][0], lw["cmp_w"][1])
    nseg = f.shape[1]
    oc, ps = _nsa_sample_cmp(op["qn"], f, s, cb, pos0, t)
    nb = -(-(pos0 + t) // SEL_BLOCK)
    sel = _select_rows(ps.reshape(bsz * NSA_KV_HEADS * TP, nseg), nb, pos0, TP, t)
    sel = sel.reshape(bsz, NSA_KV_HEADS, TP, -1)
    o_nsa = _nsa_sample(pt, layer, sel_pool, op["qr"], sel, op["sel"], wstate, op["win"], oc, op["gate"], pos0, t)
    o_diff = _diff_sample(pt, layer, diff_pool, op["dq"], op["diff"], lw["diff_lambda"], lw["diff_gain"],
                          lw["lam_init"], pos0, t)
    crow = _cumsum_paged(pt, layer, logf_pool_t)
    o_fox = _fox_sample(pt, layer, fox_pool, op["fq"], op["fox"], crow, op["gate"], pos0, t)
    mixed = jnp.concatenate([o_nsa, o_diff, o_fox], axis=-1)[:, :t].reshape(n, -1)
    x2 = _linear(mixed, lw["w_o_ext"], res=x2)
    x2 = _mem_and_peer(x2, bsz, mem_kv, lw)
    return x2, o


def kernel(x_prompt, x_sample, cache_nsa_cmp_kv, cache_nsa_sel_kv, cache_diff_kv, cache_fox_kv, cache_fox_logf, state_nsa_win_kv, cache_mem_kv, page_table, mem_prompt, g_attn, w_in, b_in, nsa_pe_k, nsa_pe_v, nsa_w_ck, nsa_w_cv, diff_lambda, diff_gain, w_o, g_mem, w_mq, w_mkv, w_mo, g_ffn, peer_wq, peer_keys, peer_u, peer_v, g_final):
    bp, tp, d = x_prompt.shape
    bs, ts, _ = x_sample.shape
    depth = w_in.shape[0]
    n_pool, psz = cache_nsa_cmp_kv.shape[1:3]
    pos0 = page_table.shape[1] * psz
    assert pos0 % CMP_STRIDE == 0 and ts < CMP_STRIDE and ts <= TP
    pos_p = jnp.arange(tp, dtype=jnp.int32)
    pos_s = pos0 + (jnp.arange(bs * ts, dtype=jnp.int32) % ts)
    tabs_p = (_rope_tables(pos_p, HEAD_DIM, ROT_DIM), _rope_tables(pos_p, DIFF_HALF, DIFF_ROT))
    tabs_s = (_rope_tables(pos_s, HEAD_DIM, ROT_DIM), _rope_tables(pos_s, DIFF_HALF, DIFF_ROT))
    xp = x_prompt.reshape(bp * tp, d)
    xs = x_sample.reshape(bs * ts, d)
    names = ("cmp", "sel", "diff", "fox")
    rows_p = {k: [] for k in names + ("logf", "win", "mem")}
    rows_s = {k: [] for k in names + ("logf", "win")}
    pools = tuple(_feature_major(c) for c in
                  (cache_nsa_cmp_kv, cache_nsa_sel_kv, cache_diff_kv, cache_fox_kv, cache_fox_logf))
    wstate_t = _feature_major(state_nsa_win_kv)
    for l in range(depth):
        w_ext, b_ext = _extend_inproj(w_in[l], b_in[l])
        lw = dict(
            g_attn=g_attn[l], w_ext=w_ext, b_ext=b_ext,
            cmp_w=_compress_weights(nsa_w_ck[l], nsa_w_cv[l], nsa_pe_k[l], nsa_pe_v[l]),
            diff_lambda=diff_lambda[l], diff_gain=diff_gain[l], lam_init=0.8 - 0.6 * math.exp(-0.3 * l),
            w_o_ext=_extend_wo(w_o[l]), g_mem=g_mem[l], w_mq=w_mq[l], w_mkv=w_mkv[l], w_mo=w_mo[l],
            g_ffn=g_ffn[l], peer_wq=peer_wq[l], peer_keys=peer_keys[l],
            peer_u=peer_u[l].astype(BF16), peer_vt=peer_v[l].T.astype(BF16))
        xp, o, mkv, cb = _prompt_layer(xp, bp, mem_prompt, lw, tabs_p)
        wstate = state_nsa_win_kv[l].reshape(bs, -1, 256)
        xs, os_ = _sample_layer(xs, bs, page_table, l, pools, wstate_t, cache_mem_kv[l].reshape(bs, -1, 2 * d), lw,
                                tabs_s, cb, pos0)
        for k in names:
            rows_p[k].append(o[k])
            rows_s[k].append(os_[k])
        rows_p["logf"].append(o["gate"][..., 2 * LANES:2 * LANES + FOX_HEADS])
        rows_s["logf"].append(os_["gate"][..., 2 * LANES:2 * LANES + FOX_HEADS])
        rows_p["win"].append(o["win"][:, -min(WINDOW, tp):])
        win_all = jnp.concatenate([wstate, os_["win"]], axis=1)
        rows_s["win"].append(win_all[:, -min(WINDOW, win_all.shape[1]):])
        rows_p["mem"].append(mkv)
    y_p = _final_norm(xp, g_final).reshape(bp, tp, d)
    y_s = _final_norm(xs, g_final).reshape(bs, ts, d)

    def st(lst, tail):
        a = jnp.stack(lst, axis=0)
        return a.reshape(a.shape[:3] + tail)

    kv2 = (2, NSA_KV_HEADS, HEAD_DIM)
    kv4 = (2, DIFF_HEADS, HEAD_DIM)
    return (y_p, y_s,
            st(rows_p["cmp"], kv2), st(rows_s["cmp"], kv2), st(rows_p["sel"], kv2), st(rows_s["sel"], kv2),
            st(rows_p["diff"], kv4), st(rows_s["diff"], kv4), st(rows_p["fox"], kv4), st(rows_s["fox"], kv4),
            st(rows_p["logf"], (FOX_HEADS,)), st(rows_s["logf"], (FOX_HEADS,)),
            st(rows_p["win"], kv2), st(rows_s["win"], kv2),
            jnp.stack(rows_p["mem"], 0).reshape(depth, bp, -1, 2, MEM_HEADS, d // MEM_HEADS))
```

```python
import functools
import math

import jax
import jax.numpy as jnp
from jax import lax
from jax.experimental import pallas as pl
from jax.experimental.pallas import tpu as pltpu

F32 = jnp.float32
BF16 = jnp.bfloat16

HEAD_DIM = 64
ROT_DIM = HEAD_DIM // 4
ROPE_THETA = 500000.0
NSA_HEADS = 8
NSA_KV_HEADS = 2
NSA_GROUP = NSA_HEADS // NSA_KV_HEADS
CMP_STRIDE = 16
CMP_BLOCK = 32
SEL_BLOCK = 64
N_SEL = 16
WINDOW = 512
DIFF_HEADS = 4
DIFF_HALF = HEAD_DIM // 2
DIFF_ROT = DIFF_HALF // 4
FOX_HEADS = 4
MEM_HEADS = 4
PEER_HEADS = 8
PEER_KEYS = 128
PEER_TOPK = 16
PEER_HALF = 128
EPS = 1e-6
NEG_INF = -1e30
FORCE_SCORE = 1e4
LANES = 128
VMEM_LIMIT = 56 * 1024 * 1024


def _params(sem, vmem=VMEM_LIMIT):
    return pltpu.CompilerParams(dimension_semantics=sem, vmem_limit_bytes=vmem)


def _dot(a, b):
    return jnp.dot(a, b, preferred_element_type=F32)


def _dot_nt(a, b):
    return lax.dot_general(a, b, (((1,), (1,)), ((), ())), preferred_element_type=F32)


def _rms(x, g):
    return x * lax.rsqrt(jnp.mean(x * x, axis=-1, keepdims=True) + EPS) * g


def _linear_kernel(*refs, has_gain, has_bias, has_res):
    it = iter(refs)
    x_ref = next(it)
    g_ref = next(it) if has_gain else None
    w_ref = next(it)
    b_ref = next(it) if has_bias else None
    r_ref = next(it) if has_res else None
    o_ref = next(it)
    xb_ref = next(it)

    @pl.when(pl.program_id(1) == 0)
    def _():
        x = x_ref[...].astype(F32)
        if has_gain:
            x = _rms(x, g_ref[...])
        xb_ref[...] = x.astype(BF16)

    y = _dot(xb_ref[...], w_ref[...])
    if has_bias:
        y = y + b_ref[...]
    if has_res:
        y = y + r_ref[...]
    o_ref[...] = y.astype(o_ref.dtype)


def _linear(x, w, gain=None, bias=None, res=None, tm=512, tn=512, out_dtype=F32):
    m, k = x.shape
    n = w.shape[1]
    tm = min(tm, m)
    tn = min(tn, n)
    assert m % tm == 0 and n % tn == 0
    args = [x]
    specs = [pl.BlockSpec((tm, k), lambda i, j: (i, 0))]
    if gain is not None:
        args.append(gain.reshape(1, k))
        specs.append(pl.BlockSpec((1, k), lambda i, j: (0, 0)))
    args.append(w.astype(BF16))
    specs.append(pl.BlockSpec((k, tn), lambda i, j: (0, j)))
    if bias is not None:
        args.append(bias.reshape(1, n))
        specs.append(pl.BlockSpec((1, tn), lambda i, j: (0, j)))
    if res is not None:
        args.append(res)
        specs.append(pl.BlockSpec((tm, tn), lambda i, j: (i, j)))
    return pl.pallas_call(
        functools.partial(_linear_kernel, has_gain=gain is not None, has_bias=bias is not None,
                          has_res=res is not None),
        grid=(m // tm, n // tn),
        in_specs=specs,
        out_specs=pl.BlockSpec((tm, tn), lambda i, j: (i, j)),
        out_shape=jax.ShapeDtypeStruct((m, n), out_dtype),
        scratch_shapes=[pltpu.VMEM((tm, k), BF16)],
        compiler_params=_params(("parallel", "arbitrary")),
        name="linear",
    )(*args)


def _peer_scores_kernel(x_ref, g_ref, wq_ref, keys_ref, xn_ref, st_ref):
    hb = _rms(x_ref[...], g_ref[...]).astype(BF16)
    xn_ref[...] = hb
    qb = _dot(hb, wq_ref[...]).astype(BF16)
    for hc in range(2 * PEER_HEADS):
        st_ref[hc] = _dot_nt(keys_ref[hc], qb[:, hc * PEER_HALF:(hc + 1) * PEER_HALF])


def _peer_scores(x, g, wq, keys):
    n, d = x.shape
    tm = min(256, n)
    nq = wq.shape[1]
    hc = 2 * PEER_HEADS
    return pl.pallas_call(
        _peer_scores_kernel,
        grid=(n // tm,),
        in_specs=[pl.BlockSpec((tm, d), lambda i: (i, 0)),
                  pl.BlockSpec((1, d), lambda i: (0, 0)),
                  pl.BlockSpec((d, nq), lambda i: (0, 0)),
                  pl.BlockSpec((hc, PEER_KEYS, PEER_HALF), lambda i: (0, 0, 0))],
        out_specs=[pl.BlockSpec((tm, d), lambda i: (i, 0)),
                   pl.BlockSpec((hc, PEER_KEYS, tm), lambda i: (0, 0, i))],
        out_shape=[jax.ShapeDtypeStruct((n, d), BF16),
                   jax.ShapeDtypeStruct((hc, PEER_KEYS, n), F32)],
        compiler_params=_params(("parallel",)),
        name="peer_scores",
    )(x, g.reshape(1, d), wq.astype(BF16), keys.reshape(hc, PEER_KEYS, PEER_HALF).astype(BF16))


_PEER_PAIRS = [(a, b) for a in range(PEER_TOPK) for b in range(PEER_TOPK) if (a + 1) * (b + 1) <= PEER_TOPK]


def _peer_topk_kernel(st_ref, t_ref, r_ref, a_ref, c_ref, rk_ref, sv_ref, av_ref, cn_ref):
    tn = st_ref.shape[-1]
    row = lax.broadcasted_iota(jnp.int32, (PEER_KEYS, tn), 0).astype(F32)
    big = float(PEER_KEYS)

    rk_ref[...] = jnp.full(rk_ref.shape, big, F32)
    for h in range(PEER_HEADS):

        def extract(t, carry, h=h):
            out = []
            for c, s in enumerate(carry):
                m = jnp.max(s, axis=0, keepdims=True)
                idx = jnp.min(jnp.where(s == m, row, big), axis=0, keepdims=True)
                hit = row == idx
                sv_ref[c, t, h:h + 1, :] = m
                rk_ref[2 * h + c] = jnp.where(hit, jnp.asarray(t).astype(F32), rk_ref[2 * h + c])
                out.append(jnp.where(hit, -jnp.inf, s))
            return tuple(out)

        lax.fori_loop(0, PEER_TOPK, extract, (st_ref[2 * h], st_ref[2 * h + 1]))

    cand = [sv_ref[0, a] + sv_ref[1, b] for a, b in _PEER_PAIRS]
    top = sv_ref[0, 0] + sv_ref[1, 0]
    sel = []
    for ia, (a, b) in enumerate(_PEER_PAIRS):
        cnt = jnp.zeros_like(top)
        for ib, (a2, b2) in enumerate(_PEER_PAIRS):
            if ib == ia:
                continue
            ahead = (cand[ib] >= cand[ia]) if (a2 * PEER_TOPK + b2) < (a * PEER_TOPK + b) else (cand[ib] > cand[ia])
            cnt = cnt + jnp.where(ahead, 1.0, 0.0)
        sel.append(jnp.where(cnt < float(PEER_TOPK), 1.0, 0.0))
    z = jnp.zeros_like(top)
    counts = [jnp.zeros_like(top) for _ in range(PEER_TOPK)]
    for ia, (a, b) in enumerate(_PEER_PAIRS):
        z = z + sel[ia] * jnp.exp(cand[ia] - top)
        counts[a] = counts[a] + sel[ia]
    for a in range(PEER_TOPK):
        av_ref[a] = jnp.exp(sv_ref[0, a] - sv_ref[0, 0]) / z
        cn_ref[a] = counts[a]

    for h in range(PEER_HEADS):
        rank1 = rk_ref[2 * h]
        rank2 = rk_ref[2 * h + 1]

        def scatter(a, carry, h=h, rank1=rank1):
            wa, ca = carry
            hit = rank1 == jnp.asarray(a).astype(F32)
            wa = jnp.where(hit, av_ref[a, h:h + 1, :], wa)
            ca = jnp.where(hit, cn_ref[a, h:h + 1, :], ca)
            return wa, ca

        zero = jnp.zeros((PEER_KEYS, tn), F32)
        wa, ca = lax.fori_loop(0, PEER_TOPK, scatter, (zero, zero))
        a_ref[h] = wa.astype(a_ref.dtype)
        c_ref[h] = ca.astype(c_ref.dtype)
        r_ref[h] = rank2.astype(r_ref.dtype)
        t_ref[h] = jnp.where(rank2 < float(PEER_TOPK), jnp.exp(st_ref[2 * h + 1] - sv_ref[1, 0, h:h + 1, :]),
                             0.0).astype(t_ref.dtype)


def _peer_topk(st):
    hc, nk, n = st.shape
    tn = LANES
    spec_h = pl.BlockSpec((PEER_HEADS, nk, tn), lambda i: (0, 0, i))
    shape_h = jax.ShapeDtypeStruct((PEER_HEADS, nk, n), F32)
    shape_b = jax.ShapeDtypeStruct((PEER_HEADS, nk, n), BF16)
    return pl.pallas_call(
        _peer_topk_kernel,
        grid=(n // tn,),
        in_specs=[pl.BlockSpec((hc, nk, tn), lambda i: (0, 0, i))],
        out_specs=[spec_h] * 4,
        out_shape=[shape_b, shape_b, shape_h, shape_h],
        scratch_shapes=[pltpu.VMEM((hc, nk, tn), F32),
                        pltpu.VMEM((2, PEER_TOPK, PEER_HEADS, tn), F32),
                        pltpu.VMEM((PEER_TOPK, PEER_HEADS, tn), F32),
                        pltpu.VMEM((PEER_TOPK, PEER_HEADS, tn), F32)],
        compiler_params=_params(("parallel",)),
        name="peer_topk",
    )(st)


def _gelu(a):
    return 0.5 * a * (1.0 + lax.erf(a * math.sqrt(0.5)))


def _peer_dense_kernel(xn_ref, u_ref, vt_ref, t_ref, r_ref, a_ref, c_ref, res_ref, o_ref, acc_ref, h_ref):
    e = pl.program_id(1)
    te = u_ref.shape[0]
    nib = te // PEER_KEYS
    tm = xn_ref.shape[0]
    sub = 16

    @pl.when(e == 0)
    def _():
        acc_ref[...] = jnp.zeros_like(acc_ref)

    at = _dot_nt(u_ref[...], xn_ref[...])
    for ib in range(nib):
        i = e * nib + ib
        g = jnp.zeros((PEER_KEYS // sub, sub, tm), BF16)
        for h in range(PEER_HEADS):
            cnt = jnp.broadcast_to(c_ref[h, pl.ds(i, 1), :], (sub, tm)).astype(BF16)[None]
            wa = jnp.broadcast_to(a_ref[h, pl.ds(i, 1), :], (sub, tm)).astype(BF16)[None]
            r3 = r_ref[h].reshape(PEER_KEYS // sub, sub, tm)
            t3 = t_ref[h].reshape(PEER_KEYS // sub, sub, tm)
            g = g + jnp.where(r3 < cnt, t3 * wa, 0.0)
        rows = slice(ib * PEER_KEYS, (ib + 1) * PEER_KEYS)
        h_ref[rows, :] = g.reshape(PEER_KEYS, tm) * _gelu(at[rows, :]).astype(BF16)
    acc_ref[...] += _dot(vt_ref[...], h_ref[...])

    @pl.when(e == pl.num_programs(1) - 1)
    def _():
        o_ref[...] = res_ref[...] + acc_ref[...].T


def _peer_dense(xn, u, vt, t, r, a, c, res, te=1024):
    n, d = xn.shape
    ne = u.shape[0]
    tm = min(512, n)
    spec_h = pl.BlockSpec((PEER_HEADS, PEER_KEYS, tm), lambda i, e: (0, 0, i))
    return pl.pallas_call(
        _peer_dense_kernel,
        grid=(n // tm, ne // te),
        in_specs=[pl.BlockSpec((tm, d), lambda i, e: (i, 0)),
                  pl.BlockSpec((te, d), lambda i, e: (e, 0)),
                  pl.BlockSpec((d, te), lambda i, e: (0, e)),
                  spec_h, spec_h, spec_h, spec_h,
                  pl.BlockSpec((tm, d), lambda i, e: (i, 0))],
        out_specs=pl.BlockSpec((tm, d), lambda i, e: (i, 0)),
        out_shape=jax.ShapeDtypeStruct((n, d), F32),
        scratch_shapes=[pltpu.VMEM((d, tm), F32), pltpu.VMEM((te, tm), BF16)],
        compiler_params=_params(("parallel", "arbitrary")),
        name="peer_dense",
    )(xn, u, vt, t, r, a, c, res)


def _peer(x, g, wq, keys, u_b, vt_b):
    xn, st = _peer_scores(x, g, wq, keys)
    t, r, a, c = _peer_topk(st)
    return _peer_dense(xn, u_b, vt_b, t, r, a, c, x)


_SEGS = (
    ("qn", 1024, "p" * 8),
    ("qr", 1024, "a" * 8),
    ("cmp", 256, "pp"),
    ("sel", 256, "ap"),
    ("win", 256, "ap"),
    ("dq", 1024, "b" * 8),
    ("diff", 512, "bbpp"),
    ("fq", 512, "pppp"),
    ("fox", 512, "pppp"),
    ("gate", 384, "ssl"),
)
_OFF = dict(nq=0, kc=512, vc=640, ks=768, vs=896, kw=1024, vw=1152, ng=1280, dq=1304, dk=1560, dv=1816,
            fq=2072, fk=2328, fv=2584, ff=2840)


def _inproj_columns():
    import numpy as np
    cols = []

    def blocks(n):
        return [np.full(LANES, -1, np.int64) for _ in range(n)]

    for base in ("nq", "nq"):
        bl = blocks(NSA_HEADS)
        for h in range(NSA_HEADS):
            g = h // NSA_GROUP
            bl[h][64 * g:64 * g + 64] = _OFF[base] + 64 * h + np.arange(64)
        cols += bl
    cols.append(np.arange(_OFF["kc"], _OFF["kc"] + 256))
    cols.append(np.arange(_OFF["ks"], _OFF["ks"] + 256))
    cols.append(np.arange(_OFF["kw"], _OFF["kw"] + 256))
    bl = blocks(2 * DIFF_HEADS)
    for h in range(DIFF_HEADS):
        for c in range(2):
            o = 32 * (2 * (h % 2) + c)
            bl[2 * h + c][o:o + 32] = _OFF["dq"] + 64 * h + 32 * c + np.arange(32)
    cols += bl
    cols.append(np.arange(_OFF["dk"], _OFF["dk"] + 512))
    bl = blocks(FOX_HEADS)
    for h in range(FOX_HEADS):
        o = 64 * (h % 2)
        bl[h][o:o + 64] = _OFF["fq"] + 64 * h + np.arange(64)
    cols += bl
    cols.append(np.arange(_OFF["fk"], _OFF["fk"] + 512))
    bl = blocks(3)
    for g in range(NSA_KV_HEADS):
        bl[g][0:3 * NSA_GROUP] = _OFF["ng"] + 3 * NSA_GROUP * g + np.arange(3 * NSA_GROUP)
    bl[2][0:FOX_HEADS] = _OFF["ff"] + np.arange(FOX_HEADS)
    cols += bl
    return np.concatenate(cols)


def _extend_inproj(w, b):
    import numpy as np
    idx = _inproj_columns()
    keep = jnp.asarray(idx >= 0)
    src = jnp.asarray(np.maximum(idx, 0))
    return (jnp.where(keep[None, :], w[:, src], 0.0).astype(BF16),
            jnp.where(keep, b[src], 0.0).reshape(1, -1))


def _rope_tables(pos, period, rot):
    half = rot // 2
    inv = jnp.power(jnp.float32(ROPE_THETA), -jnp.arange(half, dtype=F32) / half)
    ang = pos.astype(F32)[:, None] * inv[None, :]
    cos, sin = jnp.cos(ang), jnp.sin(ang)
    n = pos.shape[0]
    reps = LANES // period
    pad = jnp.zeros((n, period - rot), F32)
    c = jnp.concatenate([cos, cos, pad + 1.0], axis=1)
    s_up = jnp.concatenate([jnp.zeros((n, half), F32), sin, pad], axis=1)
    s_dn = jnp.concatenate([-sin, jnp.zeros((n, half), F32), pad], axis=1)
    return jnp.stack([jnp.tile(c, (1, reps)), jnp.tile(s_up, (1, reps)), jnp.tile(s_dn, (1, reps))])


def _log_sigmoid(x):
    return -(jnp.maximum(-x, 0.0) + jnp.log1p(jnp.exp(-jnp.abs(x))))


def _inproj_kernel(x_ref, g_ref, w_ref, b_ref, ta_ref, tb_ref, *out_refs):
    hb = _rms(x_ref[...], g_ref[...]).astype(BF16)
    c0 = 0
    for (name, width, kinds), o_ref in zip(_SEGS, out_refs):
        z = _dot(hb, w_ref[:, c0:c0 + width]) + b_ref[:, c0:c0 + width]
        for k, kind in enumerate(kinds):
            zk = z[:, k * LANES:(k + 1) * LANES]
            if kind == "a":
                zk = (zk * ta_ref[0] + pltpu.roll(zk, ROT_DIM // 2, 1) * ta_ref[1]
                      + pltpu.roll(zk, LANES - ROT_DIM // 2, 1) * ta_ref[2])
            elif kind == "b":
                zk = (zk * tb_ref[0] + pltpu.roll(zk, DIFF_ROT // 2, 1) * tb_ref[1]
                      + pltpu.roll(zk, LANES - DIFF_ROT // 2, 1) * tb_ref[2])
            elif kind == "s":
                zk = jax.nn.sigmoid(zk)
            elif kind == "l":
                zk = _log_sigmoid(zk)
            o_ref[:, k * LANES:(k + 1) * LANES] = zk
        c0 += width


def _inproj(x, g, w_ext, b_ext, tab_a, tab_b, tm):
    n, d = x.shape
    npos = tab_a.shape[1]
    tm = min(tm, npos)
    nt = npos // tm
    ctot = w_ext.shape[1]
    return pl.pallas_call(
        _inproj_kernel,
        grid=(n // tm,),
        in_specs=[pl.BlockSpec((tm, d), lambda i: (i, 0)),
                  pl.BlockSpec((1, d), lambda i: (0, 0)),
                  pl.BlockSpec((d, ctot), lambda i: (0, 0)),
                  pl.BlockSpec((1, ctot), lambda i: (0, 0)),
                  pl.BlockSpec((3, tm, LANES), lambda i: (0, i % nt, 0)),
                  pl.BlockSpec((3, tm, LANES), lambda i: (0, i % nt, 0))],
        out_specs=[pl.BlockSpec((tm, wd), lambda i: (i, 0)) for _, wd, _ in _SEGS],
        out_shape=[jax.ShapeDtypeStruct((n, wd), F32) for _, wd, _ in _SEGS],
        compiler_params=_params(("parallel",)),
        name="inproj",
    )(x, g.reshape(1, d), w_ext, b_ext, tab_a, tab_b)


def _softmax_masked(s, mask):
    s = jnp.where(mask, s, NEG_INF)
    e = jnp.where(mask, jnp.exp(s - jnp.max(s, axis=-1, keepdims=True)), 0.0)
    return e / jnp.maximum(jnp.sum(e, axis=-1, keepdims=True), 1e-30)


def _online_update(s, mask, v, carry, vt=False):
    m, l, acc = carry
    r, tq, tk = s.shape
    s = jnp.where(mask, s, NEG_INF)
    m_new = jnp.maximum(m, jnp.max(s, axis=-1, keepdims=True))
    alpha = jnp.exp(m - m_new)
    p = jnp.where(mask, jnp.exp(s - m_new), 0.0)
    l = alpha * l + jnp.sum(p, axis=-1, keepdims=True)
    pb = p.reshape(r * tq, tk).astype(BF16)
    pv = (_dot_nt(pb, v) if vt else _dot(pb, v)).reshape(r, tq, LANES)
    return m_new, l, alpha * acc + pv


def _mask_bias(mask):
    return jnp.where(mask, 0.0, NEG_INF)


def _online_update_fast(s, v, carry, vt=False):
    m, l, acc = carry
    r, tq, tk = s.shape
    m_new = jnp.maximum(m, jnp.max(s, axis=-1, keepdims=True))
    alpha = jnp.exp(m - m_new)
    p = jnp.exp(s - m_new)
    l = alpha * l + jnp.sum(p, axis=-1, keepdims=True)
    pb = p.reshape(r * tq, tk).astype(BF16)
    pv = (_dot_nt(pb, v) if vt else _dot(pb, v)).reshape(r, tq, LANES)
    return m_new, l, alpha * acc + pv


def _online_init(r, tq, dv=LANES):
    return (jnp.full((r, tq, 1), NEG_INF, F32), jnp.zeros((r, tq, 1), F32), jnp.zeros((r, tq, dv), F32))


def _online_final(carry):
    _, l, acc = carry
    return acc / jnp.maximum(l, 1e-30)


def _stack_rows(ref, nblk, scale):
    parts = [ref[0, :, i * LANES:(i + 1) * LANES] for i in range(nblk)]
    return (jnp.concatenate(parts, axis=0) * scale).astype(BF16)


def _split3(hi):
    a = hi.astype(BF16)
    r1 = hi - a.astype(F32)
    b = r1.astype(BF16)
    c = (r1 - b.astype(F32)).astype(BF16)
    return a, b, c


def _compress_weights(w_ck, w_cv, pe_k, pe_v):
    d = HEAD_DIM
    wk = w_ck.reshape(CMP_BLOCK, d, d)
    wv = w_cv.reshape(CMP_BLOCK, d, d)
    wst = jnp.stack([wk, wk, wv, wv])
    eye = jnp.eye(4, dtype=F32)
    big = jnp.einsum("cpde,cf->pcdfe", wst, eye).reshape(CMP_BLOCK, 4 * d, 4 * d)
    w1 = big[:CMP_STRIDE].reshape(CMP_STRIDE * 4 * d, 4 * d).astype(BF16)
    w2 = big[CMP_STRIDE:].reshape(CMP_STRIDE * 4 * d, 4 * d).astype(BF16)
    pst = jnp.stack([pe_k, pe_k, pe_v, pe_v], axis=1)
    pe1 = jnp.broadcast_to(pst[:CMP_STRIDE].reshape(1, -1), (8, CMP_STRIDE * 4 * d)).astype(BF16)
    pe2 = jnp.broadcast_to(pst[CMP_STRIDE:].reshape(1, -1), (8, CMP_STRIDE * 4 * d)).astype(BF16)
    return w1, w2, pe1, pe2


def _compress_kernel(a_ref, w1_ref, w2_ref, pe1_ref, pe2_ref, f_ref, s_ref, b_ref):
    a = a_ref[0].astype(BF16)
    f_ref[0] = _dot(a, w1_ref[...])
    s_ref[0] = _dot(a, w2_ref[...])
    b_ref[...] = _dot(pe1_ref[...], w1_ref[...]) + _dot(pe2_ref[...], w2_ref[...])


def _compress(a, w1, w2, pe1, pe2):
    b, nseg, ka = a.shape
    ts = min(256, nseg)
    wspec = pl.BlockSpec((ka, 256), lambda i, j: (0, 0))
    pspec = pl.BlockSpec((8, ka), lambda i, j: (0, 0))
    ospec = pl.BlockSpec((1, ts, 256), lambda i, j: (i, j, 0))
    return pl.pallas_call(
        _compress_kernel,
        grid=(b, nseg // ts),
        in_specs=[pl.BlockSpec((1, ts, ka), lambda i, j: (i, j, 0)), wspec, wspec, pspec, pspec],
        out_specs=[ospec, ospec, pl.BlockSpec((8, 256), lambda i, j: (0, 0))],
        out_shape=[jax.ShapeDtypeStruct((b, nseg, 256), F32), jax.ShapeDtypeStruct((b, nseg, 256), F32),
                   jax.ShapeDtypeStruct((8, 256), F32)],
        compiler_params=_params(("arbitrary", "arbitrary")),
        name="compress",
    )(a, w1, w2, pe1, pe2)


def _select_blocks(psum, qpos, nb, sc_ref):
    tq, ncmp = psum.shape
    nbp = sc_ref.shape[0]
    n_i = lax.broadcasted_iota(jnp.int32, (ncmp, nbp), 0)
    j_i = lax.broadcasted_iota(jnp.int32, (ncmp, nbp), 1)
    dlt = n_i - (SEL_BLOCK // CMP_STRIDE) * j_i
    wmat = jnp.where((dlt == -1) | (dlt == 3), 1.0, jnp.where((dlt >= 0) & (dlt <= 2), 2.0, 0.0)).astype(BF16)
    p_hi = psum.astype(BF16)
    p_lo = (psum - p_hi.astype(F32)).astype(BF16)
    imp = _dot(p_hi, wmat) + _dot(p_lo, wmat)
    blk = lax.broadcasted_iota(jnp.int32, (tq, nbp), 1)
    cur = qpos // SEL_BLOCK
    forced = (blk == 0) | (blk == cur) | (blk == cur - 1)
    score = jnp.where(forced, FORCE_SCORE, jnp.where(blk <= cur, imp, NEG_INF))
    nbr = -(-nb // 8) * 8
    sc = score.T[0:nbr, :]
    sc_ref[0:nbr, :] = sc
    rowi = lax.broadcasted_iota(jnp.int32, (nbr, tq), 0)

    def body(j, cnt):
        sj = sc_ref[pl.ds(j, 1), :]
        ahead = jnp.where(sj > sc, 1.0, jnp.where(sj == sc, jnp.where(j < rowi, 1.0, 0.0), 0.0))
        return cnt + ahead

    cnt = lax.fori_loop(0, nb, body, jnp.zeros((nbr, tq), F32))
    sel = jnp.where(cnt < float(min(N_SEL, nb)), 1.0, 0.0)
    if nbr < nbp:
        sel = jnp.concatenate([sel, jnp.zeros((nbp - nbr, tq), F32)], axis=0)
    return sel.T


def _expand_blocks(sel, k0, tk):
    nbp = sel.shape[1]
    j_i = lax.broadcasted_iota(jnp.int32, (nbp, tk), 0)
    s_i = lax.broadcasted_iota(jnp.int32, (nbp, tk), 1) + k0
    e = jnp.where(lax.shift_right_logical(s_i, 6) == j_i, 1.0, 0.0).astype(BF16)
    return _dot(sel.astype(BF16), e) > 0.5


def _cmp_branch(qn, kcmp, vcmp, qpos, nseg, ncmp_valid, pos0=0):
    tq = qpos.shape[0]
    n_i = lax.broadcasted_iota(jnp.int32, (1, nseg), 1)
    m_c = ((n_i * CMP_STRIDE + (CMP_BLOCK - 1) + pos0) <= qpos) & (n_i < ncmp_valid)
    s_c = _dot_nt(qn, kcmp).reshape(NSA_GROUP, tq, nseg)
    p_c = _softmax_masked(s_c, m_c[None])
    o_c = _dot(p_c.reshape(NSA_GROUP * tq, nseg).astype(BF16), vcmp).reshape(NSA_GROUP, tq, LANES)
    return o_c, jnp.sum(p_c, axis=0)


def _nsa_prompt_kernel(qn_ref, qr_ref, f_ref, s_ref, cb_ref, sel_ref, win_ref, gate_ref, o_ref,
                       ks16, vs16, kw16, vw16, sc_ref, *, tq, tk, nseg, nb):
    qi = pl.program_id(1)
    q0 = qi * tq

    @pl.when(qi == 0)
    def _():
        ks16[...] = sel_ref[0, :, 0:LANES].astype(BF16)
        vs16[...] = sel_ref[0, :, LANES:2 * LANES].astype(BF16)
        kw16[...] = win_ref[0, :, 0:LANES].astype(BF16)
        vw16[...] = win_ref[0, :, LANES:2 * LANES].astype(BF16)

    scale = HEAD_DIM ** -0.5
    qpos = q0 + lax.broadcasted_iota(jnp.int32, (tq, 1), 0)
    kv = f_ref[0] + pltpu.roll(s_ref[0], nseg - 1, 0) + cb_ref[0:1, :]
    kcmp = kv[:, 0:LANES].astype(BF16)
    vcmp = kv[:, LANES:2 * LANES].astype(BF16)
    wspan = WINDOW + tq
    kstart = pl.multiple_of(jnp.maximum(q0 - WINDOW, 0), tq)
    nkt = (q0 + tq + tk - 1) // tk

    for g in range(NSA_KV_HEADS):
        qn = _stack_rows(qn_ref.at[:, :, g * NSA_GROUP * LANES:(g + 1) * NSA_GROUP * LANES], NSA_GROUP, scale)
        o_c, psum = _cmp_branch(qn, kcmp, vcmp, qpos, nseg, nseg - 1)
        sel = _select_blocks(psum, qpos, nb, sc_ref)
        qr = _stack_rows(qr_ref.at[:, :, g * NSA_GROUP * LANES:(g + 1) * NSA_GROUP * LANES], NSA_GROUP, scale)

        def body(kt, carry, qr=qr, sel=sel):
            k0 = pl.multiple_of(kt * tk, tk)
            s = _dot_nt(qr, ks16[pl.ds(k0, tk), :]).reshape(NSA_GROUP, tq, tk)
            kpos = k0 + lax.broadcasted_iota(jnp.int32, (1, tk), 1)
            bias = jnp.where(_expand_blocks(sel, k0, tk), _mask_bias(kpos <= qpos), NEG_INF)
            return _online_update_fast(s + bias[None], vs16[pl.ds(k0, tk), :], carry)

        o_s = _online_final(lax.fori_loop(0, nkt, body, _online_init(NSA_GROUP, tq)))

        kpos = kstart + lax.broadcasted_iota(jnp.int32, (1, wspan), 1)
        m_w = (kpos <= qpos) & (kpos > qpos - WINDOW)
        s_w = _dot_nt(qr, kw16[pl.ds(kstart, wspan), :]).reshape(NSA_GROUP, tq, wspan)
        p_w = _softmax_masked(s_w, m_w[None])
        o_w = _dot(p_w.reshape(NSA_GROUP * tq, wspan).astype(BF16), vw16[pl.ds(kstart, wspan), :])
        o_w = o_w.reshape(NSA_GROUP, tq, LANES)

        gates = gate_ref[0, :, g * LANES:(g + 1) * LANES]
        for i in range(NSA_GROUP):
            o = (gates[:, 3 * i:3 * i + 1] * o_c[i] + gates[:, 3 * i + 1:3 * i + 2] * o_s[i]
                 + gates[:, 3 * i + 2:3 * i + 3] * o_w[i])
            hblk = g * NSA_GROUP + i
            o_ref[0, :, hblk * LANES:(hblk + 1) * LANES] = o.astype(o_ref.dtype)


def _nsa_prompt(qn, qr, f, s, cb, sel, win, gate, tq=128, tk=512):
    b, t, _ = qn.shape
    nseg = f.shape[1]
    nb = -(-t // SEL_BLOCK)
    nbp = -(-nb // LANES) * LANES
    tk = min(tk, t)
    assert t % tk == 0 and t % tq == 0 and t >= WINDOW + tq
    qspec = pl.BlockSpec((1, tq, NSA_HEADS * LANES), lambda i, j: (i, j, 0))
    fspec = pl.BlockSpec((1, nseg, 256), lambda i, j: (i, 0, 0))
    kspec = pl.BlockSpec((1, t, 256), lambda i, j: (i, 0, 0))
    return pl.pallas_call(
        functools.partial(_nsa_prompt_kernel, tq=tq, tk=tk, nseg=nseg, nb=nb),
        grid=(b, t // tq),
        in_specs=[qspec, qspec, fspec, fspec, pl.BlockSpec((8, 256), lambda i, j: (0, 0)), kspec, kspec,
                  pl.BlockSpec((1, tq, 3 * LANES), lambda i, j: (i, j, 0))],
        out_specs=qspec,
        out_shape=jax.ShapeDtypeStruct((b, t, NSA_HEADS * LANES), BF16),
        scratch_shapes=[pltpu.VMEM((t, LANES), BF16)] * 4 + [pltpu.VMEM((nbp, tq), F32)],
        compiler_params=_params(("arbitrary", "arbitrary")),
        name="nsa_prompt",
    )(qn, qr, f, s, cb, sel, win, gate)


def _cumsum_kernel(lf_ref, col_ref, row_ref, carry_ref, *, tm):
    j = pl.program_id(1)

    @pl.when(j == 0)
    def _():
        carry_ref[...] = jnp.zeros_like(carry_ref)

    r_i = lax.broadcasted_iota(jnp.int32, (tm, tm), 0)
    c_i = lax.broadcasted_iota(jnp.int32, (tm, tm), 1)
    tri = jnp.where(c_i <= r_i, 1.0, 0.0).astype(BF16)
    a, b, c = _split3(lf_ref[0])
    cs = _dot(tri, a) + _dot(tri, b) + _dot(tri, c) + carry_ref[0:1, :]
    carry_ref[...] = jnp.broadcast_to(cs[tm - 1:tm, :], carry_ref.shape)
    sh = pltpu.roll(cs, LANES - 2, 1)
    col_ref[0, :, 0:LANES] = cs
    col_ref[0, :, LANES:2 * LANES] = sh
    row_ref[0, 0:8, :] = cs.T[0:8, :]
    row_ref[0, 8:16, :] = sh.T[0:8, :]


def _cumsum(gate, tm=512):
    b, t, _ = gate.shape
    tm = min(tm, t)
    return pl.pallas_call(
        functools.partial(_cumsum_kernel, tm=tm),
        grid=(b, t // tm),
        in_specs=[pl.BlockSpec((1, tm, LANES), lambda i, j: (i, j, 2))],
        out_specs=[pl.BlockSpec((1, tm, 2 * LANES), lambda i, j: (i, j, 0)),
                   pl.BlockSpec((1, 16, tm), lambda i, j: (i, 0, j))],
        out_shape=[jax.ShapeDtypeStruct((b, t, 2 * LANES), F32), jax.ShapeDtypeStruct((b, 16, t), F32)],
        scratch_shapes=[pltpu.VMEM((8, LANES), F32)],
        compiler_params=_params(("arbitrary", "arbitrary")),
        name="cumsum",
    )(gate)


def _diff_lambda(dl_ref, lam_init):
    dl = dl_ref[...]
    a = jnp.sum(dl[0:1] * dl[1:2], axis=-1, keepdims=True)
    b = jnp.sum(dl[2:3] * dl[3:4], axis=-1, keepdims=True)
    return jnp.exp(a) - jnp.exp(b) + lam_init


def _diff_finish(o, lam, gain_ref, lam_init, o_ref):
    lane = lax.broadcasted_iota(jnp.int32, (1, LANES), 1)
    for hh in range(2):
        w = o[2 * hh] - lam * o[2 * hh + 1]
        keep = jnp.where((lane >= hh * HEAD_DIM) & (lane < (hh + 1) * HEAD_DIM), 1.0, 0.0)
        w = w * keep
        ms = jnp.sum(w * w, axis=-1, keepdims=True) * (1.0 / HEAD_DIM)
        y = w * lax.rsqrt(ms + EPS) * gain_ref[...] * (1.0 - lam_init)
        o_ref[0, :, hh * LANES:(hh + 1) * LANES] = y.astype(o_ref.dtype)


def _diff_prompt_kernel(q_ref, k_ref, v_ref, dl_ref, gain_ref, o_ref, k16, v16, *, tq, tk, lam_init):
    qi = pl.program_id(2)
    q0 = qi * tq

    @pl.when(qi == 0)
    def _():
        k16[...] = k_ref[0].astype(BF16)
        v16[...] = v_ref[0].astype(BF16)

    qpos = q0 + lax.broadcasted_iota(jnp.int32, (tq, 1), 0)
    q = _stack_rows(q_ref, 4, DIFF_HALF ** -0.5)

    def body(kt, carry):
        k0 = pl.multiple_of(kt * tk, tk)
        s = _dot_nt(q, k16[pl.ds(k0, tk), :]).reshape(4, tq, tk)
        kpos = k0 + lax.broadcasted_iota(jnp.int32, (1, tk), 1)
        return _online_update_fast(s + _mask_bias(kpos <= qpos)[None], v16[pl.ds(k0, tk), :], carry)

    o = _online_final(lax.fori_loop(0, (q0 + tq + tk - 1) // tk, body, _online_init(4, tq)))
    _diff_finish(o, _diff_lambda(dl_ref, lam_init), gain_ref, lam_init, o_ref)


def _gain_lanes(gain):
    return jnp.tile(gain.reshape(1, HEAD_DIM), (1, LANES // HEAD_DIM))


def _diff_prompt(dq, rows, dl, gain, lam_init, tq=128, tk=512):
    b, t, _ = dq.shape
    tk = min(tk, t)
    return pl.pallas_call(
        functools.partial(_diff_prompt_kernel, tq=tq, tk=tk, lam_init=lam_init),
        grid=(b, 2, t // tq),
        in_specs=[pl.BlockSpec((1, tq, 4 * LANES), lambda i, kb, j: (i, j, kb)),
                  pl.BlockSpec((1, t, LANES), lambda i, kb, j: (i, 0, kb)),
                  pl.BlockSpec((1, t, LANES), lambda i, kb, j: (i, 0, 2 + kb)),
                  pl.BlockSpec((4, DIFF_HALF), lambda i, kb, j: (0, 0)),
                  pl.BlockSpec((1, LANES), lambda i, kb, j: (0, 0))],
        out_specs=pl.BlockSpec((1, tq, 2 * LANES), lambda i, kb, j: (i, j, kb)),
        out_shape=jax.ShapeDtypeStruct((b, t, 4 * LANES), BF16),
        scratch_shapes=[pltpu.VMEM((t, LANES), BF16)] * 2,
        compiler_params=_params(("arbitrary", "arbitrary", "arbitrary")),
        name="diff_prompt",
    )(dq, rows, rows, dl, _gain_lanes(gain))


def _fox_prompt_kernel(q_ref, k_ref, v_ref, cc_ref, cr_ref, o_ref, k16, v16, *, tq, tk):
    qi = pl.program_id(2)
    q0 = qi * tq

    @pl.when(qi == 0)
    def _():
        k16[...] = k_ref[0].astype(BF16)
        v16[...] = v_ref[0].astype(BF16)

    qpos = q0 + lax.broadcasted_iota(jnp.int32, (tq, 1), 0)
    q = _stack_rows(q_ref, 2, HEAD_DIM ** -0.5)
    cq = jnp.stack([cc_ref[0, :, 0:1], cc_ref[0, :, 1:2]])

    def body(kt, carry):
        k0 = pl.multiple_of(kt * tk, tk)
        ck = cr_ref[0, 0:2, pl.ds(k0, tk)][:, None, :]
        kpos = k0 + lax.broadcasted_iota(jnp.int32, (1, tk), 1)
        s = _dot_nt(q, k16[pl.ds(k0, tk), :]).reshape(2, tq, tk) + cq - ck + _mask_bias(kpos <= qpos)[None]
        return _online_update_fast(s, v16[pl.ds(k0, tk), :], carry)

    o = _online_final(lax.fori_loop(0, (q0 + tq + tk - 1) // tk, body, _online_init(2, tq)))
    for hh in range(2):
        o_ref[0, :, hh * LANES:(hh + 1) * LANES] = o[hh].astype(o_ref.dtype)


def _fox_prompt(fq, rows, ccol, crow, tq=128, tk=512):
    b, t, _ = fq.shape
    tk = min(tk, t)
    return pl.pallas_call(
        functools.partial(_fox_prompt_kernel, tq=tq, tk=tk),
        grid=(b, 2, t // tq),
        in_specs=[pl.BlockSpec((1, tq, 2 * LANES), lambda i, kb, j: (i, j, kb)),
                  pl.BlockSpec((1, t, LANES), lambda i, kb, j: (i, 0, kb)),
                  pl.BlockSpec((1, t, LANES), lambda i, kb, j: (i, 0, 2 + kb)),
                  pl.BlockSpec((1, tq, LANES), lambda i, kb, j: (i, j, kb)),
                  pl.BlockSpec((1, 8, t), lambda i, kb, j: (i, kb, 0))],
        out_specs=pl.BlockSpec((1, tq, 2 * LANES), lambda i, kb, j: (i, j, kb)),
        out_shape=jax.ShapeDtypeStruct((b, t, 4 * LANES), BF16),
        scratch_shapes=[pltpu.VMEM((t, LANES), BF16)] * 2,
        compiler_params=_params(("arbitrary", "arbitrary", "arbitrary")),
        name="fox_prompt",
    )(fq, rows, rows, ccol, crow)


def _mem_attn_kernel(q_ref, kv_ref, o_ref, *, dh):
    kv = kv_ref[0].astype(BF16)
    nh = q_ref.shape[-1] // dh
    for h in range(nh):
        q = (q_ref[0, :, h * dh:(h + 1) * dh] * dh ** -0.5).astype(BF16)
        s = _dot_nt(q, kv[:, h * dh:(h + 1) * dh])
        e = jnp.exp(s - jnp.max(s, axis=-1, keepdims=True))
        p = e / jnp.sum(e, axis=-1, keepdims=True)
        o = _dot(p.astype(BF16), kv[:, (nh + h) * dh:(nh + h + 1) * dh])
        o_ref[0, :, h * dh:(h + 1) * dh] = o.astype(o_ref.dtype)


def _mem_attn(q, kv, tq=512):
    b, t, d = q.shape
    m = kv.shape[1]
    tq = min(tq, t)
    return pl.pallas_call(
        functools.partial(_mem_attn_kernel, dh=d // MEM_HEADS),
        grid=(b, t // tq),
        in_specs=[pl.BlockSpec((1, tq, d), lambda i, j: (i, j, 0)),
                  pl.BlockSpec((1, m, 2 * d), lambda i, j: (i, 0, 0))],
        out_specs=pl.BlockSpec((1, tq, d), lambda i, j: (i, j, 0)),
        out_shape=jax.ShapeDtypeStruct((b, t, d), BF16),
        compiler_params=_params(("parallel", "arbitrary")),
        name="mem_attn",
    )(q, kv)


def _final_norm_kernel(x_ref, g_ref, o_ref):
    o_ref[...] = _rms(x_ref[...], g_ref[...])


def _final_norm(x, g):
    n, d = x.shape
    tm = min(512, n)
    return pl.pallas_call(
        _final_norm_kernel,
        grid=(n // tm,),
        in_specs=[pl.BlockSpec((tm, d), lambda i: (i, 0)), pl.BlockSpec((1, d), lambda i: (0, 0))],
        out_specs=pl.BlockSpec((tm, d), lambda i: (i, 0)),
        out_shape=jax.ShapeDtypeStruct((n, d), F32),
        compiler_params=_params(("parallel",)),
        name="final_norm",
    )(x, g.reshape(1, d))


PAGES = 16
TP = 8


def _page_specs(block, npg, layer):
    def imap(i, j, pt, k):
        return (layer, pt[i, j * npg + k]) + (0,) * len(block)
    return [pl.BlockSpec((1, 1) + block, functools.partial(imap, k=k)) for k in range(npg)]


def _feature_major(cache):
    nd = cache.ndim
    t = jnp.transpose(cache, (0, 1) + tuple(range(3, nd)) + (2,))
    return t.reshape(t.shape[0], t.shape[1], -1, t.shape[-1])


def _paged_call(kern, pt, layer, pools, pool_block, others, other_specs, out_specs, out_shape, scratch, name):
    b, n_pages = pt.shape
    npg = min(PAGES, n_pages)
    assert n_pages % npg == 0
    in_specs = []
    args = []
    for pool in pools:
        in_specs += _page_specs(pool_block, npg, layer)
        args += [pool] * npg
    in_specs += other_specs
    args += others
    gs = pltpu.PrefetchScalarGridSpec(num_scalar_prefetch=1, grid=(b, n_pages // npg), in_specs=in_specs,
                                      out_specs=out_specs, scratch_shapes=scratch)
    return pl.pallas_call(functools.partial(kern, npg=npg), grid_spec=gs, out_shape=out_shape,
                          compiler_params=_params(("arbitrary", "arbitrary")), name=name)(pt, *args)


def _cumsum_paged_kernel(pt_ref, *refs, npg):
    pages = refs[:npg]
    row_ref, carry_ref = refs[npg], refs[npg + 1]
    j = pl.program_id(1)

    @pl.when(j == 0)
    def _():
        carry_ref[...] = jnp.zeros_like(carry_ref)

    psz = pages[0].shape[-1]
    r_i = lax.broadcasted_iota(jnp.int32, (psz, psz), 0)
    c_i = lax.broadcasted_iota(jnp.int32, (psz, psz), 1)
    tri = jnp.where(r_i <= c_i, 1.0, 0.0).astype(BF16)
    rows = 8 * npg
    zpad = jnp.zeros((8 - FOX_HEADS, psz), F32)
    x = jnp.concatenate([blk for k in range(npg) for blk in (pages[k][0, 0], zpad)], axis=0)
    a, b, c = _split3(x)
    local = _dot(a, tri) + _dot(b, tri) + _dot(c, tri)
    tot = jnp.broadcast_to(local[:, psz - 1:psz], (rows, psz))
    p_r = lax.broadcasted_iota(jnp.int32, (rows, rows), 0)
    p_c = lax.broadcasted_iota(jnp.int32, (rows, rows), 1)
    earlier = jnp.where((p_c < p_r) & ((p_r - p_c) % 8 == 0), 1.0, 0.0).astype(BF16)
    ta, tb, tc = _split3(tot)
    offs = _dot(earlier, ta) + _dot(earlier, tb) + _dot(earlier, tc)
    cs = local + offs + jnp.tile(carry_ref[...], (npg, 1))
    for k in range(npg):
        row_ref[0, :, k * psz:(k + 1) * psz] = cs[8 * k:8 * (k + 1), :]
    carry_ref[...] = jnp.broadcast_to(cs[rows - 8:rows, psz - 1:psz], carry_ref.shape)


def _cumsum_paged(pt, layer, logf_pool_t):
    b, n_pages = pt.shape
    psz = logf_pool_t.shape[-1]
    npg = min(PAGES, n_pages)
    return _paged_call(
        _cumsum_paged_kernel, pt, layer, [logf_pool_t], (FOX_HEADS, psz), [], [],
        pl.BlockSpec((1, 8, npg * psz), lambda i, j, pt: (i, 0, j)),
        jax.ShapeDtypeStruct((b, 8, n_pages * psz), F32),
        [pltpu.VMEM((8, psz), F32)], "cumsum_paged")


def _compress_paged_kernel(pt_ref, *refs, npg):
    pages = refs[:npg]
    w1_ref, w2_ref, f_ref, s_ref, a_ref = refs[npg:npg + 5]
    psz = pages[0].shape[-1]
    spp = psz // CMP_STRIDE
    r_i = lax.broadcasted_iota(jnp.int32, (psz, psz), 0)
    c_i = lax.broadcasted_iota(jnp.int32, (psz, psz), 1)
    perm = jnp.where(c_i == CMP_STRIDE * (r_i % spp) + r_i // spp, 1.0, 0.0).astype(BF16)
    for k in range(npg):
        xp = _dot_nt(perm, pages[k][0, 0].astype(BF16)).astype(BF16)
        for p in range(CMP_STRIDE):
            a_ref[k * spp:(k + 1) * spp, p * 256:(p + 1) * 256] = xp[p * spp:(p + 1) * spp, :]
    a = a_ref[...]
    f_ref[0] = _dot(a, w1_ref[...])
    s_ref[0] = _dot(a, w2_ref[...])


def _compress_paged(pt, layer, pool, w1, w2):
    b, n_pages = pt.shape
    psz = pool.shape[-1]
    spp = psz // CMP_STRIDE
    npg = min(PAGES, n_pages)
    nseg = n_pages * spp
    wspec = pl.BlockSpec(w1.shape, lambda i, j, pt: (0, 0))
    ospec = pl.BlockSpec((1, npg * spp, 256), lambda i, j, pt: (i, j, 0))
    return _paged_call(
        _compress_paged_kernel, pt, layer, [pool], (256, psz), [w1, w2], [wspec, wspec],
        [ospec, ospec],
        [jax.ShapeDtypeStruct((b, nseg, 256), F32)] * 2,
        [pltpu.VMEM((npg * spp, CMP_STRIDE * 256), BF16)], "compress_paged")


def _nsa_sample_cmp_kernel(qn_ref, f_ref, s_ref, cb_ref, oc_ref, ps_ref, *, nseg, pos0, n_new):
    scale = HEAD_DIM ** -0.5
    tq = qn_ref.shape[1]
    qpos = pos0 + jnp.minimum(lax.broadcasted_iota(jnp.int32, (tq, 1), 0), n_new - 1)
    kv = f_ref[0] + pltpu.roll(s_ref[0], nseg - 1, 0) + cb_ref[0:1, :]
    kcmp = kv[:, 0:LANES].astype(BF16)
    vcmp = kv[:, LANES:2 * LANES].astype(BF16)
    for g in range(NSA_KV_HEADS):
        qn = _stack_rows(qn_ref.at[:, :, g * NSA_GROUP * LANES:(g + 1) * NSA_GROUP * LANES], NSA_GROUP, scale)
        o_c, psum = _cmp_branch(qn, kcmp, vcmp, qpos, nseg, nseg - 1)
        for i in range(NSA_GROUP):
            hblk = g * NSA_GROUP + i
            oc_ref[0, :, hblk * LANES:(hblk + 1) * LANES] = o_c[i]
        ps_ref[0, g] = psum


def _nsa_sample_cmp(qn, f, s, cb, pos0, n_new):
    b, tq, _ = qn.shape
    nseg = f.shape[1]
    fspec = pl.BlockSpec((1, nseg, 256), lambda i: (i, 0, 0))
    return pl.pallas_call(
        functools.partial(_nsa_sample_cmp_kernel, nseg=nseg, pos0=pos0, n_new=n_new),
        grid=(b,),
        in_specs=[pl.BlockSpec((1, tq, NSA_HEADS * LANES), lambda i: (i, 0, 0)), fspec, fspec,
                  pl.BlockSpec((8, 256), lambda i: (0, 0))],
        out_specs=[pl.BlockSpec((1, tq, NSA_HEADS * LANES), lambda i: (i, 0, 0)),
                   pl.BlockSpec((1, NSA_KV_HEADS, tq, nseg), lambda i: (i, 0, 0, 0))],
        out_shape=[jax.ShapeDtypeStruct((b, tq, NSA_HEADS * LANES), F32),
                   jax.ShapeDtypeStruct((b, NSA_KV_HEADS, tq, nseg), F32)],
        compiler_params=_params(("parallel",)),
        name="nsa_sample_cmp",
    )(qn, f, s, cb)


def _select_rows_kernel(ps_ref, sel_ref, sc_ref, *, nb, pos0, tq, n_new):
    rows = ps_ref.shape[0]
    qpos = pos0 + jnp.minimum(lax.broadcasted_iota(jnp.int32, (rows, 1), 0) % tq, n_new - 1)
    sel_ref[...] = _select_blocks(ps_ref[...], qpos, nb, sc_ref)


def _select_rows(psum, nb, pos0, tq, n_new):
    rows, ncmp = psum.shape
    nbp = -(-nb // LANES) * LANES
    return pl.pallas_call(
        functools.partial(_select_rows_kernel, nb=nb, pos0=pos0, tq=tq, n_new=n_new),
        grid=(1,),
        in_specs=[pl.BlockSpec((rows, ncmp), lambda i: (0, 0))],
        out_specs=pl.BlockSpec((rows, nbp), lambda i: (0, 0)),
        out_shape=jax.ShapeDtypeStruct((rows, nbp), F32),
        scratch_shapes=[pltpu.VMEM((nbp, rows), F32)],
        compiler_params=_params(("arbitrary",)),
        name="select_rows",
    )(psum)


def _state_update(st_refs, g, carry_fn):
    m_ref, l_ref, acc_ref = st_refs
    m, l, acc = carry_fn((m_ref[g], l_ref[g], acc_ref[g]))
    m_ref[g] = m
    l_ref[g] = l
    acc_ref[g] = acc


def _state_init(st_refs):
    m_ref, l_ref, acc_ref = st_refs
    m_ref[...] = jnp.full(m_ref.shape, NEG_INF, F32)
    l_ref[...] = jnp.zeros(l_ref.shape, F32)
    acc_ref[...] = jnp.zeros(acc_ref.shape, F32)


def _state_scratch(groups, r, tq):
    return [pltpu.VMEM((groups, r, tq, 1), F32), pltpu.VMEM((groups, r, tq, 1), F32),
            pltpu.VMEM((groups, r, tq, LANES), F32)]


def _nsa_sample_kernel(pt_ref, *refs, npg, pos0, n_new):
    pages = refs[:npg]
    (qr_ref, sel_ref, new_ref, wst_ref, wnew_ref, oc_ref, gate_ref, o_ref,
     m_ref, l_ref, acc_ref, k16, v16) = refs[npg:]
    st = (m_ref, l_ref, acc_ref)
    j = pl.program_id(1)
    tq = qr_ref.shape[1]
    psz = pages[0].shape[-1]
    tk = npg * psz
    scale = HEAD_DIM ** -0.5
    qpos = pos0 + jnp.minimum(lax.broadcasted_iota(jnp.int32, (tq, 1), 0), n_new - 1)

    @pl.when(j == 0)
    def _():
        _state_init(st)

    for k in range(npg):
        k16[:, k * psz:(k + 1) * psz] = pages[k][0, 0, 0:LANES, :].astype(BF16)
        v16[:, k * psz:(k + 1) * psz] = pages[k][0, 0, LANES:2 * LANES, :].astype(BF16)
    k0 = j * tk
    kpos = k0 + lax.broadcasted_iota(jnp.int32, (1, tk), 1)
    qrs = []
    for g in range(NSA_KV_HEADS):
        qr = _stack_rows(qr_ref.at[:, :, g * NSA_GROUP * LANES:(g + 1) * NSA_GROUP * LANES], NSA_GROUP, scale)
        qrs.append(qr)
        bias = jnp.where(_expand_blocks(sel_ref[0, g], k0, tk), _mask_bias(kpos <= qpos), NEG_INF)
        s = _dot(qr, k16[...]).reshape(NSA_GROUP, tq, tk) + bias[None]
        _state_update(st, g, functools.partial(_online_update_fast, s, v16[...], vt=True))

    @pl.when(j == pl.num_programs(1) - 1)
    def _():
        nrow = new_ref.shape[1]
        r_i = lax.broadcasted_iota(jnp.int32, (1, nrow), 1)
        npos = pos0 + r_i
        m_new = (r_i < n_new) & (npos <= qpos)
        wlen = wst_ref.shape[-1]
        wpos = pos0 - wlen + lax.broadcasted_iota(jnp.int32, (1, wlen), 1)
        m_old = (wpos > qpos - WINDOW) & (wpos >= 0)
        m_wnew = m_new & (npos > qpos - WINDOW)
        kn = new_ref[0, :, 0:LANES].astype(BF16)
        vn = new_ref[0, :, LANES:2 * LANES].astype(BF16)
        kwo = wst_ref[0, 0, 0:LANES, :].astype(BF16)
        vwo = wst_ref[0, 0, LANES:2 * LANES, :].astype(BF16)
        kwn = wnew_ref[0, :, 0:LANES].astype(BF16)
        vwn = wnew_ref[0, :, LANES:2 * LANES].astype(BF16)
        for g in range(NSA_KV_HEADS):
            qr = qrs[g]
            blk_ok = _expand_blocks(sel_ref[0, g], pos0, nrow)
            s = _dot_nt(qr, kn).reshape(NSA_GROUP, tq, nrow)
            _state_update(st, g, functools.partial(_online_update, s, (m_new & blk_ok)[None], vn))
            o_s = _online_final((m_ref[g], l_ref[g], acc_ref[g]))
            cw = _online_init(NSA_GROUP, tq)
            cw = _online_update(_dot(qr, kwo).reshape(NSA_GROUP, tq, wlen), m_old[None], vwo, cw, vt=True)
            cw = _online_update(_dot_nt(qr, kwn).reshape(NSA_GROUP, tq, nrow), m_wnew[None], vwn, cw)
            o_w = _online_final(cw)
            gates = gate_ref[0, :, g * LANES:(g + 1) * LANES]
            for i in range(NSA_GROUP):
                hblk = g * NSA_GROUP + i
                o_c = oc_ref[0, :, hblk * LANES:(hblk + 1) * LANES]
                o = (gates[:, 3 * i:3 * i + 1] * o_c + gates[:, 3 * i + 1:3 * i + 2] * o_s[i]
                     + gates[:, 3 * i + 2:3 * i + 3] * o_w[i])
                o_ref[0, :, hblk * LANES:(hblk + 1) * LANES] = o.astype(o_ref.dtype)


def _nsa_sample(pt, layer, pool, qr, sel, new, wst, wnew, oc, gate, pos0, n_new):
    b, n_pages = pt.shape
    psz = pool.shape[-1]
    npg = min(PAGES, n_pages)
    tq = qr.shape[1]
    nbp = sel.shape[-1]
    wlen = wst.shape[-1]
    full = lambda shp: pl.BlockSpec((1,) + shp, lambda i, j, pt: (i,) + (0,) * len(shp))
    return _paged_call(
        functools.partial(_nsa_sample_kernel, pos0=pos0, n_new=n_new), pt, layer, [pool], (256, psz),
        [qr, sel, new, wst, wnew, oc, gate],
        [full((tq, NSA_HEADS * LANES)), full((NSA_KV_HEADS, tq, nbp)), full((tq, 256)),
         pl.BlockSpec((1, 1, 256, wlen), lambda i, j, pt: (layer, i, 0, 0)),
         full((tq, 256)), full((tq, NSA_HEADS * LANES)), full((tq, 3 * LANES))],
        full((tq, NSA_HEADS * LANES)),
        jax.ShapeDtypeStruct((b, tq, NSA_HEADS * LANES), BF16),
        _state_scratch(NSA_KV_HEADS, NSA_GROUP, tq)
        + [pltpu.VMEM((LANES, npg * psz), BF16)] * 2, "nsa_sample")


def _diff_sample_kernel(pt_ref, *refs, npg, pos0, n_new, lam_init):
    pages = refs[:npg]
    q_ref, new_ref, dl_ref, gain_ref, o_ref, m_ref, l_ref, acc_ref, k16, v16 = refs[npg:]
    st = (m_ref, l_ref, acc_ref)
    j = pl.program_id(1)
    tq = q_ref.shape[1]
    psz = pages[0].shape[-1]
    tk = npg * psz
    qpos = pos0 + jnp.minimum(lax.broadcasted_iota(jnp.int32, (tq, 1), 0), n_new - 1)

    @pl.when(j == 0)
    def _():
        _state_init(st)

    kpos = j * tk + lax.broadcasted_iota(jnp.int32, (1, tk), 1)
    bias = _mask_bias(kpos <= qpos)[None]
    qs = []
    for kb in range(2):
        for k in range(npg):
            k16[:, k * psz:(k + 1) * psz] = pages[k][0, 0, kb * LANES:(kb + 1) * LANES, :].astype(BF16)
            v16[:, k * psz:(k + 1) * psz] = pages[k][0, 0, (2 + kb) * LANES:(3 + kb) * LANES, :].astype(BF16)
        q = _stack_rows(q_ref.at[:, :, kb * 4 * LANES:(kb + 1) * 4 * LANES], 4, DIFF_HALF ** -0.5)
        qs.append(q)
        s = _dot(q, k16[...]).reshape(4, tq, tk) + bias
        _state_update(st, kb, functools.partial(_online_update_fast, s, v16[...], vt=True))

    @pl.when(j == pl.num_programs(1) - 1)
    def _():
        nrow = new_ref.shape[1]
        r_i = lax.broadcasted_iota(jnp.int32, (1, nrow), 1)
        m_new = ((r_i < n_new) & (pos0 + r_i <= qpos))[None]
        lam = _diff_lambda(dl_ref, lam_init)
        for kb in range(2):
            kn = new_ref[0, :, kb * LANES:(kb + 1) * LANES].astype(BF16)
            vn = new_ref[0, :, (2 + kb) * LANES:(3 + kb) * LANES].astype(BF16)
            s = _dot_nt(qs[kb], kn).reshape(4, tq, nrow)
            _state_update(st, kb, functools.partial(_online_update, s, m_new, vn))
            o = _online_final((m_ref[kb], l_ref[kb], acc_ref[kb]))
            _diff_finish(o, lam, gain_ref, lam_init, o_ref.at[:, :, kb * 2 * LANES:(kb + 1) * 2 * LANES])


def _diff_sample(pt, layer, pool, dq, new, dl, gain, lam_init, pos0, n_new):
    b, n_pages = pt.shape
    psz = pool.shape[-1]
    npg = min(PAGES, n_pages)
    tq = dq.shape[1]
    full = lambda shp: pl.BlockSpec((1,) + shp, lambda i, j, pt: (i,) + (0,) * len(shp))
    const = lambda shp: pl.BlockSpec(shp, lambda i, j, pt: (0,) * len(shp))
    return _paged_call(
        functools.partial(_diff_sample_kernel, pos0=pos0, n_new=n_new, lam_init=lam_init), pt, layer, [pool],
        (512, psz), [dq, new, dl, _gain_lanes(gain)],
        [full((tq, 8 * LANES)), full((tq, 512)), const((4, DIFF_HALF)), const((1, LANES))],
        full((tq, 4 * LANES)),
        jax.ShapeDtypeStruct((b, tq, 4 * LANES), BF16),
        _state_scratch(2, 4, tq) + [pltpu.VMEM((LANES, npg * psz), BF16)] * 2, "diff_sample")


def _fox_sample_kernel(pt_ref, *refs, npg, pos0, n_new):
    pages = refs[:npg]
    q_ref, new_ref, cr_ref, tot_ref, lf_ref, o_ref, m_ref, l_ref, acc_ref, k16, v16 = refs[npg:]
    st = (m_ref, l_ref, acc_ref)
    j = pl.program_id(1)
    tq = q_ref.shape[1]
    psz = pages[0].shape[-1]
    tk = npg * psz
    qpos = pos0 + jnp.minimum(lax.broadcasted_iota(jnp.int32, (tq, 1), 0), n_new - 1)

    @pl.when(j == 0)
    def _():
        _state_init(st)

    lf = lf_ref[0]
    row = lax.broadcasted_iota(jnp.int32, (tq, 1), 0)
    cnew = jnp.zeros_like(lf)
    for t in range(n_new):
        cnew = cnew + jnp.where(row >= t, lf[t:t + 1, :], 0.0)
    psz_l = tot_ref.shape[-1]
    kpos = j * tk + lax.broadcasted_iota(jnp.int32, (1, tk), 1)
    bias = _mask_bias(kpos <= qpos)[None]
    qs = []
    for kb in range(2):
        for k in range(npg):
            k16[:, k * psz:(k + 1) * psz] = pages[k][0, 0, kb * LANES:(kb + 1) * LANES, :].astype(BF16)
            v16[:, k * psz:(k + 1) * psz] = pages[k][0, 0, (2 + kb) * LANES:(3 + kb) * LANES, :].astype(BF16)
        q = _stack_rows(q_ref.at[:, :, kb * 2 * LANES:(kb + 1) * 2 * LANES], 2, HEAD_DIM ** -0.5)
        qs.append(q)
        cq = jnp.stack([tot_ref[0, 2 * kb + hh:2 * kb + hh + 1, psz_l - 1:psz_l]
                        + cnew[:, 2 * kb + hh:2 * kb + hh + 1] for hh in range(2)])
        ck = cr_ref[0, 2 * kb:2 * kb + 2, :][:, None, :]
        s = _dot(q, k16[...]).reshape(2, tq, tk) + cq - ck + bias
        _state_update(st, kb, functools.partial(_online_update_fast, s, v16[...], vt=True))

    @pl.when(j == pl.num_programs(1) - 1)
    def _():
        nrow = new_ref.shape[1]
        r_i = lax.broadcasted_iota(jnp.int32, (1, nrow), 1)
        m_new = ((r_i < n_new) & (pos0 + r_i <= qpos))[None]
        for kb in range(2):
            kn = new_ref[0, :, kb * LANES:(kb + 1) * LANES].astype(BF16)
            vn = new_ref[0, :, (2 + kb) * LANES:(3 + kb) * LANES].astype(BF16)
            bias = []
            for hh in range(2):
                h = 2 * kb + hh
                d = jnp.zeros((tq, nrow), F32)
                for t in range(n_new):
                    d = d + jnp.where((row >= t) & (r_i < t), lf[t:t + 1, h:h + 1], 0.0)
                bias.append(d)
            s = _dot_nt(qs[kb], kn).reshape(2, tq, nrow) + jnp.stack(bias)
            _state_update(st, kb, functools.partial(_online_update, s, m_new, vn))
            o = _online_final((m_ref[kb], l_ref[kb], acc_ref[kb]))
            for hh in range(2):
                hblk = 2 * kb + hh
                o_ref[0, :, hblk * LANES:(hblk + 1) * LANES] = o[hh].astype(o_ref.dtype)


def _fox_sample(pt, layer, pool, fq, new, crow, lf, pos0, n_new):
    b, n_pages = pt.shape
    psz = pool.shape[-1]
    npg = min(PAGES, n_pages)
    tq = fq.shape[1]
    past = crow.shape[-1]
    full = lambda shp: pl.BlockSpec((1,) + shp, lambda i, j, pt: (i,) + (0,) * len(shp))
    return _paged_call(
        functools.partial(_fox_sample_kernel, pos0=pos0, n_new=n_new), pt, layer, [pool], (512, psz),
        [fq, new, crow, crow, lf],
        [full((tq, 4 * LANES)), full((tq, 512)),
         pl.BlockSpec((1, 8, npg * psz), lambda i, j, pt: (i, 0, j)),
         pl.BlockSpec((1, 8, LANES), lambda i, j, pt: (i, 0, past // LANES - 1)),
         pl.BlockSpec((1, tq, LANES), lambda i, j, pt: (i, 0, 2))],
        full((tq, 4 * LANES)),
        jax.ShapeDtypeStruct((b, tq, 4 * LANES), BF16),
        _state_scratch(2, 2, tq) + [pltpu.VMEM((LANES, npg * psz), BF16)] * 2, "fox_sample")


def _wo_rows():
    import numpy as np
    rows = []
    for h in range(NSA_HEADS):
        r = np.full(LANES, -1, np.int64)
        g = h // NSA_GROUP
        r[64 * g:64 * g + 64] = 64 * h + np.arange(64)
        rows.append(r)
    for base, nh in ((NSA_HEADS * HEAD_DIM, DIFF_HEADS), ((NSA_HEADS + DIFF_HEADS) * HEAD_DIM, FOX_HEADS)):
        for h in range(nh):
            r = np.full(LANES, -1, np.int64)
            r[64 * (h % 2):64 * (h % 2) + 64] = base + 64 * h + np.arange(64)
            rows.append(r)
    return np.concatenate(rows)


def _extend_wo(w):
    import numpy as np
    idx = _wo_rows()
    return jnp.where(jnp.asarray(idx >= 0)[:, None], w[jnp.asarray(np.maximum(idx, 0))], 0.0).astype(BF16)


def _mem_and_peer(x2, bsz, kv, lw):
    n, d = x2.shape
    q = _linear(x2, lw["w_mq"], gain=lw["g_mem"])
    t = n // bsz
    if t < TP:
        qp = jnp.pad(q.reshape(bsz, t, d), ((0, 0), (0, TP - t), (0, 0)))
        om = _mem_attn(qp, kv)[:, :t]
    else:
        om = _mem_attn(q.reshape(bsz, t, d), kv)
    x2 = _linear(om.reshape(n, d), lw["w_mo"], res=x2)
    return _peer(x2, lw["g_ffn"], lw["peer_wq"], lw["peer_keys"], lw["peer_u"], lw["peer_vt"])


def _prompt_layer(x2, bsz, mem_prompt, lw, tabs):
    n, d = x2.shape
    t = n // bsz
    outs = _inproj(x2, lw["g_attn"], lw["w_ext"], lw["b_ext"], tabs[0], tabs[1], 256)
    o = {name: v.reshape(bsz, t, -1) for (name, _, _), v in zip(_SEGS, outs)}
    f, s, cb = _compress(o["cmp"].reshape(bsz, t // CMP_STRIDE, CMP_STRIDE * 256), *lw["cmp_w"])
    o_nsa = _nsa_prompt(o["qn"], o["qr"], f, s, cb, o["sel"], o["win"], o["gate"])
    o_diff = _diff_prompt(o["dq"], o["diff"], lw["diff_lambda"], lw["diff_gain"], lw["lam_init"])
    ccol, crow = _cumsum(o["gate"])
    o_fox = _fox_prompt(o["fq"], o["fox"], ccol, crow)
    mixed = jnp.concatenate([o_nsa, o_diff, o_fox], axis=-1).reshape(n, -1)
    x2 = _linear(mixed, lw["w_o_ext"], res=x2)
    m = mem_prompt.shape[1]
    mkv = _linear(mem_prompt.reshape(bsz * m, d), lw["w_mkv"])
    x2 = _mem_and_peer(x2, bsz, mkv.reshape(bsz, m, 2 * d), lw)
    return x2, o, mkv, cb


def _sample_layer(x2, bsz, pt, layer, pools, wstate, mem_kv, lw, tabs, cb, pos0):
    n, d = x2.shape
    t = n // bsz
    outs = _inproj(x2, lw["g_attn"], lw["w_ext"], lw["b_ext"], tabs[0], tabs[1], n)
    o = {name: v.reshape(bsz, t, -1) for (name, _, _), v in zip(_SEGS, outs)}
    op = {name: jnp.pad(v, ((0, 0), (0, TP - t), (0, 0))) for name, v in o.items()}
    cmp_pool, sel_pool, diff_pool, fox_pool, logf_pool_t = pools
    f, s = _compress_paged(pt, layer, cmp_pool, lw["cmp_w"][0], lw["cmp_w"][1])
    nseg = f.shape[1]
    oc, ps = _nsa_sample_cmp(op["qn"], f, s, cb, pos0, t)
    nb = -(-(pos0 + t) // SEL_BLOCK)
    sel = _select_rows(ps.reshape(bsz * NSA_KV_HEADS * TP, nseg), nb, pos0, TP, t)
    sel = sel.reshape(bsz, NSA_KV_HEADS, TP, -1)
    o_nsa = _nsa_sample(pt, layer, sel_pool, op["qr"], sel, op["sel"], wstate, op["win"], oc, op["gate"], pos0, t)
    o_diff = _diff_sample(pt, layer, diff_pool, op["dq"], op["diff"], lw["diff_lambda"], lw["diff_gain"],
                          lw["lam_init"], pos0, t)
    crow = _cumsum_paged(pt, layer, logf_pool_t)
    o_fox = _fox_sample(pt, layer, fox_pool, op["fq"], op["fox"], crow, op["gate"], pos0, t)
    mixed = jnp.concatenate([o_nsa, o_diff, o_fox], axis=-1)[:, :t].reshape(n, -1)
    x2 = _linear(mixed, lw["w_o_ext"], res=x2)
    x2 = _mem_and_peer(x2, bsz, mem_kv, lw)
    return x2, o


def kernel(x_prompt, x_sample, cache_nsa_cmp_kv, cache_nsa_sel_kv, cache_diff_kv, cache_fox_kv, cache_fox_logf, state_nsa_win_kv, cache_mem_kv, page_table, mem_prompt, g_attn, w_in, b_in, nsa_pe_k, nsa_pe_v, nsa_w_ck, nsa_w_cv, diff_lambda, diff_gain, w_o, g_mem, w_mq, w_mkv, w_mo, g_ffn, peer_wq, peer_keys, peer_u, peer_v, g_final):
    bp, tp, d = x_prompt.shape
    bs, ts, _ = x_sample.shape
    depth = w_in.shape[0]
    n_pool, psz = cache_nsa_cmp_kv.shape[1:3]
    pos0 = page_table.shape[1] * psz
    assert pos0 % CMP_STRIDE == 0 and ts < CMP_STRIDE and ts <= TP
    pos_p = jnp.arange(tp, dtype=jnp.int32)
    pos_s = pos0 + (jnp.arange(bs * ts, dtype=jnp.int32) % ts)
    tabs_p = (_rope_tables(pos_p, HEAD_DIM, ROT_DIM), _rope_tables(pos_p, DIFF_HALF, DIFF_ROT))
    tabs_s = (_rope_tables(pos_s, HEAD_DIM, ROT_DIM), _rope_tables(pos_s, DIFF_HALF, DIFF_ROT))
    xp = x_prompt.reshape(bp * tp, d)
    xs = x_sample.reshape(bs * ts, d)
    names = ("cmp", "sel", "diff", "fox")
    rows_p = {k: [] for k in names + ("logf", "win", "mem")}
    rows_s = {k: [] for k in names + ("logf", "win")}
    pools = tuple(_feature_major(c) for c in
                  (cache_nsa_cmp_kv, cache_nsa_sel_kv, cache_diff_kv, cache_fox_kv, cache_fox_logf))
    wstate_t = _feature_major(state_nsa_win_kv)
    for l in range(depth):
        w_ext, b_ext = _extend_inproj(w_in[l], b_in[l])
        lw = dict(
            g_attn=g_attn[l], w_ext=w_ext, b_ext=b_ext,
            cmp_w=_compress_weights(nsa_w_ck[l], nsa_w_cv[l], nsa_pe_k[l], nsa_pe_v[l]),
            diff_lambda=diff_lambda[l], diff_gain=diff_gain[l], lam_init=0.8 - 0.6 * math.exp(-0.3 * l),
            w_o_ext=_extend_wo(w_o[l]), g_mem=g_mem[l], w_mq=w_mq[l], w_mkv=w_mkv[l], w_mo=w_mo[l],
            g_ffn=g_ffn[l], peer_wq=peer_wq[l], peer_keys=peer_keys[l],
            peer_u=peer_u[l].astype(BF16), peer_vt=peer_v[l].T.astype(BF16))
        xp, o, mkv, cb = _prompt_layer(xp, bp, mem_prompt, lw, tabs_p)
        wstate = state_nsa_win_kv[l].reshape(bs, -1, 256)
        xs, os_ = _sample_layer(xs, bs, page_table, l, pools, wstate_t, cache_mem_kv[l].reshape(bs, -1, 2 * d), lw,
                                tabs_s, cb, pos0)
        for k in names:
            rows_p[k].append(o[k])
            rows_s[k].append(os_[k])
        rows_p["logf"].append(o["gate"][..., 2 * LANES:2 * LANES + FOX_HEADS])
        rows_s["logf"].append(os_["gate"][..., 2 * LANES:2 * LANES + FOX_HEADS])
        rows_p["win"].append(o["win"][:, -min(WINDOW, tp):])
        win_all = jnp.concatenate([wstate, os_["win"]], axis=1)
        rows_s["win"].append(win_all[:, -min(WINDOW, win_all.shape[1]):])
        rows_p["mem"].append(mkv)
    y_p = _final_norm(xp, g_final).reshape(bp, tp, d)
    y_s = _final_norm(xs, g_final).reshape(bs, ts, d)

    def st(lst, tail):
        a = jnp.stack(lst, axis=0)
        return a.reshape(a.shape[:3] + tail)

    kv2 = (2, NSA_KV_HEADS, HEAD_DIM)
    kv4 = (2, DIFF_HEADS, HEAD_DIM)
    return (y_p, y_s,
            st(rows_p["cmp"], kv2), st(rows_s["cmp"], kv2), st(rows_p["sel"], kv2), st(rows_s["sel"], kv2),
            st(rows_p["diff"], kv4), st(rows_s["diff"], kv4), st(rows_p["fox"], kv4), st(rows_s["fox"], kv4),
            st(rows_p["logf"], (FOX_HEADS,)), st(rows_s["logf"], (FOX_HEADS,)),
            st(rows_p["win"], kv2), st(rows_s["win"], kv2),
            jnp.stack(rows_p["mem"], 0).reshape(depth, bp, -1, 2, MEM_HEADS, d // MEM_HEADS))
```

```python
import functools
import math

import jax
import jax.numpy as jnp
from jax import lax
from jax.experimental import pallas as pl
from jax.experimental.pallas import tpu as pltpu

F32 = jnp.float32
BF16 = jnp.bfloat16

HEAD_DIM = 64
ROT_DIM = HEAD_DIM // 4
ROPE_THETA = 500000.0
NSA_HEADS = 8
NSA_KV_HEADS = 2
NSA_GROUP = NSA_HEADS // NSA_KV_HEADS
CMP_STRIDE = 16
CMP_BLOCK = 32
SEL_BLOCK = 64
N_SEL = 16
WINDOW = 512
DIFF_HEADS = 4
DIFF_HALF = HEAD_DIM // 2
DIFF_ROT = DIFF_HALF // 4
FOX_HEADS = 4
MEM_HEADS = 4
PEER_HEADS = 8
PEER_KEYS = 128
PEER_TOPK = 16
PEER_HALF = 128
EPS = 1e-6
NEG_INF = -1e30
FORCE_SCORE = 1e4
LANES = 128
VMEM_LIMIT = 56 * 1024 * 1024


def _params(sem, vmem=VMEM_LIMIT):
    return pltpu.CompilerParams(dimension_semantics=sem, vmem_limit_bytes=vmem)


def _dot(a, b):
    return jnp.dot(a, b, preferred_element_type=F32)


def _dot_nt(a, b):
    return lax.dot_general(a, b, (((1,), (1,)), ((), ())), preferred_element_type=F32)


def _rms(x, g):
    return x * lax.rsqrt(jnp.mean(x * x, axis=-1, keepdims=True) + EPS) * g


def _linear_kernel(*refs, has_gain, has_bias, has_res):
    it = iter(refs)
    x_ref = next(it)
    g_ref = next(it) if has_gain else None
    w_ref = next(it)
    b_ref = next(it) if has_bias else None
    r_ref = next(it) if has_res else None
    o_ref = next(it)
    xb_ref = next(it)

    @pl.when(pl.program_id(1) == 0)
    def _():
        x = x_ref[...].astype(F32)
        if has_gain:
            x = _rms(x, g_ref[...])
        xb_ref[...] = x.astype(BF16)

    y = _dot(xb_ref[...], w_ref[...])
    if has_bias:
        y = y + b_ref[...]
    if has_res:
        y = y + r_ref[...]
    o_ref[...] = y.astype(o_ref.dtype)


def _linear(x, w, gain=None, bias=None, res=None, tm=512, tn=512, out_dtype=F32):
    m, k = x.shape
    n = w.shape[1]
    tm = min(tm, m)
    tn = min(tn, n)
    assert m % tm == 0 and n % tn == 0
    args = [x]
    specs = [pl.BlockSpec((tm, k), lambda i, j: (i, 0))]
    if gain is not None:
        args.append(gain.reshape(1, k))
        specs.append(pl.BlockSpec((1, k), lambda i, j: (0, 0)))
    args.append(w.astype(BF16))
    specs.append(pl.BlockSpec((k, tn), lambda i, j: (0, j)))
    if bias is not None:
        args.append(bias.reshape(1, n))
        specs.append(pl.BlockSpec((1, tn), lambda i, j: (0, j)))
    if res is not None:
        args.append(res)
        specs.append(pl.BlockSpec((tm, tn), lambda i, j: (i, j)))
    return pl.pallas_call(
        functools.partial(_linear_kernel, has_gain=gain is not None, has_bias=bias is not None,
                          has_res=res is not None),
        grid=(m // tm, n // tn),
        in_specs=specs,
        out_specs=pl.BlockSpec((tm, tn), lambda i, j: (i, j)),
        out_shape=jax.ShapeDtypeStruct((m, n), out_dtype),
        scratch_shapes=[pltpu.VMEM((tm, k), BF16)],
        compiler_params=_params(("parallel", "arbitrary")),
        name="linear",
    )(*args)


def _peer_scores_kernel(x_ref, g_ref, wq_ref, keys_ref, xn_ref, st_ref):
    hb = _rms(x_ref[...], g_ref[...]).astype(BF16)
    xn_ref[...] = hb
    qb = _dot(hb, wq_ref[...]).astype(BF16)
    for hc in range(2 * PEER_HEADS):
        st_ref[hc] = _dot_nt(keys_ref[hc], qb[:, hc * PEER_HALF:(hc + 1) * PEER_HALF])


def _peer_scores(x, g, wq, keys):
    n, d = x.shape
    tm = min(256, n)
    nq = wq.shape[1]
    hc = 2 * PEER_HEADS
    return pl.pallas_call(
        _peer_scores_kernel,
        grid=(n // tm,),
        in_specs=[pl.BlockSpec((tm, d), lambda i: (i, 0)),
                  pl.BlockSpec((1, d), lambda i: (0, 0)),
                  pl.BlockSpec((d, nq), lambda i: (0, 0)),
                  pl.BlockSpec((hc, PEER_KEYS, PEER_HALF), lambda i: (0, 0, 0))],
        out_specs=[pl.BlockSpec((tm, d), lambda i: (i, 0)),
                   pl.BlockSpec((hc, PEER_KEYS, tm), lambda i: (0, 0, i))],
        out_shape=[jax.ShapeDtypeStruct((n, d), BF16),
                   jax.ShapeDtypeStruct((hc, PEER_KEYS, n), F32)],
        compiler_params=_params(("parallel",)),
        name="peer_scores",
    )(x, g.reshape(1, d), wq.astype(BF16), keys.reshape(hc, PEER_KEYS, PEER_HALF).astype(BF16))


_PEER_PAIRS = [(a, b) for a in range(PEER_TOPK) for b in range(PEER_TOPK) if (a + 1) * (b + 1) <= PEER_TOPK]


def _peer_topk_kernel(st_ref, t_ref, r_ref, a_ref, c_ref, rk_ref, sv_ref, av_ref, cn_ref):
    tn = st_ref.shape[-1]
    row = lax.broadcasted_iota(jnp.int32, (PEER_KEYS, tn), 0).astype(F32)
    big = float(PEER_KEYS)

    rk_ref[...] = jnp.full(rk_ref.shape, big, F32)
    nch = 2
    for h0 in range(0, 2 * PEER_HEADS, nch):

        def extract(t, carry, h0=h0):
            out = []
            for k, s in enumerate(carry):
                h, c = divmod(h0 + k, 2)
                m = jnp.max(s, axis=0, keepdims=True)
                idx = jnp.min(jnp.where(s == m, row, big), axis=0, keepdims=True)
                hit = row == idx
                sv_ref[c, t, h:h + 1, :] = m
                rk_ref[h0 + k] = jnp.where(hit, jnp.asarray(t).astype(F32), rk_ref[h0 + k])
                out.append(jnp.where(hit, -jnp.inf, s))
            return tuple(out)

        lax.fori_loop(0, PEER_TOPK, extract, tuple(st_ref[h0 + k] for k in range(nch)))

    cand = [sv_ref[0, a] + sv_ref[1, b] for a, b in _PEER_PAIRS]
    top = sv_ref[0, 0] + sv_ref[1, 0]
    sel = []
    for ia, (a, b) in enumerate(_PEER_PAIRS):
        cnt = jnp.zeros_like(top)
        for ib, (a2, b2) in enumerate(_PEER_PAIRS):
            if ib == ia:
                continue
            ahead = (cand[ib] >= cand[ia]) if (a2 * PEER_TOPK + b2) < (a * PEER_TOPK + b) else (cand[ib] > cand[ia])
            cnt = cnt + jnp.where(ahead, 1.0, 0.0)
        sel.append(jnp.where(cnt < float(PEER_TOPK), 1.0, 0.0))
    z = jnp.zeros_like(top)
    counts = [jnp.zeros_like(top) for _ in range(PEER_TOPK)]
    for ia, (a, b) in enumerate(_PEER_PAIRS):
        z = z + sel[ia] * jnp.exp(cand[ia] - top)
        counts[a] = counts[a] + sel[ia]
    for a in range(PEER_TOPK):
        av_ref[a] = jnp.exp(sv_ref[0, a] - sv_ref[0, 0]) / z
        cn_ref[a] = counts[a]

    for h in range(PEER_HEADS):
        rank1 = rk_ref[2 * h]
        rank2 = rk_ref[2 * h + 1]

        def scatter(a, carry, h=h, rank1=rank1):
            wa, ca = carry
            hit = rank1 == jnp.asarray(a).astype(F32)
            wa = jnp.where(hit, av_ref[a, h:h + 1, :], wa)
            ca = jnp.where(hit, cn_ref[a, h:h + 1, :], ca)
            return wa, ca

        zero = jnp.zeros((PEER_KEYS, tn), F32)
        wa, ca = lax.fori_loop(0, PEER_TOPK, scatter, (zero, zero))
        a_ref[h] = wa.astype(a_ref.dtype)
        c_ref[h] = ca.astype(c_ref.dtype)
        r_ref[h] = rank2.astype(r_ref.dtype)
        t_ref[h] = jnp.where(rank2 < float(PEER_TOPK), jnp.exp(st_ref[2 * h + 1] - sv_ref[1, 0, h:h + 1, :]),
                             0.0).astype(t_ref.dtype)


def _peer_topk(st):
    hc, nk, n = st.shape
    tn = LANES
    spec_h = pl.BlockSpec((PEER_HEADS, nk, tn), lambda i: (0, 0, i))
    shape_h = jax.ShapeDtypeStruct((PEER_HEADS, nk, n), F32)
    shape_b = jax.ShapeDtypeStruct((PEER_HEADS, nk, n), BF16)
    return pl.pallas_call(
        _peer_topk_kernel,
        grid=(n // tn,),
        in_specs=[pl.BlockSpec((hc, nk, tn), lambda i: (0, 0, i))],
        out_specs=[spec_h] * 4,
        out_shape=[shape_b, shape_b, shape_h, shape_h],
        scratch_shapes=[pltpu.VMEM((hc, nk, tn), F32),
                        pltpu.VMEM((2, PEER_TOPK, PEER_HEADS, tn), F32),
                        pltpu.VMEM((PEER_TOPK, PEER_HEADS, tn), F32),
                        pltpu.VMEM((PEER_TOPK, PEER_HEADS, tn), F32)],
        compiler_params=_params(("parallel",)),
        name="peer_topk",
    )(st)


def _gelu(a):
    return 0.5 * a * (1.0 + lax.erf(a * math.sqrt(0.5)))


def _peer_dense_kernel(xn_ref, u_ref, vt_ref, t_ref, r_ref, a_ref, c_ref, res_ref, o_ref, acc_ref, h_ref):
    e = pl.program_id(1)
    te = u_ref.shape[0]
    nib = te // PEER_KEYS
    tm = xn_ref.shape[0]
    sub = 16

    @pl.when(e == 0)
    def _():
        acc_ref[...] = jnp.zeros_like(acc_ref)

    at = _dot_nt(u_ref[...], xn_ref[...])
    for ib in range(nib):
        i = e * nib + ib
        g = jnp.zeros((PEER_KEYS // sub, sub, tm), BF16)
        for h in range(PEER_HEADS):
            cnt = jnp.broadcast_to(c_ref[h, pl.ds(i, 1), :], (sub, tm)).astype(BF16)[None]
            wa = jnp.broadcast_to(a_ref[h, pl.ds(i, 1), :], (sub, tm)).astype(BF16)[None]
            r3 = r_ref[h].reshape(PEER_KEYS // sub, sub, tm)
            t3 = t_ref[h].reshape(PEER_KEYS // sub, sub, tm)
            g = g + jnp.where(r3 < cnt, t3 * wa, 0.0)
        rows = slice(ib * PEER_KEYS, (ib + 1) * PEER_KEYS)
        h_ref[rows, :] = g.reshape(PEER_KEYS, tm) * _gelu(at[rows, :]).astype(BF16)
    acc_ref[...] += _dot(vt_ref[...], h_ref[...])

    @pl.when(e == pl.num_programs(1) - 1)
    def _():
        o_ref[...] = res_ref[...] + acc_ref[...].T


def _peer_dense(xn, u, vt, t, r, a, c, res, te=1024):
    n, d = xn.shape
    ne = u.shape[0]
    tm = min(512, n)
    spec_h = pl.BlockSpec((PEER_HEADS, PEER_KEYS, tm), lambda i, e: (0, 0, i))
    return pl.pallas_call(
        _peer_dense_kernel,
        grid=(n // tm, ne // te),
        in_specs=[pl.BlockSpec((tm, d), lambda i, e: (i, 0)),
                  pl.BlockSpec((te, d), lambda i, e: (e, 0)),
                  pl.BlockSpec((d, te), lambda i, e: (0, e)),
                  spec_h, spec_h, spec_h, spec_h,
                  pl.BlockSpec((tm, d), lambda i, e: (i, 0))],
        out_specs=pl.BlockSpec((tm, d), lambda i, e: (i, 0)),
        out_shape=jax.ShapeDtypeStruct((n, d), F32),
        scratch_shapes=[pltpu.VMEM((d, tm), F32), pltpu.VMEM((te, tm), BF16)],
        compiler_params=_params(("parallel", "arbitrary")),
        name="peer_dense",
    )(xn, u, vt, t, r, a, c, res)


def _peer(x, g, wq, keys, u_b, vt_b):
    xn, st = _peer_scores(x, g, wq, keys)
    t, r, a, c = _peer_topk(st)
    return _peer_dense(xn, u_b, vt_b, t, r, a, c, x)


_SEGS = (
    ("qn", 1024, "p" * 8),
    ("qr", 1024, "a" * 8),
    ("cmp", 256, "pp"),
    ("sel", 256, "ap"),
    ("win", 256, "ap"),
    ("dq", 1024, "b" * 8),
    ("diff", 512, "bbpp"),
    ("fq", 512, "pppp"),
    ("fox", 512, "pppp"),
    ("gate", 384, "ssl"),
)
_OFF = dict(nq=0, kc=512, vc=640, ks=768, vs=896, kw=1024, vw=1152, ng=1280, dq=1304, dk=1560, dv=1816,
            fq=2072, fk=2328, fv=2584, ff=2840)


def _inproj_columns():
    import numpy as np
    cols = []

    def blocks(n):
        return [np.full(LANES, -1, np.int64) for _ in range(n)]

    for base in ("nq", "nq"):
        bl = blocks(NSA_HEADS)
        for h in range(NSA_HEADS):
            g = h // NSA_GROUP
            bl[h][64 * g:64 * g + 64] = _OFF[base] + 64 * h + np.arange(64)
        cols += bl
    cols.append(np.arange(_OFF["kc"], _OFF["kc"] + 256))
    cols.append(np.arange(_OFF["ks"], _OFF["ks"] + 256))
    cols.append(np.arange(_OFF["kw"], _OFF["kw"] + 256))
    bl = blocks(2 * DIFF_HEADS)
    for h in range(DIFF_HEADS):
        for c in range(2):
            o = 32 * (2 * (h % 2) + c)
            bl[2 * h + c][o:o + 32] = _OFF["dq"] + 64 * h + 32 * c + np.arange(32)
    cols += bl
    cols.append(np.arange(_OFF["dk"], _OFF["dk"] + 512))
    bl = blocks(FOX_HEADS)
    for h in range(FOX_HEADS):
        o = 64 * (h % 2)
        bl[h][o:o + 64] = _OFF["fq"] + 64 * h + np.arange(64)
    cols += bl
    cols.append(np.arange(_OFF["fk"], _OFF["fk"] + 512))
    bl = blocks(3)
    for g in range(NSA_KV_HEADS):
        bl[g][0:3 * NSA_GROUP] = _OFF["ng"] + 3 * NSA_GROUP * g + np.arange(3 * NSA_GROUP)
    bl[2][0:FOX_HEADS] = _OFF["ff"] + np.arange(FOX_HEADS)
    cols += bl
    return np.concatenate(cols)


def _extend_inproj(w, b):
    import numpy as np
    idx = _inproj_columns()
    keep = jnp.asarray(idx >= 0)
    src = jnp.asarray(np.maximum(idx, 0))
    return (jnp.where(keep[None, :], w[:, src], 0.0).astype(BF16),
            jnp.where(keep, b[src], 0.0).reshape(1, -1))


def _rope_tables(pos, period, rot):
    half = rot // 2
    inv = jnp.power(jnp.float32(ROPE_THETA), -jnp.arange(half, dtype=F32) / half)
    ang = pos.astype(F32)[:, None] * inv[None, :]
    cos, sin = jnp.cos(ang), jnp.sin(ang)
    n = pos.shape[0]
    reps = LANES // period
    pad = jnp.zeros((n, period - rot), F32)
    c = jnp.concatenate([cos, cos, pad + 1.0], axis=1)
    s_up = jnp.concatenate([jnp.zeros((n, half), F32), sin, pad], axis=1)
    s_dn = jnp.concatenate([-sin, jnp.zeros((n, half), F32), pad], axis=1)
    return jnp.stack([jnp.tile(c, (1, reps)), jnp.tile(s_up, (1, reps)), jnp.tile(s_dn, (1, reps))])


def _log_sigmoid(x):
    return -(jnp.maximum(-x, 0.0) + jnp.log1p(jnp.exp(-jnp.abs(x))))


def _inproj_kernel(x_ref, g_ref, w_ref, b_ref, ta_ref, tb_ref, *out_refs):
    hb = _rms(x_ref[...], g_ref[...]).astype(BF16)
    c0 = 0
    for (name, width, kinds), o_ref in zip(_SEGS, out_refs):
        z = _dot(hb, w_ref[:, c0:c0 + width]) + b_ref[:, c0:c0 + width]
        for k, kind in enumerate(kinds):
            zk = z[:, k * LANES:(k + 1) * LANES]
            if kind == "a":
                zk = (zk * ta_ref[0] + pltpu.roll(zk, ROT_DIM // 2, 1) * ta_ref[1]
                      + pltpu.roll(zk, LANES - ROT_DIM // 2, 1) * ta_ref[2])
            elif kind == "b":
                zk = (zk * tb_ref[0] + pltpu.roll(zk, DIFF_ROT // 2, 1) * tb_ref[1]
                      + pltpu.roll(zk, LANES - DIFF_ROT // 2, 1) * tb_ref[2])
            elif kind == "s":
                zk = jax.nn.sigmoid(zk)
            elif kind == "l":
                zk = _log_sigmoid(zk)
            o_ref[:, k * LANES:(k + 1) * LANES] = zk
        c0 += width


def _inproj(x, g, w_ext, b_ext, tab_a, tab_b, tm):
    n, d = x.shape
    npos = tab_a.shape[1]
    tm = min(tm, npos)
    nt = npos // tm
    ctot = w_ext.shape[1]
    return pl.pallas_call(
        _inproj_kernel,
        grid=(n // tm,),
        in_specs=[pl.BlockSpec((tm, d), lambda i: (i, 0)),
                  pl.BlockSpec((1, d), lambda i: (0, 0)),
                  pl.BlockSpec((d, ctot), lambda i: (0, 0)),
                  pl.BlockSpec((1, ctot), lambda i: (0, 0)),
                  pl.BlockSpec((3, tm, LANES), lambda i: (0, i % nt, 0)),
                  pl.BlockSpec((3, tm, LANES), lambda i: (0, i % nt, 0))],
        out_specs=[pl.BlockSpec((tm, wd), lambda i: (i, 0)) for _, wd, _ in _SEGS],
        out_shape=[jax.ShapeDtypeStruct((n, wd), F32) for _, wd, _ in _SEGS],
        compiler_params=_params(("parallel",)),
        name="inproj",
    )(x, g.reshape(1, d), w_ext, b_ext, tab_a, tab_b)


def _softmax_masked(s, mask):
    s = jnp.where(mask, s, NEG_INF)
    e = jnp.where(mask, jnp.exp(s - jnp.max(s, axis=-1, keepdims=True)), 0.0)
    return e / jnp.maximum(jnp.sum(e, axis=-1, keepdims=True), 1e-30)


def _online_update(s, mask, v, carry, vt=False):
    m, l, acc = carry
    r, tq, tk = s.shape
    s = jnp.where(mask, s, NEG_INF)
    m_new = jnp.maximum(m, jnp.max(s, axis=-1, keepdims=True))
    alpha = jnp.exp(m - m_new)
    p = jnp.where(mask, jnp.exp(s - m_new), 0.0)
    l = alpha * l + jnp.sum(p, axis=-1, keepdims=True)
    pb = p.reshape(r * tq, tk).astype(BF16)
    pv = (_dot_nt(pb, v) if vt else _dot(pb, v)).reshape(r, tq, LANES)
    return m_new, l, alpha * acc + pv


def _mask_bias(mask):
    return jnp.where(mask, 0.0, NEG_INF)


def _online_update_fast(s, v, carry, vt=False):
    m, l, acc = carry
    r, tq, tk = s.shape
    m_new = jnp.maximum(m, jnp.max(s, axis=-1, keepdims=True))
    alpha = jnp.exp(m - m_new)
    p = jnp.exp(s - m_new)
    l = alpha * l + jnp.sum(p, axis=-1, keepdims=True)
    pb = p.reshape(r * tq, tk).astype(BF16)
    pv = (_dot_nt(pb, v) if vt else _dot(pb, v)).reshape(r, tq, LANES)
    return m_new, l, alpha * acc + pv


def _online_init(r, tq, dv=LANES):
    return (jnp.full((r, tq, 1), NEG_INF, F32), jnp.zeros((r, tq, 1), F32), jnp.zeros((r, tq, dv), F32))


def _online_final(carry):
    _, l, acc = carry
    return acc / jnp.maximum(l, 1e-30)


def _stack_rows(ref, nblk, scale):
    parts = [ref[0, :, i * LANES:(i + 1) * LANES] for i in range(nblk)]
    return (jnp.concatenate(parts, axis=0) * scale).astype(BF16)


def _split3(hi):
    a = hi.astype(BF16)
    r1 = hi - a.astype(F32)
    b = r1.astype(BF16)
    c = (r1 - b.astype(F32)).astype(BF16)
    return a, b, c


def _compress_weights(w_ck, w_cv, pe_k, pe_v):
    d = HEAD_DIM
    wk = w_ck.reshape(CMP_BLOCK, d, d)
    wv = w_cv.reshape(CMP_BLOCK, d, d)
    wst = jnp.stack([wk, wk, wv, wv])
    eye = jnp.eye(4, dtype=F32)
    big = jnp.einsum("cpde,cf->pcdfe", wst, eye).reshape(CMP_BLOCK, 4 * d, 4 * d)
    w1 = big[:CMP_STRIDE].reshape(CMP_STRIDE * 4 * d, 4 * d).astype(BF16)
    w2 = big[CMP_STRIDE:].reshape(CMP_STRIDE * 4 * d, 4 * d).astype(BF16)
    pst = jnp.stack([pe_k, pe_k, pe_v, pe_v], axis=1)
    pe1 = jnp.broadcast_to(pst[:CMP_STRIDE].reshape(1, -1), (8, CMP_STRIDE * 4 * d)).astype(BF16)
    pe2 = jnp.broadcast_to(pst[CMP_STRIDE:].reshape(1, -1), (8, CMP_STRIDE * 4 * d)).astype(BF16)
    return w1, w2, pe1, pe2


def _compress_kernel(a_ref, w1_ref, w2_ref, pe1_ref, pe2_ref, f_ref, s_ref, b_ref):
    a = a_ref[0].astype(BF16)
    f_ref[0] = _dot(a, w1_ref[...])
    s_ref[0] = _dot(a, w2_ref[...])
    b_ref[...] = _dot(pe1_ref[...], w1_ref[...]) + _dot(pe2_ref[...], w2_ref[...])


def _compress(a, w1, w2, pe1, pe2):
    b, nseg, ka = a.shape
    ts = min(256, nseg)
    wspec = pl.BlockSpec((ka, 256), lambda i, j: (0, 0))
    pspec = pl.BlockSpec((8, ka), lambda i, j: (0, 0))
    ospec = pl.BlockSpec((1, ts, 256), lambda i, j: (i, j, 0))
    return pl.pallas_call(
        _compress_kernel,
        grid=(b, nseg // ts),
        in_specs=[pl.BlockSpec((1, ts, ka), lambda i, j: (i, j, 0)), wspec, wspec, pspec, pspec],
        out_specs=[ospec, ospec, pl.BlockSpec((8, 256), lambda i, j: (0, 0))],
        out_shape=[jax.ShapeDtypeStruct((b, nseg, 256), F32), jax.ShapeDtypeStruct((b, nseg, 256), F32),
                   jax.ShapeDtypeStruct((8, 256), F32)],
        compiler_params=_params(("arbitrary", "arbitrary")),
        name="compress",
    )(a, w1, w2, pe1, pe2)


def _select_blocks(psum, qpos, nb, sc_ref):
    tq, ncmp = psum.shape
    nbp = sc_ref.shape[0]
    n_i = lax.broadcasted_iota(jnp.int32, (ncmp, nbp), 0)
    j_i = lax.broadcasted_iota(jnp.int32, (ncmp, nbp), 1)
    dlt = n_i - (SEL_BLOCK // CMP_STRIDE) * j_i
    wmat = jnp.where((dlt == -1) | (dlt == 3), 1.0, jnp.where((dlt >= 0) & (dlt <= 2), 2.0, 0.0)).astype(BF16)
    p_hi = psum.astype(BF16)
    p_lo = (psum - p_hi.astype(F32)).astype(BF16)
    imp = _dot(p_hi, wmat) + _dot(p_lo, wmat)
    blk = lax.broadcasted_iota(jnp.int32, (tq, nbp), 1)
    cur = qpos // SEL_BLOCK
    forced = (blk == 0) | (blk == cur) | (blk == cur - 1)
    score = jnp.where(forced, FORCE_SCORE, jnp.where(blk <= cur, imp, NEG_INF))
    nbr = -(-nb // 8) * 8
    sc = score.T[0:nbr, :]
    rowf = lax.broadcasted_iota(jnp.int32, (nbr, tq), 0).astype(F32)

    def take_max(_, carry):
        s, sel = carry
        m = jnp.max(s, axis=0, keepdims=True)
        idx = jnp.min(jnp.where(s == m, rowf, float(nbr)), axis=0, keepdims=True)
        hit = rowf == idx
        return jnp.where(hit, -jnp.inf, s), jnp.where(hit, 1.0, sel)

    _, sel = lax.fori_loop(0, min(N_SEL, nb), take_max, (sc, jnp.zeros((nbr, tq), F32)))
    if nbr < nbp:
        sel = jnp.concatenate([sel, jnp.zeros((nbp - nbr, tq), F32)], axis=0)
    return sel.T


def _expand_blocks(sel, k0, tk):
    nbp = sel.shape[1]
    j_i = lax.broadcasted_iota(jnp.int32, (nbp, tk), 0)
    s_i = lax.broadcasted_iota(jnp.int32, (nbp, tk), 1) + k0
    e = jnp.where(lax.shift_right_logical(s_i, 6) == j_i, 1.0, 0.0).astype(BF16)
    return _dot(sel.astype(BF16), e) > 0.5


def _cmp_branch(qn, kcmp, vcmp, qpos, nseg, ncmp_valid, pos0=0):
    tq = qpos.shape[0]
    n_i = lax.broadcasted_iota(jnp.int32, (1, nseg), 1)
    m_c = ((n_i * CMP_STRIDE + (CMP_BLOCK - 1) + pos0) <= qpos) & (n_i < ncmp_valid)
    s_c = _dot_nt(qn, kcmp).reshape(NSA_GROUP, tq, nseg)
    p_c = _softmax_masked(s_c, m_c[None])
    o_c = _dot(p_c.reshape(NSA_GROUP * tq, nseg).astype(BF16), vcmp).reshape(NSA_GROUP, tq, LANES)
    return o_c, jnp.sum(p_c, axis=0)


def _nsa_prompt_kernel(qn_ref, qr_ref, f_ref, s_ref, cb_ref, sel_ref, win_ref, gate_ref, o_ref,
                       ks16, vs16, kw16, vw16, sc_ref, *, tq, tk, nseg, nb):
    qi = pl.program_id(1)
    q0 = qi * tq

    @pl.when(qi == 0)
    def _():
        ks16[...] = sel_ref[0, :, 0:LANES].astype(BF16)
        vs16[...] = sel_ref[0, :, LANES:2 * LANES].astype(BF16)
        kw16[...] = win_ref[0, :, 0:LANES].astype(BF16)
        vw16[...] = win_ref[0, :, LANES:2 * LANES].astype(BF16)

    scale = HEAD_DIM ** -0.5
    qpos = q0 + lax.broadcasted_iota(jnp.int32, (tq, 1), 0)
    kv = f_ref[0] + pltpu.roll(s_ref[0], nseg - 1, 0) + cb_ref[0:1, :]
    kcmp = kv[:, 0:LANES].astype(BF16)
    vcmp = kv[:, LANES:2 * LANES].astype(BF16)
    wspan = WINDOW + tq
    kstart = pl.multiple_of(jnp.maximum(q0 - WINDOW, 0), tq)
    nkt = (q0 + tq + tk - 1) // tk

    for g in range(NSA_KV_HEADS):
        qn = _stack_rows(qn_ref.at[:, :, g * NSA_GROUP * LANES:(g + 1) * NSA_GROUP * LANES], NSA_GROUP, scale)
        o_c, psum = _cmp_branch(qn, kcmp, vcmp, qpos, nseg, nseg - 1)
        sel = _select_blocks(psum, qpos, nb, sc_ref)
        qr = _stack_rows(qr_ref.at[:, :, g * NSA_GROUP * LANES:(g + 1) * NSA_GROUP * LANES], NSA_GROUP, scale)

        def body(kt, carry, qr=qr, sel=sel):
            k0 = pl.multiple_of(kt * tk, tk)
            s = _dot_nt(qr, ks16[pl.ds(k0, tk), :]).reshape(NSA_GROUP, tq, tk)
            kpos = k0 + lax.broadcasted_iota(jnp.int32, (1, tk), 1)
            bias = jnp.where(_expand_blocks(sel, k0, tk), _mask_bias(kpos <= qpos), NEG_INF)
            return _online_update_fast(s + bias[None], vs16[pl.ds(k0, tk), :], carry)

        o_s = _online_final(lax.fori_loop(0, nkt, body, _online_init(NSA_GROUP, tq)))

        kpos = kstart + lax.broadcasted_iota(jnp.int32, (1, wspan), 1)
        m_w = (kpos <= qpos) & (kpos > qpos - WINDOW)
        s_w = _dot_nt(qr, kw16[pl.ds(kstart, wspan), :]).reshape(NSA_GROUP, tq, wspan)
        p_w = _softmax_masked(s_w, m_w[None])
        o_w = _dot(p_w.reshape(NSA_GROUP * tq, wspan).astype(BF16), vw16[pl.ds(kstart, wspan), :])
        o_w = o_w.reshape(NSA_GROUP, tq, LANES)

        gates = gate_ref[0, :, g * LANES:(g + 1) * LANES]
        for i in range(NSA_GROUP):
            o = (gates[:, 3 * i:3 * i + 1] * o_c[i] + gates[:, 3 * i + 1:3 * i + 2] * o_s[i]
                 + gates[:, 3 * i + 2:3 * i + 3] * o_w[i])
            hblk = g * NSA_GROUP + i
            o_ref[0, :, hblk * LANES:(hblk + 1) * LANES] = o.astype(o_ref.dtype)


def _nsa_prompt(qn, qr, f, s, cb, sel, win, gate, tq=256, tk=512):
    b, t, _ = qn.shape
    nseg = f.shape[1]
    nb = -(-t // SEL_BLOCK)
    nbp = -(-nb // LANES) * LANES
    tk = min(tk, t)
    assert t % tk == 0 and t % tq == 0 and t >= WINDOW + tq
    qspec = pl.BlockSpec((1, tq, NSA_HEADS * LANES), lambda i, j: (i, j, 0))
    fspec = pl.BlockSpec((1, nseg, 256), lambda i, j: (i, 0, 0))
    kspec = pl.BlockSpec((1, t, 256), lambda i, j: (i, 0, 0))
    return pl.pallas_call(
        functools.partial(_nsa_prompt_kernel, tq=tq, tk=tk, nseg=nseg, nb=nb),
        grid=(b, t // tq),
        in_specs=[qspec, qspec, fspec, fspec, pl.BlockSpec((8, 256), lambda i, j: (0, 0)), kspec, kspec,
                  pl.BlockSpec((1, tq, 3 * LANES), lambda i, j: (i, j, 0))],
        out_specs=qspec,
        out_shape=jax.ShapeDtypeStruct((b, t, NSA_HEADS * LANES), BF16),
        scratch_shapes=[pltpu.VMEM((t, LANES), BF16)] * 4 + [pltpu.VMEM((nbp, tq), F32)],
        compiler_params=_params(("arbitrary", "arbitrary")),
        name="nsa_prompt",
    )(qn, qr, f, s, cb, sel, win, gate)


def _cumsum_kernel(lf_ref, col_ref, row_ref, carry_ref, *, tm):
    j = pl.program_id(1)

    @pl.when(j == 0)
    def _():
        carry_ref[...] = jnp.zeros_like(carry_ref)

    r_i = lax.broadcasted_iota(jnp.int32, (tm, tm), 0)
    c_i = lax.broadcasted_iota(jnp.int32, (tm, tm), 1)
    tri = jnp.where(c_i <= r_i, 1.0, 0.0).astype(BF16)
    a, b, c = _split3(lf_ref[0])
    cs = _dot(tri, a) + _dot(tri, b) + _dot(tri, c) + carry_ref[0:1, :]
    carry_ref[...] = jnp.broadcast_to(cs[tm - 1:tm, :], carry_ref.shape)
    sh = pltpu.roll(cs, LANES - 2, 1)
    col_ref[0, :, 0:LANES] = cs
    col_ref[0, :, LANES:2 * LANES] = sh
    row_ref[0, 0:8, :] = cs.T[0:8, :]
    row_ref[0, 8:16, :] = sh.T[0:8, :]


def _cumsum(gate, tm=512):
    b, t, _ = gate.shape
    tm = min(tm, t)
    return pl.pallas_call(
        functools.partial(_cumsum_kernel, tm=tm),
        grid=(b, t // tm),
        in_specs=[pl.BlockSpec((1, tm, LANES), lambda i, j: (i, j, 2))],
        out_specs=[pl.BlockSpec((1, tm, 2 * LANES), lambda i, j: (i, j, 0)),
                   pl.BlockSpec((1, 16, tm), lambda i, j: (i, 0, j))],
        out_shape=[jax.ShapeDtypeStruct((b, t, 2 * LANES), F32), jax.ShapeDtypeStruct((b, 16, t), F32)],
        scratch_shapes=[pltpu.VMEM((8, LANES), F32)],
        compiler_params=_params(("arbitrary", "arbitrary")),
        name="cumsum",
    )(gate)


def _diff_lambda(dl_ref, lam_init):
    dl = dl_ref[...]
    a = jnp.sum(dl[0:1] * dl[1:2], axis=-1, keepdims=True)
    b = jnp.sum(dl[2:3] * dl[3:4], axis=-1, keepdims=True)
    return jnp.exp(a) - jnp.exp(b) + lam_init


def _diff_finish(o, lam, gain_ref, lam_init, o_ref):
    lane = lax.broadcasted_iota(jnp.int32, (1, LANES), 1)
    for hh in range(2):
        w = o[2 * hh] - lam * o[2 * hh + 1]
        keep = jnp.where((lane >= hh * HEAD_DIM) & (lane < (hh + 1) * HEAD_DIM), 1.0, 0.0)
        w = w * keep
        ms = jnp.sum(w * w, axis=-1, keepdims=True) * (1.0 / HEAD_DIM)
        y = w * lax.rsqrt(ms + EPS) * gain_ref[...] * (1.0 - lam_init)
        o_ref[0, :, hh * LANES:(hh + 1) * LANES] = y.astype(o_ref.dtype)


def _diff_prompt_kernel(q_ref, k_ref, v_ref, dl_ref, gain_ref, o_ref, k16, v16, *, tq, tk, lam_init):
    qi = pl.program_id(2)
    q0 = qi * tq

    @pl.when(qi == 0)
    def _():
        k16[...] = k_ref[0].astype(BF16)
        v16[...] = v_ref[0].astype(BF16)

    qpos = q0 + lax.broadcasted_iota(jnp.int32, (tq, 1), 0)
    q = _stack_rows(q_ref, 4, DIFF_HALF ** -0.5)

    def body(kt, carry):
        k0 = pl.multiple_of(kt * tk, tk)
        s = _dot_nt(q, k16[pl.ds(k0, tk), :]).reshape(4, tq, tk)
        kpos = k0 + lax.broadcasted_iota(jnp.int32, (1, tk), 1)
        return _online_update_fast(s + _mask_bias(kpos <= qpos)[None], v16[pl.ds(k0, tk), :], carry)

    o = _online_final(lax.fori_loop(0, (q0 + tq + tk - 1) // tk, body, _online_init(4, tq)))
    _diff_finish(o, _diff_lambda(dl_ref, lam_init), gain_ref, lam_init, o_ref)


def _gain_lanes(gain):
    return jnp.tile(gain.reshape(1, HEAD_DIM), (1, LANES // HEAD_DIM))


def _diff_prompt(dq, rows, dl, gain, lam_init, tq=256, tk=512):
    b, t, _ = dq.shape
    tk = min(tk, t)
    return pl.pallas_call(
        functools.partial(_diff_prompt_kernel, tq=tq, tk=tk, lam_init=lam_init),
        grid=(b, 2, t // tq),
        in_specs=[pl.BlockSpec((1, tq, 4 * LANES), lambda i, kb, j: (i, j, kb)),
                  pl.BlockSpec((1, t, LANES), lambda i, kb, j: (i, 0, kb)),
                  pl.BlockSpec((1, t, LANES), lambda i, kb, j: (i, 0, 2 + kb)),
                  pl.BlockSpec((4, DIFF_HALF), lambda i, kb, j: (0, 0)),
                  pl.BlockSpec((1, LANES), lambda i, kb, j: (0, 0))],
        out_specs=pl.BlockSpec((1, tq, 2 * LANES), lambda i, kb, j: (i, j, kb)),
        out_shape=jax.ShapeDtypeStruct((b, t, 4 * LANES), BF16),
        scratch_shapes=[pltpu.VMEM((t, LANES), BF16)] * 2,
        compiler_params=_params(("arbitrary", "arbitrary", "arbitrary")),
        name="diff_prompt",
    )(dq, rows, rows, dl, _gain_lanes(gain))


def _fox_prompt_kernel(q_ref, k_ref, v_ref, cc_ref, cr_ref, o_ref, k16, v16, *, tq, tk):
    qi = pl.program_id(2)
    q0 = qi * tq

    @pl.when(qi == 0)
    def _():
        k16[...] = k_ref[0].astype(BF16)
        v16[...] = v_ref[0].astype(BF16)

    qpos = q0 + lax.broadcasted_iota(jnp.int32, (tq, 1), 0)
    q = _stack_rows(q_ref, 2, HEAD_DIM ** -0.5)
    cq = jnp.stack([cc_ref[0, :, 0:1], cc_ref[0, :, 1:2]])

    def body(kt, carry):
        k0 = pl.multiple_of(kt * tk, tk)
        ck = cr_ref[0, 0:2, pl.ds(k0, tk)][:, None, :]
        kpos = k0 + lax.broadcasted_iota(jnp.int32, (1, tk), 1)
        s = _dot_nt(q, k16[pl.ds(k0, tk), :]).reshape(2, tq, tk) + cq - ck + _mask_bias(kpos <= qpos)[None]
        return _online_update_fast(s, v16[pl.ds(k0, tk), :], carry)

    o = _online_final(lax.fori_loop(0, (q0 + tq + tk - 1) // tk, body, _online_init(2, tq)))
    for hh in range(2):
        o_ref[0, :, hh * LANES:(hh + 1) * LANES] = o[hh].astype(o_ref.dtype)


def _fox_prompt(fq, rows, ccol, crow, tq=256, tk=512):
    b, t, _ = fq.shape
    tk = min(tk, t)
    tq = min(tq, t)
    return pl.pallas_call(
        functools.partial(_fox_prompt_kernel, tq=tq, tk=tk),
        grid=(b, 2, t // tq),
        in_specs=[pl.BlockSpec((1, tq, 2 * LANES), lambda i, kb, j: (i, j, kb)),
                  pl.BlockSpec((1, t, LANES), lambda i, kb, j: (i, 0, kb)),
                  pl.BlockSpec((1, t, LANES), lambda i, kb, j: (i, 0, 2 + kb)),
                  pl.BlockSpec((1, tq, LANES), lambda i, kb, j: (i, j, kb)),
                  pl.BlockSpec((1, 8, t), lambda i, kb, j: (i, kb, 0))],
        out_specs=pl.BlockSpec((1, tq, 2 * LANES), lambda i, kb, j: (i, j, kb)),
        out_shape=jax.ShapeDtypeStruct((b, t, 4 * LANES), BF16),
        scratch_shapes=[pltpu.VMEM((t, LANES), BF16)] * 2,
        compiler_params=_params(("arbitrary", "arbitrary", "arbitrary")),
        name="fox_prompt",
    )(fq, rows, rows, ccol, crow)


def _mem_attn_kernel(q_ref, kv_ref, o_ref, *, dh):
    kv = kv_ref[0].astype(BF16)
    nh = q_ref.shape[-1] // dh
    for h in range(nh):
        q = (q_ref[0, :, h * dh:(h + 1) * dh] * dh ** -0.5).astype(BF16)
        s = _dot_nt(q, kv[:, h * dh:(h + 1) * dh])
        e = jnp.exp(s - jnp.max(s, axis=-1, keepdims=True))
        p = e / jnp.sum(e, axis=-1, keepdims=True)
        o = _dot(p.astype(BF16), kv[:, (nh + h) * dh:(nh + h + 1) * dh])
        o_ref[0, :, h * dh:(h + 1) * dh] = o.astype(o_ref.dtype)


def _mem_attn(q, kv, tq=512):
    b, t, d = q.shape
    m = kv.shape[1]
    tq = min(tq, t)
    return pl.pallas_call(
        functools.partial(_mem_attn_kernel, dh=d // MEM_HEADS),
        grid=(b, t // tq),
        in_specs=[pl.BlockSpec((1, tq, d), lambda i, j: (i, j, 0)),
                  pl.BlockSpec((1, m, 2 * d), lambda i, j: (i, 0, 0))],
        out_specs=pl.BlockSpec((1, tq, d), lambda i, j: (i, j, 0)),
        out_shape=jax.ShapeDtypeStruct((b, t, d), BF16),
        compiler_params=_params(("parallel", "arbitrary")),
        name="mem_attn",
    )(q, kv)


def _final_norm_kernel(x_ref, g_ref, o_ref):
    o_ref[...] = _rms(x_ref[...], g_ref[...])


def _final_norm(x, g):
    n, d = x.shape
    tm = min(512, n)
    return pl.pallas_call(
        _final_norm_kernel,
        grid=(n // tm,),
        in_specs=[pl.BlockSpec((tm, d), lambda i: (i, 0)), pl.BlockSpec((1, d), lambda i: (0, 0))],
        out_specs=pl.BlockSpec((tm, d), lambda i: (i, 0)),
        out_shape=jax.ShapeDtypeStruct((n, d), F32),
        compiler_params=_params(("parallel",)),
        name="final_norm",
    )(x, g.reshape(1, d))


PAGES = 16
PAGES_CMP = 32
PAGES_LOGF = 64
TP = 8


def _pages_per_step(n_pages, want):
    npg = min(want, n_pages)
    while n_pages % npg:
        npg //= 2
    return npg


def _page_specs(block, npg, layer):
    def imap(i, j, pt, k):
        return (layer, pt[i, j * npg + k]) + (0,) * len(block)
    return [pl.BlockSpec((1, 1) + block, functools.partial(imap, k=k)) for k in range(npg)]


def _feature_major(cache):
    nd = cache.ndim
    t = jnp.transpose(cache, (0, 1) + tuple(range(3, nd)) + (2,))
    return t.reshape(t.shape[0], t.shape[1], -1, t.shape[-1])


def _paged_call(kern, pt, layer, pools, pool_block, others, other_specs, out_specs, out_shape, scratch, name,
                pages=PAGES):
    b, n_pages = pt.shape
    npg = _pages_per_step(n_pages, pages)
    in_specs = []
    args = []
    for pool in pools:
        in_specs += _page_specs(pool_block, npg, layer)
        args += [pool] * npg
    in_specs += other_specs
    args += others
    gs = pltpu.PrefetchScalarGridSpec(num_scalar_prefetch=1, grid=(b, n_pages // npg), in_specs=in_specs,
                                      out_specs=out_specs, scratch_shapes=scratch)
    return pl.pallas_call(functools.partial(kern, npg=npg), grid_spec=gs, out_shape=out_shape,
                          compiler_params=_params(("arbitrary", "arbitrary")), name=name)(pt, *args)


def _cumsum_paged_kernel(pt_ref, *refs, npg):
    pages = refs[:npg]
    row_ref, carry_ref = refs[npg], refs[npg + 1]
    j = pl.program_id(1)

    @pl.when(j == 0)
    def _():
        carry_ref[...] = jnp.zeros_like(carry_ref)

    psz = pages[0].shape[-1]
    r_i = lax.broadcasted_iota(jnp.int32, (psz, psz), 0)
    c_i = lax.broadcasted_iota(jnp.int32, (psz, psz), 1)
    tri = jnp.where(r_i <= c_i, 1.0, 0.0).astype(BF16)
    rows = 8 * npg
    zpad = jnp.zeros((8 - FOX_HEADS, psz), F32)
    x = jnp.concatenate([blk for k in range(npg) for blk in (pages[k][0, 0], zpad)], axis=0)
    a, b, c = _split3(x)
    local = _dot(a, tri) + _dot(b, tri) + _dot(c, tri)
    tot = jnp.broadcast_to(local[:, psz - 1:psz], (rows, psz))
    p_r = lax.broadcasted_iota(jnp.int32, (rows, rows), 0)
    p_c = lax.broadcasted_iota(jnp.int32, (rows, rows), 1)
    earlier = jnp.where((p_c < p_r) & ((p_r - p_c) % 8 == 0), 1.0, 0.0).astype(BF16)
    ta, tb, tc = _split3(tot)
    offs = _dot(earlier, ta) + _dot(earlier, tb) + _dot(earlier, tc)
    cs = local + offs + jnp.tile(carry_ref[...], (npg, 1))
    for k in range(npg):
        row_ref[0, :, k * psz:(k + 1) * psz] = cs[8 * k:8 * (k + 1), :]
    carry_ref[...] = jnp.broadcast_to(cs[rows - 8:rows, psz - 1:psz], carry_ref.shape)


def _cumsum_paged(pt, layer, logf_pool_t):
    b, n_pages = pt.shape
    psz = logf_pool_t.shape[-1]
    npg = _pages_per_step(n_pages, PAGES_LOGF)
    return _paged_call(
        _cumsum_paged_kernel, pt, layer, [logf_pool_t], (FOX_HEADS, psz), [], [],
        pl.BlockSpec((1, 8, npg * psz), lambda i, j, pt: (i, 0, j)),
        jax.ShapeDtypeStruct((b, 8, n_pages * psz), F32),
        [pltpu.VMEM((8, psz), F32)], "cumsum_paged", pages=PAGES_LOGF)


def _compress_paged_kernel(pt_ref, *refs, npg):
    pages = refs[:npg]
    w1_ref, w2_ref, f_ref, s_ref, a_ref = refs[npg:npg + 5]
    psz = pages[0].shape[-1]
    spp = psz // CMP_STRIDE
    r_i = lax.broadcasted_iota(jnp.int32, (psz, psz), 0)
    c_i = lax.broadcasted_iota(jnp.int32, (psz, psz), 1)
    perm = jnp.where(c_i == CMP_STRIDE * (r_i % spp) + r_i // spp, 1.0, 0.0).astype(BF16)
    for k in range(npg):
        xp = _dot_nt(perm, pages[k][0, 0].astype(BF16)).astype(BF16)
        for p in range(CMP_STRIDE):
            a_ref[k * spp:(k + 1) * spp, p * 256:(p + 1) * 256] = xp[p * spp:(p + 1) * spp, :]
    a = a_ref[...]
    f_ref[0] = _dot(a, w1_ref[...])
    s_ref[0] = _dot(a, w2_ref[...])


def _compress_paged(pt, layer, pool, w1, w2):
    b, n_pages = pt.shape
    psz = pool.shape[-1]
    spp = psz // CMP_STRIDE
    npg = _pages_per_step(n_pages, PAGES_CMP)
    nseg = n_pages * spp
    wspec = pl.BlockSpec(w1.shape, lambda i, j, pt: (0, 0))
    ospec = pl.BlockSpec((1, npg * spp, 256), lambda i, j, pt: (i, j, 0))
    return _paged_call(
        _compress_paged_kernel, pt, layer, [pool], (256, psz), [w1, w2], [wspec, wspec],
        [ospec, ospec],
        [jax.ShapeDtypeStruct((b, nseg, 256), F32)] * 2,
        [pltpu.VMEM((npg * spp, CMP_STRIDE * 256), BF16)], "compress_paged", pages=PAGES_CMP)


def _nsa_sample_cmp_kernel(qn_ref, f_ref, s_ref, cb_ref, oc_ref, ps_ref, *, nseg, pos0, n_new):
    scale = HEAD_DIM ** -0.5
    tq = qn_ref.shape[1]
    qpos = pos0 + jnp.minimum(lax.broadcasted_iota(jnp.int32, (tq, 1), 0), n_new - 1)
    kv = f_ref[0] + pltpu.roll(s_ref[0], nseg - 1, 0) + cb_ref[0:1, :]
    kcmp = kv[:, 0:LANES].astype(BF16)
    vcmp = kv[:, LANES:2 * LANES].astype(BF16)
    for g in range(NSA_KV_HEADS):
        qn = _stack_rows(qn_ref.at[:, :, g * NSA_GROUP * LANES:(g + 1) * NSA_GROUP * LANES], NSA_GROUP, scale)
        o_c, psum = _cmp_branch(qn, kcmp, vcmp, qpos, nseg, nseg - 1)
        for i in range(NSA_GROUP):
            hblk = g * NSA_GROUP + i
            oc_ref[0, :, hblk * LANES:(hblk + 1) * LANES] = o_c[i]
        ps_ref[0, g] = psum


def _nsa_sample_cmp(qn, f, s, cb, pos0, n_new):
    b, tq, _ = qn.shape
    nseg = f.shape[1]
    fspec = pl.BlockSpec((1, nseg, 256), lambda i: (i, 0, 0))
    return pl.pallas_call(
        functools.partial(_nsa_sample_cmp_kernel, nseg=nseg, pos0=pos0, n_new=n_new),
        grid=(b,),
        in_specs=[pl.BlockSpec((1, tq, NSA_HEADS * LANES), lambda i: (i, 0, 0)), fspec, fspec,
                  pl.BlockSpec((8, 256), lambda i: (0, 0))],
        out_specs=[pl.BlockSpec((1, tq, NSA_HEADS * LANES), lambda i: (i, 0, 0)),
                   pl.BlockSpec((1, NSA_KV_HEADS, tq, nseg), lambda i: (i, 0, 0, 0))],
        out_shape=[jax.ShapeDtypeStruct((b, tq, NSA_HEADS * LANES), F32),
                   jax.ShapeDtypeStruct((b, NSA_KV_HEADS, tq, nseg), F32)],
        compiler_params=_params(("parallel",)),
        name="nsa_sample_cmp",
    )(qn, f, s, cb)


def _select_rows_kernel(ps_ref, sel_ref, sc_ref, *, nb, pos0, tq, n_new):
    rows = ps_ref.shape[0]
    qpos = pos0 + jnp.minimum(lax.broadcasted_iota(jnp.int32, (rows, 1), 0) % tq, n_new - 1)
    sel_ref[...] = _select_blocks(ps_ref[...], qpos, nb, sc_ref)


def _select_rows(psum, nb, pos0, tq, n_new):
    rows, ncmp = psum.shape
    nbp = -(-nb // LANES) * LANES
    return pl.pallas_call(
        functools.partial(_select_rows_kernel, nb=nb, pos0=pos0, tq=tq, n_new=n_new),
        grid=(1,),
        in_specs=[pl.BlockSpec((rows, ncmp), lambda i: (0, 0))],
        out_specs=pl.BlockSpec((rows, nbp), lambda i: (0, 0)),
        out_shape=jax.ShapeDtypeStruct((rows, nbp), F32),
        scratch_shapes=[pltpu.VMEM((nbp, rows), F32)],
        compiler_params=_params(("arbitrary",)),
        name="select_rows",
    )(psum)


def _state_update(st_refs, g, carry_fn):
    m_ref, l_ref, acc_ref = st_refs
    m, l, acc = carry_fn((m_ref[g], l_ref[g], acc_ref[g]))
    m_ref[g] = m
    l_ref[g] = l
    acc_ref[g] = acc


def _state_init(st_refs):
    m_ref, l_ref, acc_ref = st_refs
    m_ref[...] = jnp.full(m_ref.shape, NEG_INF, F32)
    l_ref[...] = jnp.zeros(l_ref.shape, F32)
    acc_ref[...] = jnp.zeros(acc_ref.shape, F32)


def _state_scratch(groups, r, tq):
    return [pltpu.VMEM((groups, r, tq, 1), F32), pltpu.VMEM((groups, r, tq, 1), F32),
            pltpu.VMEM((groups, r, tq, LANES), F32)]


def _nsa_sample_kernel(pt_ref, *refs, npg, pos0, n_new):
    pages = refs[:npg]
    (qr_ref, sel_ref, new_ref, wst_ref, wnew_ref, oc_ref, gate_ref, o_ref,
     m_ref, l_ref, acc_ref, k16, v16) = refs[npg:]
    st = (m_ref, l_ref, acc_ref)
    j = pl.program_id(1)
    tq = qr_ref.shape[1]
    psz = pages[0].shape[-1]
    tk = npg * psz
    scale = HEAD_DIM ** -0.5
    qpos = pos0 + jnp.minimum(lax.broadcasted_iota(jnp.int32, (tq, 1), 0), n_new - 1)

    @pl.when(j == 0)
    def _():
        _state_init(st)

    for k in range(npg):
        k16[:, k * psz:(k + 1) * psz] = pages[k][0, 0, 0:LANES, :].astype(BF16)
        v16[:, k * psz:(k + 1) * psz] = pages[k][0, 0, LANES:2 * LANES, :].astype(BF16)
    k0 = j * tk
    kpos = k0 + lax.broadcasted_iota(jnp.int32, (1, tk), 1)
    qrs = []
    for g in range(NSA_KV_HEADS):
        qr = _stack_rows(qr_ref.at[:, :, g * NSA_GROUP * LANES:(g + 1) * NSA_GROUP * LANES], NSA_GROUP, scale)
        qrs.append(qr)
        bias = jnp.where(_expand_blocks(sel_ref[0, g], k0, tk), _mask_bias(kpos <= qpos), NEG_INF)
        s = _dot(qr, k16[...]).reshape(NSA_GROUP, tq, tk) + bias[None]
        _state_update(st, g, functools.partial(_online_update_fast, s, v16[...], vt=True))

    @pl.when(j == pl.num_programs(1) - 1)
    def _():
        nrow = new_ref.shape[1]
        r_i = lax.broadcasted_iota(jnp.int32, (1, nrow), 1)
        npos = pos0 + r_i
        m_new = (r_i < n_new) & (npos <= qpos)
        wlen = wst_ref.shape[-1]
        wpos = pos0 - wlen + lax.broadcasted_iota(jnp.int32, (1, wlen), 1)
        m_old = (wpos > qpos - WINDOW) & (wpos >= 0)
        m_wnew = m_new & (npos > qpos - WINDOW)
        kn = new_ref[0, :, 0:LANES].astype(BF16)
        vn = new_ref[0, :, LANES:2 * LANES].astype(BF16)
        kwo = wst_ref[0, 0, 0:LANES, :].astype(BF16)
        vwo = wst_ref[0, 0, LANES:2 * LANES, :].astype(BF16)
        kwn = wnew_ref[0, :, 0:LANES].astype(BF16)
        vwn = wnew_ref[0, :, LANES:2 * LANES].astype(BF16)
        for g in range(NSA_KV_HEADS):
            qr = qrs[g]
            blk_ok = _expand_blocks(sel_ref[0, g], pos0, nrow)
            s = _dot_nt(qr, kn).reshape(NSA_GROUP, tq, nrow)
            _state_update(st, g, functools.partial(_online_update, s, (m_new & blk_ok)[None], vn))
            o_s = _online_final((m_ref[g], l_ref[g], acc_ref[g]))
            cw = _online_init(NSA_GROUP, tq)
            cw = _online_update(_dot(qr, kwo).reshape(NSA_GROUP, tq, wlen), m_old[None], vwo, cw, vt=True)
            cw = _online_update(_dot_nt(qr, kwn).reshape(NSA_GROUP, tq, nrow), m_wnew[None], vwn, cw)
            o_w = _online_final(cw)
            gates = gate_ref[0, :, g * LANES:(g + 1) * LANES]
            for i in range(NSA_GROUP):
                hblk = g * NSA_GROUP + i
                o_c = oc_ref[0, :, hblk * LANES:(hblk + 1) * LANES]
                o = (gates[:, 3 * i:3 * i + 1] * o_c + gates[:, 3 * i + 1:3 * i + 2] * o_s[i]
                     + gates[:, 3 * i + 2:3 * i + 3] * o_w[i])
                o_ref[0, :, hblk * LANES:(hblk + 1) * LANES] = o.astype(o_ref.dtype)


def _nsa_sample(pt, layer, pool, qr, sel, new, wst, wnew, oc, gate, pos0, n_new):
    b, n_pages = pt.shape
    psz = pool.shape[-1]
    npg = _pages_per_step(n_pages, PAGES)
    tq = qr.shape[1]
    nbp = sel.shape[-1]
    wlen = wst.shape[-1]
    full = lambda shp: pl.BlockSpec((1,) + shp, lambda i, j, pt: (i,) + (0,) * len(shp))
    return _paged_call(
        functools.partial(_nsa_sample_kernel, pos0=pos0, n_new=n_new), pt, layer, [pool], (256, psz),
        [qr, sel, new, wst, wnew, oc, gate],
        [full((tq, NSA_HEADS * LANES)), full((NSA_KV_HEADS, tq, nbp)), full((tq, 256)),
         pl.BlockSpec((1, 1, 256, wlen), lambda i, j, pt: (layer, i, 0, 0)),
         full((tq, 256)), full((tq, NSA_HEADS * LANES)), full((tq, 3 * LANES))],
        full((tq, NSA_HEADS * LANES)),
        jax.ShapeDtypeStruct((b, tq, NSA_HEADS * LANES), BF16),
        _state_scratch(NSA_KV_HEADS, NSA_GROUP, tq)
        + [pltpu.VMEM((LANES, npg * psz), BF16)] * 2, "nsa_sample", pages=PAGES)


def _diff_sample_kernel(pt_ref, *refs, npg, pos0, n_new, lam_init):
    pages = refs[:npg]
    q_ref, new_ref, dl_ref, gain_ref, o_ref, m_ref, l_ref, acc_ref, k16, v16 = refs[npg:]
    st = (m_ref, l_ref, acc_ref)
    j = pl.program_id(1)
    tq = q_ref.shape[1]
    psz = pages[0].shape[-1]
    tk = npg * psz
    qpos = pos0 + jnp.minimum(lax.broadcasted_iota(jnp.int32, (tq, 1), 0), n_new - 1)

    @pl.when(j == 0)
    def _():
        _state_init(st)

    kpos = j * tk + lax.broadcasted_iota(jnp.int32, (1, tk), 1)
    bias = _mask_bias(kpos <= qpos)[None]
    qs = []
    for kb in range(2):
        for k in range(npg):
            k16[kb, :, k * psz:(k + 1) * psz] = pages[k][0, 0, kb * LANES:(kb + 1) * LANES, :].astype(BF16)
            v16[kb, :, k * psz:(k + 1) * psz] = pages[k][0, 0, (2 + kb) * LANES:(3 + kb) * LANES, :].astype(BF16)
        q = _stack_rows(q_ref.at[:, :, kb * 4 * LANES:(kb + 1) * 4 * LANES], 4, DIFF_HALF ** -0.5)
        qs.append(q)
        s = _dot(q, k16[kb]).reshape(4, tq, tk) + bias
        _state_update(st, kb, functools.partial(_online_update_fast, s, v16[kb], vt=True))

    @pl.when(j == pl.num_programs(1) - 1)
    def _():
        nrow = new_ref.shape[1]
        r_i = lax.broadcasted_iota(jnp.int32, (1, nrow), 1)
        m_new = ((r_i < n_new) & (pos0 + r_i <= qpos))[None]
        lam = _diff_lambda(dl_ref, lam_init)
        for kb in range(2):
            kn = new_ref[0, :, kb * LANES:(kb + 1) * LANES].astype(BF16)
            vn = new_ref[0, :, (2 + kb) * LANES:(3 + kb) * LANES].astype(BF16)
            s = _dot_nt(qs[kb], kn).reshape(4, tq, nrow)
            _state_update(st, kb, functools.partial(_online_update, s, m_new, vn))
            o = _online_final((m_ref[kb], l_ref[kb], acc_ref[kb]))
            _diff_finish(o, lam, gain_ref, lam_init, o_ref.at[:, :, kb * 2 * LANES:(kb + 1) * 2 * LANES])


def _diff_sample(pt, layer, pool, dq, new, dl, gain, lam_init, pos0, n_new):
    b, n_pages = pt.shape
    psz = pool.shape[-1]
    npg = _pages_per_step(n_pages, PAGES)
    tq = dq.shape[1]
    full = lambda shp: pl.BlockSpec((1,) + shp, lambda i, j, pt: (i,) + (0,) * len(shp))
    const = lambda shp: pl.BlockSpec(shp, lambda i, j, pt: (0,) * len(shp))
    return _paged_call(
        functools.partial(_diff_sample_kernel, pos0=pos0, n_new=n_new, lam_init=lam_init), pt, layer, [pool],
        (512, psz), [dq, new, dl, _gain_lanes(gain)],
        [full((tq, 8 * LANES)), full((tq, 512)), const((4, DIFF_HALF)), const((1, LANES))],
        full((tq, 4 * LANES)),
        jax.ShapeDtypeStruct((b, tq, 4 * LANES), BF16),
        _state_scratch(2, 4, tq) + [pltpu.VMEM((2, LANES, npg * psz), BF16)] * 2, "diff_sample", pages=PAGES)


def _fox_sample_kernel(pt_ref, *refs, npg, pos0, n_new):
    pages = refs[:npg]
    q_ref, new_ref, cr_ref, tot_ref, lf_ref, o_ref, m_ref, l_ref, acc_ref, k16, v16 = refs[npg:]
    st = (m_ref, l_ref, acc_ref)
    j = pl.program_id(1)
    tq = q_ref.shape[1]
    psz = pages[0].shape[-1]
    tk = npg * psz
    qpos = pos0 + jnp.minimum(lax.broadcasted_iota(jnp.int32, (tq, 1), 0), n_new - 1)

    @pl.when(j == 0)
    def _():
        _state_init(st)

    lf = lf_ref[0]
    row = lax.broadcasted_iota(jnp.int32, (tq, 1), 0)
    cnew = jnp.zeros_like(lf)
    for t in range(n_new):
        cnew = cnew + jnp.where(row >= t, lf[t:t + 1, :], 0.0)
    psz_l = tot_ref.shape[-1]
    kpos = j * tk + lax.broadcasted_iota(jnp.int32, (1, tk), 1)
    bias = _mask_bias(kpos <= qpos)[None]
    qs = []
    for kb in range(2):
        for k in range(npg):
            k16[kb, :, k * psz:(k + 1) * psz] = pages[k][0, 0, kb * LANES:(kb + 1) * LANES, :].astype(BF16)
            v16[kb, :, k * psz:(k + 1) * psz] = pages[k][0, 0, (2 + kb) * LANES:(3 + kb) * LANES, :].astype(BF16)
        q = _stack_rows(q_ref.at[:, :, kb * 2 * LANES:(kb + 1) * 2 * LANES], 2, HEAD_DIM ** -0.5)
        qs.append(q)
        cq = jnp.stack([tot_ref[0, 2 * kb + hh:2 * kb + hh + 1, psz_l - 1:psz_l]
                        + cnew[:, 2 * kb + hh:2 * kb + hh + 1] for hh in range(2)])
        ck = cr_ref[0, 2 * kb:2 * kb + 2, :][:, None, :]
        s = _dot(q, k16[kb]).reshape(2, tq, tk) + cq - ck + bias
        _state_update(st, kb, functools.partial(_online_update_fast, s, v16[kb], vt=True))

    @pl.when(j == pl.num_programs(1) - 1)
    def _():
        nrow = new_ref.shape[1]
        r_i = lax.broadcasted_iota(jnp.int32, (1, nrow), 1)
        m_new = ((r_i < n_new) & (pos0 + r_i <= qpos))[None]
        for kb in range(2):
            kn = new_ref[0, :, kb * LANES:(kb + 1) * LANES].astype(BF16)
            vn = new_ref[0, :, (2 + kb) * LANES:(3 + kb) * LANES].astype(BF16)
            bias = []
            for hh in range(2):
                h = 2 * kb + hh
                d = jnp.zeros((tq, nrow), F32)
                for t in range(n_new):
                    d = d + jnp.where((row >= t) & (r_i < t), lf[t:t + 1, h:h + 1], 0.0)
                bias.append(d)
            s = _dot_nt(qs[kb], kn).reshape(2, tq, nrow) + jnp.stack(bias)
            _state_update(st, kb, functools.partial(_online_update, s, m_new, vn))
            o = _online_final((m_ref[kb], l_ref[kb], acc_ref[kb]))
            for hh in range(2):
                hblk = 2 * kb + hh
                o_ref[0, :, hblk * LANES:(hblk + 1) * LANES] = o[hh].astype(o_ref.dtype)


def _fox_sample(pt, layer, pool, fq, new, crow, lf, pos0, n_new):
    b, n_pages = pt.shape
    psz = pool.shape[-1]
    npg = _pages_per_step(n_pages, PAGES)
    tq = fq.shape[1]
    past = crow.shape[-1]
    full = lambda shp: pl.BlockSpec((1,) + shp, lambda i, j, pt: (i,) + (0,) * len(shp))
    return _paged_call(
        functools.partial(_fox_sample_kernel, pos0=pos0, n_new=n_new), pt, layer, [pool], (512, psz),
        [fq, new, crow, crow, lf],
        [full((tq, 4 * LANES)), full((tq, 512)),
         pl.BlockSpec((1, 8, npg * psz), lambda i, j, pt: (i, 0, j)),
         pl.BlockSpec((1, 8, LANES), lambda i, j, pt: (i, 0, past // LANES - 1)),
         pl.BlockSpec((1, tq, LANES), lambda i, j, pt: (i, 0, 2))],
        full((tq, 4 * LANES)),
        jax.ShapeDtypeStruct((b, tq, 4 * LANES), BF16),
        _state_scratch(2, 2, tq) + [pltpu.VMEM((2, LANES, npg * psz), BF16)] * 2, "fox_sample", pages=PAGES)


def _wo_rows():
    import numpy as np
    rows = []
    for h in range(NSA_HEADS):
        r = np.full(LANES, -1, np.int64)
        g = h // NSA_GROUP
        r[64 * g:64 * g + 64] = 64 * h + np.arange(64)
        rows.append(r)
    for base, nh in ((NSA_HEADS * HEAD_DIM, DIFF_HEADS), ((NSA_HEADS + DIFF_HEADS) * HEAD_DIM, FOX_HEADS)):
        for h in range(nh):
            r = np.full(LANES, -1, np.int64)
            r[64 * (h % 2):64 * (h % 2) + 64] = base + 64 * h + np.arange(64)
            rows.append(r)
    return np.concatenate(rows)


def _extend_wo(w):
    import numpy as np
    idx = _wo_rows()
    return jnp.where(jnp.asarray(idx >= 0)[:, None], w[jnp.asarray(np.maximum(idx, 0))], 0.0).astype(BF16)


def _mem_and_peer(x2, bsz, kv, lw):
    n, d = x2.shape
    q = _linear(x2, lw["w_mq"], gain=lw["g_mem"])
    t = n // bsz
    if t < TP:
        qp = jnp.pad(q.reshape(bsz, t, d), ((0, 0), (0, TP - t), (0, 0)))
        om = _mem_attn(qp, kv)[:, :t]
    else:
        om = _mem_attn(q.reshape(bsz, t, d), kv)
    x2 = _linear(om.reshape(n, d), lw["w_mo"], res=x2)
    return _peer(x2, lw["g_ffn"], lw["peer_wq"], lw["peer_keys"], lw["peer_u"], lw["peer_vt"])


def _prompt_layer(x2, bsz, mem_prompt, lw, tabs):
    n, d = x2.shape
    t = n // bsz
    outs = _inproj(x2, lw["g_attn"], lw["w_ext"], lw["b_ext"], tabs[0], tabs[1], 256)
    o = {name: v.reshape(bsz, t, -1) for (name, _, _), v in zip(_SEGS, outs)}
    f, s, cb = _compress(o["cmp"].reshape(bsz, t // CMP_STRIDE, CMP_STRIDE * 256), *lw["cmp_w"])
    o_nsa = _nsa_prompt(o["qn"], o["qr"], f, s, cb, o["sel"], o["win"], o["gate"])
    o_diff = _diff_prompt(o["dq"], o["diff"], lw["diff_lambda"], lw["diff_gain"], lw["lam_init"])
    ccol, crow = _cumsum(o["gate"])
    o_fox = _fox_prompt(o["fq"], o["fox"], ccol, crow)
    mixed = jnp.concatenate([o_nsa, o_diff, o_fox], axis=-1).reshape(n, -1)
    x2 = _linear(mixed, lw["w_o_ext"], res=x2)
    m = mem_prompt.shape[1]
    mkv = _linear(mem_prompt.reshape(bsz * m, d), lw["w_mkv"])
    x2 = _mem_and_peer(x2, bsz, mkv.reshape(bsz, m, 2 * d), lw)
    return x2, o, mkv, cb


def _sample_layer(x2, bsz, pt, layer, pools, wstate, mem_kv, lw, tabs, cb, pos0):
    n, d = x2.shape
    t = n // bsz
    outs = _inproj(x2, lw["g_attn"], lw["w_ext"], lw["b_ext"], tabs[0], tabs[1], n)
    o = {name: v.reshape(bsz, t, -1) for (name, _, _), v in zip(_SEGS, outs)}
    op = {name: jnp.pad(v, ((0, 0), (0, TP - t), (0, 0))) for name, v in o.items()}
    cmp_pool, sel_pool, diff_pool, fox_pool, logf_pool_t = pools
    f, s = _compress_paged(pt, layer, cmp_pool, lw["cmp_w"][0], lw["cmp_w"][1])
    nseg = f.shape[1]
    oc, ps = _nsa_sample_cmp(op["qn"], f, s, cb, pos0, t)
    nb = -(-(pos0 + t) // SEL_BLOCK)
    sel = _select_rows(ps.reshape(bsz * NSA_KV_HEADS * TP, nseg), nb, pos0, TP, t)
    sel = sel.reshape(bsz, NSA_KV_HEADS, TP, -1)
    o_nsa = _nsa_sample(pt, layer, sel_pool, op["qr"], sel, op["sel"], wstate, op["win"], oc, op["gate"], pos0, t)
    o_diff = _diff_sample(pt, layer, diff_pool, op["dq"], op["diff"], lw["diff_lambda"], lw["diff_gain"],
                          lw["lam_init"], pos0, t)
    crow = _cumsum_paged(pt, layer, logf_pool_t)
    o_fox = _fox_sample(pt, layer, fox_pool, op["fq"], op["fox"], crow, op["gate"], pos0, t)
    mixed = jnp.concatenate([o_nsa, o_diff, o_fox], axis=-1)[:, :t].reshape(n, -1)
    x2 = _linear(mixed, lw["w_o_ext"], res=x2)
    x2 = _mem_and_peer(x2, bsz, mem_kv, lw)
    return x2, o


def kernel(x_prompt, x_sample, cache_nsa_cmp_kv, cache_nsa_sel_kv, cache_diff_kv, cache_fox_kv, cache_fox_logf, state_nsa_win_kv, cache_mem_kv, page_table, mem_prompt, g_attn, w_in, b_in, nsa_pe_k, nsa_pe_v, nsa_w_ck, nsa_w_cv, diff_lambda, diff_gain, w_o, g_mem, w_mq, w_mkv, w_mo, g_ffn, peer_wq, peer_keys, peer_u, peer_v, g_final):
    bp, tp, d = x_prompt.shape
    bs, ts, _ = x_sample.shape
    depth = w_in.shape[0]
    n_pool, psz = cache_nsa_cmp_kv.shape[1:3]
    pos0 = page_table.shape[1] * psz
    assert pos0 % CMP_STRIDE == 0 and ts < CMP_STRIDE and ts <= TP
    pos_p = jnp.arange(tp, dtype=jnp.int32)
    pos_s = pos0 + (jnp.arange(bs * ts, dtype=jnp.int32) % ts)
    tabs_p = (_rope_tables(pos_p, HEAD_DIM, ROT_DIM), _rope_tables(pos_p, DIFF_HALF, DIFF_ROT))
    tabs_s = (_rope_tables(pos_s, HEAD_DIM, ROT_DIM), _rope_tables(pos_s, DIFF_HALF, DIFF_ROT))
    xp = x_prompt.reshape(bp * tp, d)
    xs = x_sample.reshape(bs * ts, d)
    names = ("cmp", "sel", "diff", "fox")
    rows_p = {k: [] for k in names + ("logf", "win", "mem")}
    rows_s = {k: [] for k in names + ("logf", "win")}
    pools = tuple(_feature_major(c) for c in
                  (cache_nsa_cmp_kv, cache_nsa_sel_kv, cache_diff_kv, cache_fox_kv, cache_fox_logf))
    wstate_t = _feature_major(state_nsa_win_kv)
    for l in range(depth):
        w_ext, b_ext = _extend_inproj(w_in[l], b_in[l])
        lw = dict(
            g_attn=g_attn[l], w_ext=w_ext, b_ext=b_ext,
            cmp_w=_compress_weights(nsa_w_ck[l], nsa_w_cv[l], nsa_pe_k[l], nsa_pe_v[l]),
            diff_lambda=diff_lambda[l], diff_gain=diff_gain[l], lam_init=0.8 - 0.6 * math.exp(-0.3 * l),
            w_o_ext=_extend_wo(w_o[l]), g_mem=g_mem[l], w_mq=w_mq[l], w_mkv=w_mkv[l], w_mo=w_mo[l],
            g_ffn=g_ffn[l], peer_wq=peer_wq[l], peer_keys=peer_keys[l],
            peer_u=peer_u[l].astype(BF16), peer_vt=peer_v[l].T.astype(BF16))
        xp, o, mkv, cb = _prompt_layer(xp, bp, mem_prompt, lw, tabs_p)
        wstate = state_nsa_win_kv[l].reshape(bs, -1, 256)
        xs, os_ = _sample_layer(xs, bs, page_table, l, pools, wstate_t, cache_mem_kv[l].reshape(bs, -1, 2 * d), lw,
                                tabs_s, cb, pos0)
        for k in names:
            rows_p[k].append(o[k])
            rows_s[k].append(os_[k])
        rows_p["logf"].append(o["gate"][..., 2 * LANES:2 * LANES + FOX_HEADS])
        rows_s["logf"].append(os_["gate"][..., 2 * LANES:2 * LANES + FOX_HEADS])
        rows_p["win"].append(o["win"][:, -min(WINDOW, tp):])
        win_all = jnp.concatenate([wstate, os_["win"]], axis=1)
        rows_s["win"].append(win_all[:, -min(WINDOW, win_all.shape[1]):])
        rows_p["mem"].append(mkv)
    y_p = _final_norm(xp, g_final).reshape(bp, tp, d)
    y_s = _final_norm(xs, g_final).reshape(bs, ts, d)

    def st(lst, tail):
        a = jnp.stack(lst, axis=0)
        return a.reshape(a.shape[:3] + tail)

    kv2 = (2, NSA_KV_HEADS, HEAD_DIM)
    kv4 = (2, DIFF_HEADS, HEAD_DIM)
    return (y_p, y_s,
            st(rows_p["cmp"], kv2), st(rows_s["cmp"], kv2), st(rows_p["sel"], kv2), st(rows_s["sel"], kv2),
            st(rows_p["diff"], kv4), st(rows_s["diff"], kv4), st(rows_p["fox"], kv4), st(rows_s["fox"], kv4),
            st(rows_p["logf"], (FOX_HEADS,)), st(rows_s["logf"], (FOX_HEADS,)),
            st(rows_p["win"], kv2), st(rows_s["win"], kv2),
            jnp.stack(rows_p["mem"], 0).reshape(depth, bp, -1, 2, MEM_HEADS, d // MEM_HEADS))
```

```python
import functools
import math

import jax
import jax.numpy as jnp
from jax import lax
from jax.experimental import pallas as pl
from jax.experimental.pallas import tpu as pltpu

F32 = jnp.float32
BF16 = jnp.bfloat16

HEAD_DIM = 64
ROT_DIM = HEAD_DIM // 4
ROPE_THETA = 500000.0
NSA_HEADS = 8
NSA_KV_HEADS = 2
NSA_GROUP = NSA_HEADS // NSA_KV_HEADS
CMP_STRIDE = 16
CMP_BLOCK = 32
SEL_BLOCK = 64
N_SEL = 16
WINDOW = 512
DIFF_HEADS = 4
DIFF_HALF = HEAD_DIM // 2
DIFF_ROT = DIFF_HALF // 4
FOX_HEADS = 4
MEM_HEADS = 4
PEER_HEADS = 8
PEER_KEYS = 128
PEER_TOPK = 16
PEER_HALF = 128
EPS = 1e-6
NEG_INF = -1e30
FORCE_SCORE = 1e4
LANES = 128
VMEM_LIMIT = 56 * 1024 * 1024


def _params(sem, vmem=VMEM_LIMIT):
    return pltpu.CompilerParams(dimension_semantics=sem, vmem_limit_bytes=vmem)


def _dot(a, b):
    return jnp.dot(a, b, preferred_element_type=F32)


def _dot_nt(a, b):
    return lax.dot_general(a, b, (((1,), (1,)), ((), ())), preferred_element_type=F32)


def _rms(x, g):
    return x * lax.rsqrt(jnp.mean(x * x, axis=-1, keepdims=True) + EPS) * g


def _linear_kernel(*refs, has_gain, has_bias, has_res):
    it = iter(refs)
    x_ref = next(it)
    g_ref = next(it) if has_gain else None
    w_ref = next(it)
    b_ref = next(it) if has_bias else None
    r_ref = next(it) if has_res else None
    o_ref = next(it)
    xb_ref = next(it)

    @pl.when(pl.program_id(1) == 0)
    def _():
        x = x_ref[...].astype(F32)
        if has_gain:
            x = _rms(x, g_ref[...])
        xb_ref[...] = x.astype(BF16)

    y = _dot(xb_ref[...], w_ref[...])
    if has_bias:
        y = y + b_ref[...]
    if has_res:
        y = y + r_ref[...]
    o_ref[...] = y.astype(o_ref.dtype)


def _linear(x, w, gain=None, bias=None, res=None, tm=512, tn=512, out_dtype=F32):
    m, k = x.shape
    n = w.shape[1]
    tm = min(tm, m)
    tn = min(tn, n)
    assert m % tm == 0 and n % tn == 0
    args = [x]
    specs = [pl.BlockSpec((tm, k), lambda i, j: (i, 0))]
    if gain is not None:
        args.append(gain.reshape(1, k))
        specs.append(pl.BlockSpec((1, k), lambda i, j: (0, 0)))
    args.append(w.astype(BF16))
    specs.append(pl.BlockSpec((k, tn), lambda i, j: (0, j)))
    if bias is not None:
        args.append(bias.reshape(1, n))
        specs.append(pl.BlockSpec((1, tn), lambda i, j: (0, j)))
    if res is not None:
        args.append(res)
        specs.append(pl.BlockSpec((tm, tn), lambda i, j: (i, j)))
    return pl.pallas_call(
        functools.partial(_linear_kernel, has_gain=gain is not None, has_bias=bias is not None,
                          has_res=res is not None),
        grid=(m // tm, n // tn),
        in_specs=specs,
        out_specs=pl.BlockSpec((tm, tn), lambda i, j: (i, j)),
        out_shape=jax.ShapeDtypeStruct((m, n), out_dtype),
        scratch_shapes=[pltpu.VMEM((tm, k), BF16)],
        compiler_params=_params(("parallel", "arbitrary")),
        name="linear",
    )(*args)


def _linear_parts_kernel(*refs, widths):
    xs = refs[:len(widths)]
    w_ref, r_ref, o_ref = refs[len(widths):]
    y = r_ref[...]
    k0 = 0
    for x_ref, wd in zip(xs, widths):
        y = y + _dot(x_ref[...], w_ref[k0:k0 + wd, :])
        k0 += wd
    o_ref[...] = y


def _linear_parts(parts, w, res, tm=512, tn=512):
    m = parts[0].shape[0]
    widths = tuple(p.shape[1] for p in parts)
    k, n = w.shape
    assert sum(widths) == k and m % tm == 0 and n % tn == 0
    return pl.pallas_call(
        functools.partial(_linear_parts_kernel, widths=widths),
        grid=(m // tm, n // tn),
        in_specs=[pl.BlockSpec((tm, wd), lambda i, j: (i, 0)) for wd in widths]
        + [pl.BlockSpec((k, tn), lambda i, j: (0, j)), pl.BlockSpec((tm, tn), lambda i, j: (i, j))],
        out_specs=pl.BlockSpec((tm, tn), lambda i, j: (i, j)),
        out_shape=jax.ShapeDtypeStruct((m, n), F32),
        compiler_params=_params(("parallel", "arbitrary")),
        name="linear_parts",
    )(*parts, w, res)


def _peer_scores_kernel(x_ref, g_ref, wq_ref, keys_ref, xn_ref, st_ref):
    hb = _rms(x_ref[...], g_ref[...]).astype(BF16)
    xn_ref[...] = hb
    qb = _dot(hb, wq_ref[...]).astype(BF16)
    for hc in range(2 * PEER_HEADS):
        st_ref[hc] = _dot_nt(keys_ref[hc], qb[:, hc * PEER_HALF:(hc + 1) * PEER_HALF])


def _peer_scores(x, g, wq, keys):
    n, d = x.shape
    tm = min(256, n)
    nq = wq.shape[1]
    hc = 2 * PEER_HEADS
    return pl.pallas_call(
        _peer_scores_kernel,
        grid=(n // tm,),
        in_specs=[pl.BlockSpec((tm, d), lambda i: (i, 0)),
                  pl.BlockSpec((1, d), lambda i: (0, 0)),
                  pl.BlockSpec((d, nq), lambda i: (0, 0)),
                  pl.BlockSpec((hc, PEER_KEYS, PEER_HALF), lambda i: (0, 0, 0))],
        out_specs=[pl.BlockSpec((tm, d), lambda i: (i, 0)),
                   pl.BlockSpec((hc, PEER_KEYS, tm), lambda i: (0, 0, i))],
        out_shape=[jax.ShapeDtypeStruct((n, d), BF16),
                   jax.ShapeDtypeStruct((hc, PEER_KEYS, n), F32)],
        compiler_params=_params(("parallel",)),
        name="peer_scores",
    )(x, g.reshape(1, d), wq.astype(BF16), keys.reshape(hc, PEER_KEYS, PEER_HALF).astype(BF16))


_PEER_PAIRS = [(a, b) for a in range(PEER_TOPK) for b in range(PEER_TOPK) if (a + 1) * (b + 1) <= PEER_TOPK]


def _peer_topk_kernel(st_ref, t_ref, r_ref, a_ref, c_ref, rk_ref, sv_ref, av_ref, cn_ref):
    tn = st_ref.shape[-1]
    row = lax.broadcasted_iota(jnp.int32, (PEER_KEYS, tn), 0).astype(F32)
    big = float(PEER_KEYS)

    rk_ref[...] = jnp.full(rk_ref.shape, big, F32)
    nch = 2
    for h0 in range(0, 2 * PEER_HEADS, nch):

        def extract(t, carry, h0=h0):
            out = []
            for k, s in enumerate(carry):
                h, c = divmod(h0 + k, 2)
                m = jnp.max(s, axis=0, keepdims=True)
                idx = jnp.min(jnp.where(s == m, row, big), axis=0, keepdims=True)
                hit = row == idx
                sv_ref[c, t, h:h + 1, :] = m
                rk_ref[h0 + k] = jnp.where(hit, jnp.asarray(t).astype(F32), rk_ref[h0 + k])
                out.append(jnp.where(hit, -jnp.inf, s))
            return tuple(out)

        lax.fori_loop(0, PEER_TOPK, extract, tuple(st_ref[h0 + k] for k in range(nch)))

    cand = [sv_ref[0, a] + sv_ref[1, b] for a, b in _PEER_PAIRS]
    top = sv_ref[0, 0] + sv_ref[1, 0]
    sel = []
    for ia, (a, b) in enumerate(_PEER_PAIRS):
        cnt = jnp.zeros_like(top)
        for ib, (a2, b2) in enumerate(_PEER_PAIRS):
            if ib == ia:
                continue
            ahead = (cand[ib] >= cand[ia]) if (a2 * PEER_TOPK + b2) < (a * PEER_TOPK + b) else (cand[ib] > cand[ia])
            cnt = cnt + jnp.where(ahead, 1.0, 0.0)
        sel.append(jnp.where(cnt < float(PEER_TOPK), 1.0, 0.0))
    z = jnp.zeros_like(top)
    counts = [jnp.zeros_like(top) for _ in range(PEER_TOPK)]
    for ia, (a, b) in enumerate(_PEER_PAIRS):
        z = z + sel[ia] * jnp.exp(cand[ia] - top)
        counts[a] = counts[a] + sel[ia]
    for a in range(PEER_TOPK):
        av_ref[a] = jnp.exp(sv_ref[0, a] - sv_ref[0, 0]) / z
        cn_ref[a] = counts[a]

    for h in range(PEER_HEADS):
        rank1 = rk_ref[2 * h]
        rank2 = rk_ref[2 * h + 1]

        def scatter(a, carry, h=h, rank1=rank1):
            wa, ca = carry
            hit = rank1 == jnp.asarray(a).astype(F32)
            wa = jnp.where(hit, av_ref[a, h:h + 1, :], wa)
            ca = jnp.where(hit, cn_ref[a, h:h + 1, :], ca)
            return wa, ca

        zero = jnp.zeros((PEER_KEYS, tn), F32)
        wa, ca = lax.fori_loop(0, PEER_TOPK, scatter, (zero, zero))
        a_ref[h] = wa.astype(a_ref.dtype)
        c_ref[h] = ca.astype(c_ref.dtype)
        r_ref[h] = rank2.astype(r_ref.dtype)
        t_ref[h] = jnp.where(rank2 < float(PEER_TOPK), jnp.exp(st_ref[2 * h + 1] - sv_ref[1, 0, h:h + 1, :]),
                             0.0).astype(t_ref.dtype)


def _peer_topk(st):
    hc, nk, n = st.shape
    tn = LANES
    spec_h = pl.BlockSpec((PEER_HEADS, nk, tn), lambda i: (0, 0, i))
    shape_h = jax.ShapeDtypeStruct((PEER_HEADS, nk, n), F32)
    shape_b = jax.ShapeDtypeStruct((PEER_HEADS, nk, n), BF16)
    return pl.pallas_call(
        _peer_topk_kernel,
        grid=(n // tn,),
        in_specs=[pl.BlockSpec((hc, nk, tn), lambda i: (0, 0, i))],
        out_specs=[spec_h] * 4,
        out_shape=[shape_b, shape_b, shape_h, shape_h],
        scratch_shapes=[pltpu.VMEM((hc, nk, tn), F32),
                        pltpu.VMEM((2, PEER_TOPK, PEER_HEADS, tn), F32),
                        pltpu.VMEM((PEER_TOPK, PEER_HEADS, tn), F32),
                        pltpu.VMEM((PEER_TOPK, PEER_HEADS, tn), F32)],
        compiler_params=_params(("parallel",)),
        name="peer_topk",
    )(st)


def _gelu(a):
    return 0.5 * a * (1.0 + lax.erf(a * math.sqrt(0.5)))


def _peer_dense_kernel(xn_ref, u_ref, vt_ref, t_ref, r_ref, a_ref, c_ref, res_ref, o_ref, acc_ref, h_ref):
    e = pl.program_id(1)
    te = u_ref.shape[0]
    nib = te // PEER_KEYS
    tm = xn_ref.shape[0]
    sub = 16

    @pl.when(e == 0)
    def _():
        acc_ref[...] = jnp.zeros_like(acc_ref)

    at = _dot_nt(u_ref[...], xn_ref[...])
    for ib in range(nib):
        i = e * nib + ib
        g = jnp.zeros((PEER_KEYS // sub, sub, tm), BF16)
        for h in range(PEER_HEADS):
            cnt = jnp.broadcast_to(c_ref[h, pl.ds(i, 1), :], (sub, tm)).astype(BF16)[None]
            wa = jnp.broadcast_to(a_ref[h, pl.ds(i, 1), :], (sub, tm)).astype(BF16)[None]
            r3 = r_ref[h].reshape(PEER_KEYS // sub, sub, tm)
            t3 = t_ref[h].reshape(PEER_KEYS // sub, sub, tm)
            g = g + jnp.where(r3 < cnt, t3 * wa, 0.0)
        rows = slice(ib * PEER_KEYS, (ib + 1) * PEER_KEYS)
        h_ref[rows, :] = g.reshape(PEER_KEYS, tm) * _gelu(at[rows, :]).astype(BF16)
    acc_ref[...] += _dot(vt_ref[...], h_ref[...])

    @pl.when(e == pl.num_programs(1) - 1)
    def _():
        o_ref[...] = res_ref[...] + acc_ref[...].T


def _peer_dense(xn, u, vt, t, r, a, c, res, te=2048):
    n, d = xn.shape
    ne = u.shape[0]
    tm = min(512, n)
    spec_h = pl.BlockSpec((PEER_HEADS, PEER_KEYS, tm), lambda i, e: (0, 0, i))
    return pl.pallas_call(
        _peer_dense_kernel,
        grid=(n // tm, ne // te),
        in_specs=[pl.BlockSpec((tm, d), lambda i, e: (i, 0)),
                  pl.BlockSpec((te, d), lambda i, e: (e, 0)),
                  pl.BlockSpec((d, te), lambda i, e: (0, e)),
                  spec_h, spec_h, spec_h, spec_h,
                  pl.BlockSpec((tm, d), lambda i, e: (i, 0))],
        out_specs=pl.BlockSpec((tm, d), lambda i, e: (i, 0)),
        out_shape=jax.ShapeDtypeStruct((n, d), F32),
        scratch_shapes=[pltpu.VMEM((d, tm), F32), pltpu.VMEM((te, tm), BF16)],
        compiler_params=_params(("parallel", "arbitrary")),
        name="peer_dense",
    )(xn, u, vt, t, r, a, c, res)


def _peer(x, g, wq, keys, u_b, vt_b):
    xn, st = _peer_scores(x, g, wq, keys)
    t, r, a, c = _peer_topk(st)
    return _peer_dense(xn, u_b, vt_b, t, r, a, c, x)


_SEGS = (
    ("qn", 1024, "p" * 8),
    ("qr", 1024, "a" * 8),
    ("cmp", 256, "pp"),
    ("sel", 256, "ap"),
    ("win", 256, "ap"),
    ("dq", 1024, "b" * 8),
    ("diff", 512, "bbpp"),
    ("fq", 512, "pppp"),
    ("fox", 512, "pppp"),
    ("gate", 384, "ssl"),
)
_OFF = dict(nq=0, kc=512, vc=640, ks=768, vs=896, kw=1024, vw=1152, ng=1280, dq=1304, dk=1560, dv=1816,
            fq=2072, fk=2328, fv=2584, ff=2840)


def _inproj_columns():
    import numpy as np
    cols = []

    def blocks(n):
        return [np.full(LANES, -1, np.int64) for _ in range(n)]

    for base in ("nq", "nq"):
        bl = blocks(NSA_HEADS)
        for h in range(NSA_HEADS):
            g = h // NSA_GROUP
            bl[h][64 * g:64 * g + 64] = _OFF[base] + 64 * h + np.arange(64)
        cols += bl
    cols.append(np.arange(_OFF["kc"], _OFF["kc"] + 256))
    cols.append(np.arange(_OFF["ks"], _OFF["ks"] + 256))
    cols.append(np.arange(_OFF["kw"], _OFF["kw"] + 256))
    bl = blocks(2 * DIFF_HEADS)
    for h in range(DIFF_HEADS):
        for c in range(2):
            o = 32 * (2 * (h % 2) + c)
            bl[2 * h + c][o:o + 32] = _OFF["dq"] + 64 * h + 32 * c + np.arange(32)
    cols += bl
    cols.append(np.arange(_OFF["dk"], _OFF["dk"] + 512))
    bl = blocks(FOX_HEADS)
    for h in range(FOX_HEADS):
        o = 64 * (h % 2)
        bl[h][o:o + 64] = _OFF["fq"] + 64 * h + np.arange(64)
    cols += bl
    cols.append(np.arange(_OFF["fk"], _OFF["fk"] + 512))
    bl = blocks(3)
    for g in range(NSA_KV_HEADS):
        bl[g][0:3 * NSA_GROUP] = _OFF["ng"] + 3 * NSA_GROUP * g + np.arange(3 * NSA_GROUP)
    bl[2][0:FOX_HEADS] = _OFF["ff"] + np.arange(FOX_HEADS)
    cols += bl
    return np.concatenate(cols)


def _extend_inproj(w, b):
    import numpy as np
    idx = _inproj_columns()
    keep = jnp.asarray(idx >= 0)
    src = jnp.asarray(np.maximum(idx, 0))
    return (jnp.where(keep[None, :], w[:, src], 0.0).astype(BF16),
            jnp.where(keep, b[src], 0.0).reshape(1, -1))


def _rope_tables(pos, period, rot):
    half = rot // 2
    inv = jnp.power(jnp.float32(ROPE_THETA), -jnp.arange(half, dtype=F32) / half)
    ang = pos.astype(F32)[:, None] * inv[None, :]
    cos, sin = jnp.cos(ang), jnp.sin(ang)
    n = pos.shape[0]
    reps = LANES // period
    pad = jnp.zeros((n, period - rot), F32)
    c = jnp.concatenate([cos, cos, pad + 1.0], axis=1)
    s_up = jnp.concatenate([jnp.zeros((n, half), F32), sin, pad], axis=1)
    s_dn = jnp.concatenate([-sin, jnp.zeros((n, half), F32), pad], axis=1)
    return jnp.stack([jnp.tile(c, (1, reps)), jnp.tile(s_up, (1, reps)), jnp.tile(s_dn, (1, reps))])


def _log_sigmoid(x):
    return -(jnp.maximum(-x, 0.0) + jnp.log1p(jnp.exp(-jnp.abs(x))))


def _inproj_kernel(x_ref, g_ref, w_ref, b_ref, ta_ref, tb_ref, *out_refs):
    hb = _rms(x_ref[...], g_ref[...]).astype(BF16)
    c0 = 0
    for (name, width, kinds), o_ref in zip(_SEGS, out_refs):
        z = _dot(hb, w_ref[:, c0:c0 + width]) + b_ref[:, c0:c0 + width]
        for k, kind in enumerate(kinds):
            zk = z[:, k * LANES:(k + 1) * LANES]
            if kind == "a":
                zk = (zk * ta_ref[0] + pltpu.roll(zk, ROT_DIM // 2, 1) * ta_ref[1]
                      + pltpu.roll(zk, LANES - ROT_DIM // 2, 1) * ta_ref[2])
            elif kind == "b":
                zk = (zk * tb_ref[0] + pltpu.roll(zk, DIFF_ROT // 2, 1) * tb_ref[1]
                      + pltpu.roll(zk, LANES - DIFF_ROT // 2, 1) * tb_ref[2])
            elif kind == "s":
                zk = jax.nn.sigmoid(zk)
            elif kind == "l":
                zk = _log_sigmoid(zk)
            o_ref[:, k * LANES:(k + 1) * LANES] = zk
        c0 += width


def _inproj(x, g, w_ext, b_ext, tab_a, tab_b, tm):
    n, d = x.shape
    npos = tab_a.shape[1]
    tm = min(tm, npos)
    nt = npos // tm
    ctot = w_ext.shape[1]
    return pl.pallas_call(
        _inproj_kernel,
        grid=(n // tm,),
        in_specs=[pl.BlockSpec((tm, d), lambda i: (i, 0)),
                  pl.BlockSpec((1, d), lambda i: (0, 0)),
                  pl.BlockSpec((d, ctot), lambda i: (0, 0)),
                  pl.BlockSpec((1, ctot), lambda i: (0, 0)),
                  pl.BlockSpec((3, tm, LANES), lambda i: (0, i % nt, 0)),
                  pl.BlockSpec((3, tm, LANES), lambda i: (0, i % nt, 0))],
        out_specs=[pl.BlockSpec((tm, wd), lambda i: (i, 0)) for _, wd, _ in _SEGS],
        out_shape=[jax.ShapeDtypeStruct((n, wd), F32) for _, wd, _ in _SEGS],
        compiler_params=_params(("parallel",)),
        name="inproj",
    )(x, g.reshape(1, d), w_ext, b_ext, tab_a, tab_b)


def _softmax_masked(s, mask):
    s = jnp.where(mask, s, NEG_INF)
    e = jnp.where(mask, jnp.exp(s - jnp.max(s, axis=-1, keepdims=True)), 0.0)
    return e / jnp.maximum(jnp.sum(e, axis=-1, keepdims=True), 1e-30)


def _online_update(s, mask, v, carry, vt=False):
    m, l, acc = carry
    r, tq, tk = s.shape
    s = jnp.where(mask, s, NEG_INF)
    m_new = jnp.maximum(m, jnp.max(s, axis=-1, keepdims=True))
    alpha = jnp.exp(m - m_new)
    p = jnp.where(mask, jnp.exp(s - m_new), 0.0)
    l = alpha * l + jnp.sum(p, axis=-1, keepdims=True)
    pb = p.reshape(r * tq, tk).astype(BF16)
    pv = (_dot_nt(pb, v) if vt else _dot(pb, v)).reshape(r, tq, LANES)
    return m_new, l, alpha * acc + pv


def _mask_bias(mask):
    return jnp.where(mask, 0.0, NEG_INF)


def _online_update_fast(s, v, carry, vt=False):
    m, l, acc = carry
    r, tq, tk = s.shape
    m_new = jnp.maximum(m, jnp.max(s, axis=-1, keepdims=True))
    alpha = jnp.exp(m - m_new)
    p = jnp.exp(s - m_new)
    l = alpha * l + jnp.sum(p, axis=-1, keepdims=True)
    pb = p.reshape(r * tq, tk).astype(BF16)
    pv = (_dot_nt(pb, v) if vt else _dot(pb, v)).reshape(r, tq, LANES)
    return m_new, l, alpha * acc + pv


def _online_init(r, tq, dv=LANES):
    return (jnp.full((r, tq, 1), NEG_INF, F32), jnp.zeros((r, tq, 1), F32), jnp.zeros((r, tq, dv), F32))


def _online_final(carry):
    _, l, acc = carry
    return acc / jnp.maximum(l, 1e-30)


def _stack_rows(ref, nblk, scale):
    parts = [ref[0, :, i * LANES:(i + 1) * LANES] for i in range(nblk)]
    return (jnp.concatenate(parts, axis=0) * scale).astype(BF16)


def _split3(hi):
    a = hi.astype(BF16)
    r1 = hi - a.astype(F32)
    b = r1.astype(BF16)
    c = (r1 - b.astype(F32)).astype(BF16)
    return a, b, c


def _compress_weights(w_ck, w_cv, pe_k, pe_v):
    d = HEAD_DIM
    wk = w_ck.reshape(CMP_BLOCK, d, d)
    wv = w_cv.reshape(CMP_BLOCK, d, d)
    wst = jnp.stack([wk, wk, wv, wv])
    eye = jnp.eye(4, dtype=F32)
    big = jnp.einsum("cpde,cf->pcdfe", wst, eye).reshape(CMP_BLOCK, 4 * d, 4 * d)
    w1 = big[:CMP_STRIDE].reshape(CMP_STRIDE * 4 * d, 4 * d).astype(BF16)
    w2 = big[CMP_STRIDE:].reshape(CMP_STRIDE * 4 * d, 4 * d).astype(BF16)
    pst = jnp.stack([pe_k, pe_k, pe_v, pe_v], axis=1)
    pe1 = jnp.broadcast_to(pst[:CMP_STRIDE].reshape(1, -1), (8, CMP_STRIDE * 4 * d)).astype(BF16)
    pe2 = jnp.broadcast_to(pst[CMP_STRIDE:].reshape(1, -1), (8, CMP_STRIDE * 4 * d)).astype(BF16)
    return w1, w2, pe1, pe2


def _compress_kernel(a_ref, w1_ref, w2_ref, pe1_ref, pe2_ref, f_ref, s_ref, b_ref):
    a = a_ref[0].astype(BF16)
    f_ref[0] = _dot(a, w1_ref[...])
    s_ref[0] = _dot(a, w2_ref[...])
    b_ref[...] = _dot(pe1_ref[...], w1_ref[...]) + _dot(pe2_ref[...], w2_ref[...])


def _compress(a, w1, w2, pe1, pe2):
    b, nseg, ka = a.shape
    ts = min(256, nseg)
    wspec = pl.BlockSpec((ka, 256), lambda i, j: (0, 0))
    pspec = pl.BlockSpec((8, ka), lambda i, j: (0, 0))
    ospec = pl.BlockSpec((1, ts, 256), lambda i, j: (i, j, 0))
    return pl.pallas_call(
        _compress_kernel,
        grid=(b, nseg // ts),
        in_specs=[pl.BlockSpec((1, ts, ka), lambda i, j: (i, j, 0)), wspec, wspec, pspec, pspec],
        out_specs=[ospec, ospec, pl.BlockSpec((8, 256), lambda i, j: (0, 0))],
        out_shape=[jax.ShapeDtypeStruct((b, nseg, 256), F32), jax.ShapeDtypeStruct((b, nseg, 256), F32),
                   jax.ShapeDtypeStruct((8, 256), F32)],
        compiler_params=_params(("arbitrary", "arbitrary")),
        name="compress",
    )(a, w1, w2, pe1, pe2)


def _select_blocks(psum, qpos, nb, sc_ref):
    tq, ncmp = psum.shape
    nbp = sc_ref.shape[0]
    n_i = lax.broadcasted_iota(jnp.int32, (ncmp, nbp), 0)
    j_i = lax.broadcasted_iota(jnp.int32, (ncmp, nbp), 1)
    dlt = n_i - (SEL_BLOCK // CMP_STRIDE) * j_i
    wmat = jnp.where((dlt == -1) | (dlt == 3), 1.0, jnp.where((dlt >= 0) & (dlt <= 2), 2.0, 0.0)).astype(BF16)
    p_hi = psum.astype(BF16)
    p_lo = (psum - p_hi.astype(F32)).astype(BF16)
    imp = _dot(p_hi, wmat) + _dot(p_lo, wmat)
    blk = lax.broadcasted_iota(jnp.int32, (tq, nbp), 1)
    cur = qpos // SEL_BLOCK
    forced = (blk == 0) | (blk == cur) | (blk == cur - 1)
    score = jnp.where(forced, FORCE_SCORE, jnp.where(blk <= cur, imp, NEG_INF))
    nbr = -(-nb // 8) * 8
    sc = score.T[0:nbr, :]
    rowf = lax.broadcasted_iota(jnp.int32, (nbr, tq), 0).astype(F32)

    def take_max(_, carry):
        s, sel = carry
        m = jnp.max(s, axis=0, keepdims=True)
        idx = jnp.min(jnp.where(s == m, rowf, float(nbr)), axis=0, keepdims=True)
        hit = rowf == idx
        return jnp.where(hit, -jnp.inf, s), jnp.where(hit, 1.0, sel)

    _, sel = lax.fori_loop(0, min(N_SEL, nb), take_max, (sc, jnp.zeros((nbr, tq), F32)))
    if nbr < nbp:
        sel = jnp.concatenate([sel, jnp.zeros((nbp - nbr, tq), F32)], axis=0)
    return sel.T


def _expand_blocks(sel, k0, tk, as_float=False):
    nbp = sel.shape[1]
    j_i = lax.broadcasted_iota(jnp.int32, (nbp, tk), 0)
    s_i = lax.broadcasted_iota(jnp.int32, (nbp, tk), 1) + k0
    e = jnp.where(lax.shift_right_logical(s_i, 6) == j_i, 1.0, 0.0).astype(BF16)
    keep = _dot(sel.astype(BF16), e)
    return keep if as_float else keep > 0.5


def _cmp_branch(qn, kcmp, vcmp, qpos, nseg, ncmp_valid, pos0=0):
    tq = qpos.shape[0]
    n_i = lax.broadcasted_iota(jnp.int32, (1, nseg), 1)
    m_c = ((n_i * CMP_STRIDE + (CMP_BLOCK - 1) + pos0) <= qpos) & (n_i < ncmp_valid)
    s_c = _dot_nt(qn, kcmp).reshape(NSA_GROUP, tq, nseg)
    p_c = _softmax_masked(s_c, m_c[None])
    o_c = _dot(p_c.reshape(NSA_GROUP * tq, nseg).astype(BF16), vcmp).reshape(NSA_GROUP, tq, LANES)
    return o_c, jnp.sum(p_c, axis=0)


def _nsa_prompt_kernel(qn_ref, qr_ref, f_ref, s_ref, cb_ref, sel_ref, win_ref, gate_ref, o_ref,
                       ks16, vs16, kw16, vw16, sc_ref, *, tq, tk, nseg, nb):
    qi = pl.program_id(1)
    q0 = qi * tq

    @pl.when(qi == 0)
    def _():
        ks16[...] = sel_ref[0, :, 0:LANES].astype(BF16)
        vs16[...] = sel_ref[0, :, LANES:2 * LANES].astype(BF16)
        kw16[...] = win_ref[0, :, 0:LANES].astype(BF16)
        vw16[...] = win_ref[0, :, LANES:2 * LANES].astype(BF16)

    scale = HEAD_DIM ** -0.5
    qpos = q0 + lax.broadcasted_iota(jnp.int32, (tq, 1), 0)
    kv = f_ref[0] + pltpu.roll(s_ref[0], nseg - 1, 0) + cb_ref[0:1, :]
    kcmp = kv[:, 0:LANES].astype(BF16)
    vcmp = kv[:, LANES:2 * LANES].astype(BF16)
    wspan = WINDOW + tq
    kstart = pl.multiple_of(jnp.maximum(q0 - WINDOW, 0), tq)
    nkt = (q0 + tq + tk - 1) // tk

    for g in range(NSA_KV_HEADS):
        qn = _stack_rows(qn_ref.at[:, :, g * NSA_GROUP * LANES:(g + 1) * NSA_GROUP * LANES], NSA_GROUP, scale)
        o_c, psum = _cmp_branch(qn, kcmp, vcmp, qpos, nseg, nseg - 1)
        sel = _select_blocks(psum, qpos, nb, sc_ref)
        qr = _stack_rows(qr_ref.at[:, :, g * NSA_GROUP * LANES:(g + 1) * NSA_GROUP * LANES], NSA_GROUP, scale)

        def body(kt, carry, qr=qr, sel=sel):
            k0 = pl.multiple_of(kt * tk, tk)
            s = _dot_nt(qr, ks16[pl.ds(k0, tk), :]).reshape(NSA_GROUP, tq, tk)
            kpos = k0 + lax.broadcasted_iota(jnp.int32, (1, tk), 1)
            bias = jnp.where(_expand_blocks(sel, k0, tk), _mask_bias(kpos <= qpos), NEG_INF)
            return _online_update_fast(s + bias[None], vs16[pl.ds(k0, tk), :], carry)

        o_s = _online_final(lax.fori_loop(0, nkt, body, _online_init(NSA_GROUP, tq)))

        kpos = kstart + lax.broadcasted_iota(jnp.int32, (1, wspan), 1)
        m_w = (kpos <= qpos) & (kpos > qpos - WINDOW)
        s_w = _dot_nt(qr, kw16[pl.ds(kstart, wspan), :]).reshape(NSA_GROUP, tq, wspan)
        p_w = _softmax_masked(s_w, m_w[None])
        o_w = _dot(p_w.reshape(NSA_GROUP * tq, wspan).astype(BF16), vw16[pl.ds(kstart, wspan), :])
        o_w = o_w.reshape(NSA_GROUP, tq, LANES)

        gates = gate_ref[0, :, g * LANES:(g + 1) * LANES]
        for i in range(NSA_GROUP):
            o = (gates[:, 3 * i:3 * i + 1] * o_c[i] + gates[:, 3 * i + 1:3 * i + 2] * o_s[i]
                 + gates[:, 3 * i + 2:3 * i + 3] * o_w[i])
            hblk = g * NSA_GROUP + i
            o_ref[0, :, hblk * LANES:(hblk + 1) * LANES] = o.astype(o_ref.dtype)


def _nsa_prompt(qn, qr, f, s, cb, sel, win, gate, tq=256, tk=512):
    b, t, _ = qn.shape
    nseg = f.shape[1]
    nb = -(-t // SEL_BLOCK)
    nbp = -(-nb // LANES) * LANES
    tk = min(tk, t)
    assert t % tk == 0 and t % tq == 0 and t >= WINDOW + tq
    qspec = pl.BlockSpec((1, tq, NSA_HEADS * LANES), lambda i, j: (i, j, 0))
    fspec = pl.BlockSpec((1, nseg, 256), lambda i, j: (i, 0, 0))
    kspec = pl.BlockSpec((1, t, 256), lambda i, j: (i, 0, 0))
    return pl.pallas_call(
        functools.partial(_nsa_prompt_kernel, tq=tq, tk=tk, nseg=nseg, nb=nb),
        grid=(b, t // tq),
        in_specs=[qspec, qspec, fspec, fspec, pl.BlockSpec((8, 256), lambda i, j: (0, 0)), kspec, kspec,
                  pl.BlockSpec((1, tq, 3 * LANES), lambda i, j: (i, j, 0))],
        out_specs=qspec,
        out_shape=jax.ShapeDtypeStruct((b, t, NSA_HEADS * LANES), BF16),
        scratch_shapes=[pltpu.VMEM((t, LANES), BF16)] * 4 + [pltpu.VMEM((nbp, tq), F32)],
        compiler_params=_params(("arbitrary", "arbitrary")),
        name="nsa_prompt",
    )(qn, qr, f, s, cb, sel, win, gate)


def _cumsum_kernel(lf_ref, col_ref, row_ref, carry_ref, *, tm):
    j = pl.program_id(1)

    @pl.when(j == 0)
    def _():
        carry_ref[...] = jnp.zeros_like(carry_ref)

    r_i = lax.broadcasted_iota(jnp.int32, (tm, tm), 0)
    c_i = lax.broadcasted_iota(jnp.int32, (tm, tm), 1)
    tri = jnp.where(c_i <= r_i, 1.0, 0.0).astype(BF16)
    a, b, c = _split3(lf_ref[0])
    cs = _dot(tri, a) + _dot(tri, b) + _dot(tri, c) + carry_ref[0:1, :]
    carry_ref[...] = jnp.broadcast_to(cs[tm - 1:tm, :], carry_ref.shape)
    sh = pltpu.roll(cs, LANES - 2, 1)
    col_ref[0, :, 0:LANES] = cs
    col_ref[0, :, LANES:2 * LANES] = sh
    row_ref[0, 0:8, :] = cs.T[0:8, :]
    row_ref[0, 8:16, :] = sh.T[0:8, :]


def _cumsum(gate, tm=512):
    b, t, _ = gate.shape
    tm = min(tm, t)
    return pl.pallas_call(
        functools.partial(_cumsum_kernel, tm=tm),
        grid=(b, t // tm),
        in_specs=[pl.BlockSpec((1, tm, LANES), lambda i, j: (i, j, 2))],
        out_specs=[pl.BlockSpec((1, tm, 2 * LANES), lambda i, j: (i, j, 0)),
                   pl.BlockSpec((1, 16, tm), lambda i, j: (i, 0, j))],
        out_shape=[jax.ShapeDtypeStruct((b, t, 2 * LANES), F32), jax.ShapeDtypeStruct((b, 16, t), F32)],
        scratch_shapes=[pltpu.VMEM((8, LANES), F32)],
        compiler_params=_params(("arbitrary", "arbitrary")),
        name="cumsum",
    )(gate)


def _diff_lambda(dl_ref, lam_init):
    dl = dl_ref[...]
    a = jnp.sum(dl[0:1] * dl[1:2], axis=-1, keepdims=True)
    b = jnp.sum(dl[2:3] * dl[3:4], axis=-1, keepdims=True)
    return jnp.exp(a) - jnp.exp(b) + lam_init


def _diff_finish(o, lam, gain_ref, lam_init, o_ref):
    lane = lax.broadcasted_iota(jnp.int32, (1, LANES), 1)
    for hh in range(2):
        w = o[2 * hh] - lam * o[2 * hh + 1]
        keep = jnp.where((lane >= hh * HEAD_DIM) & (lane < (hh + 1) * HEAD_DIM), 1.0, 0.0)
        w = w * keep
        ms = jnp.sum(w * w, axis=-1, keepdims=True) * (1.0 / HEAD_DIM)
        y = w * lax.rsqrt(ms + EPS) * gain_ref[...] * (1.0 - lam_init)
        o_ref[0, :, hh * LANES:(hh + 1) * LANES] = y.astype(o_ref.dtype)


def _diff_prompt_kernel(q_ref, k_ref, v_ref, dl_ref, gain_ref, o_ref, k16, v16, *, tq, tk, lam_init):
    qi = pl.program_id(2)
    q0 = qi * tq

    @pl.when(qi == 0)
    def _():
        k16[...] = k_ref[0].astype(BF16)
        v16[...] = v_ref[0].astype(BF16)

    qpos = q0 + lax.broadcasted_iota(jnp.int32, (tq, 1), 0)
    q = _stack_rows(q_ref, 4, DIFF_HALF ** -0.5)

    def body(kt, carry):
        k0 = pl.multiple_of(kt * tk, tk)
        s = _dot_nt(q, k16[pl.ds(k0, tk), :]).reshape(4, tq, tk)
        kpos = k0 + lax.broadcasted_iota(jnp.int32, (1, tk), 1)
        return _online_update_fast(s + _mask_bias(kpos <= qpos)[None], v16[pl.ds(k0, tk), :], carry)

    o = _online_final(lax.fori_loop(0, (q0 + tq + tk - 1) // tk, body, _online_init(4, tq)))
    _diff_finish(o, _diff_lambda(dl_ref, lam_init), gain_ref, lam_init, o_ref)


def _gain_lanes(gain):
    return jnp.tile(gain.reshape(1, HEAD_DIM), (1, LANES // HEAD_DIM))


def _diff_prompt(dq, rows, dl, gain, lam_init, tq=256, tk=512):
    b, t, _ = dq.shape
    tk = min(tk, t)
    return pl.pallas_call(
        functools.partial(_diff_prompt_kernel, tq=tq, tk=tk, lam_init=lam_init),
        grid=(b, 2, t // tq),
        in_specs=[pl.BlockSpec((1, tq, 4 * LANES), lambda i, kb, j: (i, j, kb)),
                  pl.BlockSpec((1, t, LANES), lambda i, kb, j: (i, 0, kb)),
                  pl.BlockSpec((1, t, LANES), lambda i, kb, j: (i, 0, 2 + kb)),
                  pl.BlockSpec((4, DIFF_HALF), lambda i, kb, j: (0, 0)),
                  pl.BlockSpec((1, LANES), lambda i, kb, j: (0, 0))],
        out_specs=pl.BlockSpec((1, tq, 2 * LANES), lambda i, kb, j: (i, j, kb)),
        out_shape=jax.ShapeDtypeStruct((b, t, 4 * LANES), BF16),
        scratch_shapes=[pltpu.VMEM((t, LANES), BF16)] * 2,
        compiler_params=_params(("arbitrary", "arbitrary", "arbitrary")),
        name="diff_prompt",
    )(dq, rows, rows, dl, _gain_lanes(gain))


def _fox_prompt_kernel(q_ref, k_ref, v_ref, cc_ref, cr_ref, o_ref, k16, v16, *, tq, tk):
    qi = pl.program_id(2)
    q0 = qi * tq

    @pl.when(qi == 0)
    def _():
        k16[...] = k_ref[0].astype(BF16)
        v16[...] = v_ref[0].astype(BF16)

    qpos = q0 + lax.broadcasted_iota(jnp.int32, (tq, 1), 0)
    q = _stack_rows(q_ref, 2, HEAD_DIM ** -0.5)
    cq = jnp.stack([cc_ref[0, :, 0:1], cc_ref[0, :, 1:2]])

    def body(kt, carry):
        k0 = pl.multiple_of(kt * tk, tk)
        ck = cr_ref[0, 0:2, pl.ds(k0, tk)][:, None, :]
        kpos = k0 + lax.broadcasted_iota(jnp.int32, (1, tk), 1)
        s = _dot_nt(q, k16[pl.ds(k0, tk), :]).reshape(2, tq, tk) + cq - ck + _mask_bias(kpos <= qpos)[None]
        return _online_update_fast(s, v16[pl.ds(k0, tk), :], carry)

    o = _online_final(lax.fori_loop(0, (q0 + tq + tk - 1) // tk, body, _online_init(2, tq)))
    for hh in range(2):
        o_ref[0, :, hh * LANES:(hh + 1) * LANES] = o[hh].astype(o_ref.dtype)


def _fox_prompt(fq, rows, ccol, crow, tq=256, tk=512):
    b, t, _ = fq.shape
    tk = min(tk, t)
    tq = min(tq, t)
    return pl.pallas_call(
        functools.partial(_fox_prompt_kernel, tq=tq, tk=tk),
        grid=(b, 2, t // tq),
        in_specs=[pl.BlockSpec((1, tq, 2 * LANES), lambda i, kb, j: (i, j, kb)),
                  pl.BlockSpec((1, t, LANES), lambda i, kb, j: (i, 0, kb)),
                  pl.BlockSpec((1, t, LANES), lambda i, kb, j: (i, 0, 2 + kb)),
                  pl.BlockSpec((1, tq, LANES), lambda i, kb, j: (i, j, kb)),
                  pl.BlockSpec((1, 8, t), lambda i, kb, j: (i, kb, 0))],
        out_specs=pl.BlockSpec((1, tq, 2 * LANES), lambda i, kb, j: (i, j, kb)),
        out_shape=jax.ShapeDtypeStruct((b, t, 4 * LANES), BF16),
        scratch_shapes=[pltpu.VMEM((t, LANES), BF16)] * 2,
        compiler_params=_params(("arbitrary", "arbitrary", "arbitrary")),
        name="fox_prompt",
    )(fq, rows, rows, ccol, crow)


def _mem_attn_kernel(q_ref, kv_ref, o_ref, *, dh):
    kv = kv_ref[0].astype(BF16)
    nh = q_ref.shape[-1] // dh
    for h in range(nh):
        q = (q_ref[0, :, h * dh:(h + 1) * dh] * dh ** -0.5).astype(BF16)
        s = _dot_nt(q, kv[:, h * dh:(h + 1) * dh])
        e = jnp.exp(s - jnp.max(s, axis=-1, keepdims=True))
        p = e / jnp.sum(e, axis=-1, keepdims=True)
        o = _dot(p.astype(BF16), kv[:, (nh + h) * dh:(nh + h + 1) * dh])
        o_ref[0, :, h * dh:(h + 1) * dh] = o.astype(o_ref.dtype)


def _mem_attn(q, kv, tq=512):
    b, t, d = q.shape
    m = kv.shape[1]
    tq = min(tq, t)
    return pl.pallas_call(
        functools.partial(_mem_attn_kernel, dh=d // MEM_HEADS),
        grid=(b, t // tq),
        in_specs=[pl.BlockSpec((1, tq, d), lambda i, j: (i, j, 0)),
                  pl.BlockSpec((1, m, 2 * d), lambda i, j: (i, 0, 0))],
        out_specs=pl.BlockSpec((1, tq, d), lambda i, j: (i, j, 0)),
        out_shape=jax.ShapeDtypeStruct((b, t, d), BF16),
        compiler_params=_params(("parallel", "arbitrary")),
        name="mem_attn",
    )(q, kv)


def _final_norm_kernel(x_ref, g_ref, o_ref):
    o_ref[...] = _rms(x_ref[...], g_ref[...])


def _final_norm(x, g):
    n, d = x.shape
    tm = min(512, n)
    return pl.pallas_call(
        _final_norm_kernel,
        grid=(n // tm,),
        in_specs=[pl.BlockSpec((tm, d), lambda i: (i, 0)), pl.BlockSpec((1, d), lambda i: (0, 0))],
        out_specs=pl.BlockSpec((tm, d), lambda i: (i, 0)),
        out_shape=jax.ShapeDtypeStruct((n, d), F32),
        compiler_params=_params(("parallel",)),
        name="final_norm",
    )(x, g.reshape(1, d))


PAGES = 16
PAGES_CMP = 32
PAGES_LOGF = 64
TP = 8


def _pages_per_step(n_pages, want):
    npg = min(want, n_pages)
    while n_pages % npg:
        npg //= 2
    return npg


def _page_specs(block, npg, layer):
    def imap(i, j, pt, k):
        return (layer, pt[i, j * npg + k]) + (0,) * len(block)
    return [pl.BlockSpec((1, 1) + block, functools.partial(imap, k=k)) for k in range(npg)]


def _feature_major(cache):
    nd = cache.ndim
    t = jnp.transpose(cache, (0, 1) + tuple(range(3, nd)) + (2,))
    return t.reshape(t.shape[0], t.shape[1], -1, t.shape[-1])


def _paged_call(kern, pt, layer, pools, pool_block, others, other_specs, out_specs, out_shape, scratch, name,
                pages=PAGES):
    b, n_pages = pt.shape
    npg = _pages_per_step(n_pages, pages)
    in_specs = []
    args = []
    for pool in pools:
        in_specs += _page_specs(pool_block, npg, layer)
        args += [pool] * npg
    in_specs += other_specs
    args += others
    gs = pltpu.PrefetchScalarGridSpec(num_scalar_prefetch=1, grid=(b, n_pages // npg), in_specs=in_specs,
                                      out_specs=out_specs, scratch_shapes=scratch)
    return pl.pallas_call(functools.partial(kern, npg=npg), grid_spec=gs, out_shape=out_shape,
                          compiler_params=_params(("arbitrary", "arbitrary")), name=name)(pt, *args)


def _cumsum_paged_kernel(pt_ref, *refs, npg):
    pages = refs[:npg]
    row_ref, carry_ref = refs[npg], refs[npg + 1]
    j = pl.program_id(1)

    @pl.when(j == 0)
    def _():
        carry_ref[...] = jnp.zeros_like(carry_ref)

    psz = pages[0].shape[-1]
    r_i = lax.broadcasted_iota(jnp.int32, (psz, psz), 0)
    c_i = lax.broadcasted_iota(jnp.int32, (psz, psz), 1)
    tri = jnp.where(r_i <= c_i, 1.0, 0.0).astype(BF16)
    rows = 8 * npg
    zpad = jnp.zeros((8 - FOX_HEADS, psz), F32)
    x = jnp.concatenate([blk for k in range(npg) for blk in (pages[k][0, 0], zpad)], axis=0)
    a, b, c = _split3(x)
    local = _dot(a, tri) + _dot(b, tri) + _dot(c, tri)
    tot = jnp.broadcast_to(local[:, psz - 1:psz], (rows, psz))
    p_r = lax.broadcasted_iota(jnp.int32, (rows, rows), 0)
    p_c = lax.broadcasted_iota(jnp.int32, (rows, rows), 1)
    earlier = jnp.where((p_c < p_r) & ((p_r - p_c) % 8 == 0), 1.0, 0.0).astype(BF16)
    ta, tb, tc = _split3(tot)
    offs = _dot(earlier, ta) + _dot(earlier, tb) + _dot(earlier, tc)
    cs = local + offs + jnp.tile(carry_ref[...], (npg, 1))
    for k in range(npg):
        row_ref[0, :, k * psz:(k + 1) * psz] = cs[8 * k:8 * (k + 1), :]
    carry_ref[...] = jnp.broadcast_to(cs[rows - 8:rows, psz - 1:psz], carry_ref.shape)


def _cumsum_paged(pt, layer, logf_pool_t):
    b, n_pages = pt.shape
    psz = logf_pool_t.shape[-1]
    npg = _pages_per_step(n_pages, PAGES_LOGF)
    return _paged_call(
        _cumsum_paged_kernel, pt, layer, [logf_pool_t], (FOX_HEADS, psz), [], [],
        pl.BlockSpec((1, 8, npg * psz), lambda i, j, pt: (i, 0, j)),
        jax.ShapeDtypeStruct((b, 8, n_pages * psz), F32),
        [pltpu.VMEM((8, psz), F32)], "cumsum_paged", pages=PAGES_LOGF)


def _compress_paged_kernel(pt_ref, *refs, npg):
    pages = refs[:npg]
    w1_ref, w2_ref, f_ref, s_ref, a_ref = refs[npg:npg + 5]
    psz = pages[0].shape[-1]
    spp = psz // CMP_STRIDE
    r_i = lax.broadcasted_iota(jnp.int32, (psz, psz), 0)
    c_i = lax.broadcasted_iota(jnp.int32, (psz, psz), 1)
    perm = jnp.where(c_i == CMP_STRIDE * (r_i % spp) + r_i // spp, 1.0, 0.0).astype(BF16)
    for k in range(npg):
        xp = _dot_nt(perm, pages[k][0, 0].astype(BF16)).astype(BF16)
        for p in range(CMP_STRIDE):
            a_ref[k * spp:(k + 1) * spp, p * 256:(p + 1) * 256] = xp[p * spp:(p + 1) * spp, :]
    a = a_ref[...]
    f_ref[0] = _dot(a, w1_ref[...])
    s_ref[0] = _dot(a, w2_ref[...])


def _compress_paged(pt, layer, pool, w1, w2):
    b, n_pages = pt.shape
    psz = pool.shape[-1]
    spp = psz // CMP_STRIDE
    npg = _pages_per_step(n_pages, PAGES_CMP)
    nseg = n_pages * spp
    wspec = pl.BlockSpec(w1.shape, lambda i, j, pt: (0, 0))
    ospec = pl.BlockSpec((1, npg * spp, 256), lambda i, j, pt: (i, j, 0))
    return _paged_call(
        _compress_paged_kernel, pt, layer, [pool], (256, psz), [w1, w2], [wspec, wspec],
        [ospec, ospec],
        [jax.ShapeDtypeStruct((b, nseg, 256), F32)] * 2,
        [pltpu.VMEM((npg * spp, CMP_STRIDE * 256), BF16)], "compress_paged", pages=PAGES_CMP)


def _nsa_sample_cmp_kernel(qn_ref, f_ref, s_ref, cb_ref, oc_ref, ps_ref, *, nseg, pos0, n_new):
    scale = HEAD_DIM ** -0.5
    tq = qn_ref.shape[1]
    qpos = pos0 + jnp.minimum(lax.broadcasted_iota(jnp.int32, (tq, 1), 0), n_new - 1)
    kv = f_ref[0] + pltpu.roll(s_ref[0], nseg - 1, 0) + cb_ref[0:1, :]
    kcmp = kv[:, 0:LANES].astype(BF16)
    vcmp = kv[:, LANES:2 * LANES].astype(BF16)
    for g in range(NSA_KV_HEADS):
        qn = _stack_rows(qn_ref.at[:, :, g * NSA_GROUP * LANES:(g + 1) * NSA_GROUP * LANES], NSA_GROUP, scale)
        o_c, psum = _cmp_branch(qn, kcmp, vcmp, qpos, nseg, nseg - 1)
        for i in range(NSA_GROUP):
            hblk = g * NSA_GROUP + i
            oc_ref[0, :, hblk * LANES:(hblk + 1) * LANES] = o_c[i]
        ps_ref[0, g] = psum


def _nsa_sample_cmp(qn, f, s, cb, pos0, n_new):
    b, tq, _ = qn.shape
    nseg = f.shape[1]
    fspec = pl.BlockSpec((1, nseg, 256), lambda i: (i, 0, 0))
    return pl.pallas_call(
        functools.partial(_nsa_sample_cmp_kernel, nseg=nseg, pos0=pos0, n_new=n_new),
        grid=(b,),
        in_specs=[pl.BlockSpec((1, tq, NSA_HEADS * LANES), lambda i: (i, 0, 0)), fspec, fspec,
                  pl.BlockSpec((8, 256), lambda i: (0, 0))],
        out_specs=[pl.BlockSpec((1, tq, NSA_HEADS * LANES), lambda i: (i, 0, 0)),
                   pl.BlockSpec((1, NSA_KV_HEADS, tq, nseg), lambda i: (i, 0, 0, 0))],
        out_shape=[jax.ShapeDtypeStruct((b, tq, NSA_HEADS * LANES), F32),
                   jax.ShapeDtypeStruct((b, NSA_KV_HEADS, tq, nseg), F32)],
        compiler_params=_params(("parallel",)),
        name="nsa_sample_cmp",
    )(qn, f, s, cb)


def _select_rows_kernel(ps_ref, sel_ref, sc_ref, *, nb, pos0, tq, n_new):
    rows = ps_ref.shape[0]
    qpos = pos0 + jnp.minimum(lax.broadcasted_iota(jnp.int32, (rows, 1), 0) % tq, n_new - 1)
    sel_ref[...] = _select_blocks(ps_ref[...], qpos, nb, sc_ref)


def _select_rows(psum, nb, pos0, tq, n_new):
    rows, ncmp = psum.shape
    nbp = -(-nb // LANES) * LANES
    return pl.pallas_call(
        functools.partial(_select_rows_kernel, nb=nb, pos0=pos0, tq=tq, n_new=n_new),
        grid=(1,),
        in_specs=[pl.BlockSpec((rows, ncmp), lambda i: (0, 0))],
        out_specs=pl.BlockSpec((rows, nbp), lambda i: (0, 0)),
        out_shape=jax.ShapeDtypeStruct((rows, nbp), F32),
        scratch_shapes=[pltpu.VMEM((nbp, rows), F32)],
        compiler_params=_params(("arbitrary",)),
        name="select_rows",
    )(psum)


def _state_update(st_refs, g, carry_fn):
    m_ref, l_ref, acc_ref = st_refs
    m, l, acc = carry_fn((m_ref[g], l_ref[g], acc_ref[g]))
    m_ref[g] = m
    l_ref[g] = l
    acc_ref[g] = acc


def _state_init(st_refs):
    m_ref, l_ref, acc_ref = st_refs
    m_ref[...] = jnp.full(m_ref.shape, NEG_INF, F32)
    l_ref[...] = jnp.zeros(l_ref.shape, F32)
    acc_ref[...] = jnp.zeros(acc_ref.shape, F32)


def _state_scratch(groups, r, tq):
    return [pltpu.VMEM((groups, r, tq, 1), F32), pltpu.VMEM((groups, r, tq, 1), F32),
            pltpu.VMEM((groups, r, tq, LANES), F32)]


def _nsa_sample_kernel(pt_ref, *refs, npg, pos0, n_new):
    pages = refs[:npg]
    (qr_ref, sel_ref, new_ref, wst_ref, wnew_ref, oc_ref, gate_ref, o_ref,
     m_ref, l_ref, acc_ref, k16, v16) = refs[npg:]
    st = (m_ref, l_ref, acc_ref)
    j = pl.program_id(1)
    tq = qr_ref.shape[1]
    psz = pages[0].shape[-1]
    tk = npg * psz
    scale = HEAD_DIM ** -0.5
    qpos = pos0 + jnp.minimum(lax.broadcasted_iota(jnp.int32, (tq, 1), 0), n_new - 1)

    @pl.when(j == 0)
    def _():
        _state_init(st)

    for k in range(npg):
        k16[:, k * psz:(k + 1) * psz] = pages[k][0, 0, 0:LANES, :].astype(BF16)
        v16[:, k * psz:(k + 1) * psz] = pages[k][0, 0, LANES:2 * LANES, :].astype(BF16)
    k0 = j * tk
    kpos = k0 + lax.broadcasted_iota(jnp.int32, (1, tk), 1)
    qrs = []
    keep = _expand_blocks(jnp.concatenate([sel_ref[0, g] for g in range(NSA_KV_HEADS)], axis=0), k0, tk,
                          as_float=True)
    causal = _mask_bias(kpos <= qpos)
    for g in range(NSA_KV_HEADS):
        qr = _stack_rows(qr_ref.at[:, :, g * NSA_GROUP * LANES:(g + 1) * NSA_GROUP * LANES], NSA_GROUP, scale)
        qrs.append(qr)
        bias = jnp.where(keep[g * tq:(g + 1) * tq] > 0.5, causal, NEG_INF)
        s = _dot(qr, k16[...]).reshape(NSA_GROUP, tq, tk) + bias[None]
        _state_update(st, g, functools.partial(_online_update_fast, s, v16[...], vt=True))

    @pl.when(j == pl.num_programs(1) - 1)
    def _():
        nrow = new_ref.shape[1]
        r_i = lax.broadcasted_iota(jnp.int32, (1, nrow), 1)
        npos = pos0 + r_i
        m_new = (r_i < n_new) & (npos <= qpos)
        wlen = wst_ref.shape[-1]
        wpos = pos0 - wlen + lax.broadcasted_iota(jnp.int32, (1, wlen), 1)
        m_old = (wpos > qpos - WINDOW) & (wpos >= 0)
        m_wnew = m_new & (npos > qpos - WINDOW)
        kn = new_ref[0, :, 0:LANES].astype(BF16)
        vn = new_ref[0, :, LANES:2 * LANES].astype(BF16)
        kwo = wst_ref[0, 0, 0:LANES, :].astype(BF16)
        vwo = wst_ref[0, 0, LANES:2 * LANES, :].astype(BF16)
        kwn = wnew_ref[0, :, 0:LANES].astype(BF16)
        vwn = wnew_ref[0, :, LANES:2 * LANES].astype(BF16)
        for g in range(NSA_KV_HEADS):
            qr = qrs[g]
            blk_ok = _expand_blocks(sel_ref[0, g], pos0, nrow)
            s = _dot_nt(qr, kn).reshape(NSA_GROUP, tq, nrow)
            _state_update(st, g, functools.partial(_online_update, s, (m_new & blk_ok)[None], vn))
            o_s = _online_final((m_ref[g], l_ref[g], acc_ref[g]))
            cw = _online_init(NSA_GROUP, tq)
            cw = _online_update(_dot(qr, kwo).reshape(NSA_GROUP, tq, wlen), m_old[None], vwo, cw, vt=True)
            cw = _online_update(_dot_nt(qr, kwn).reshape(NSA_GROUP, tq, nrow), m_wnew[None], vwn, cw)
            o_w = _online_final(cw)
            gates = gate_ref[0, :, g * LANES:(g + 1) * LANES]
            for i in range(NSA_GROUP):
                hblk = g * NSA_GROUP + i
                o_c = oc_ref[0, :, hblk * LANES:(hblk + 1) * LANES]
                o = (gates[:, 3 * i:3 * i + 1] * o_c + gates[:, 3 * i + 1:3 * i + 2] * o_s[i]
                     + gates[:, 3 * i + 2:3 * i + 3] * o_w[i])
                o_ref[0, :, hblk * LANES:(hblk + 1) * LANES] = o.astype(o_ref.dtype)


def _nsa_sample(pt, layer, pool, qr, sel, new, wst, wnew, oc, gate, pos0, n_new):
    b, n_pages = pt.shape
    psz = pool.shape[-1]
    npg = _pages_per_step(n_pages, PAGES)
    tq = qr.shape[1]
    nbp = sel.shape[-1]
    wlen = wst.shape[-1]
    full = lambda shp: pl.BlockSpec((1,) + shp, lambda i, j, pt: (i,) + (0,) * len(shp))
    return _paged_call(
        functools.partial(_nsa_sample_kernel, pos0=pos0, n_new=n_new), pt, layer, [pool], (256, psz),
        [qr, sel, new, wst, wnew, oc, gate],
        [full((tq, NSA_HEADS * LANES)), full((NSA_KV_HEADS, tq, nbp)), full((tq, 256)),
         pl.BlockSpec((1, 1, 256, wlen), lambda i, j, pt: (layer, i, 0, 0)),
         full((tq, 256)), full((tq, NSA_HEADS * LANES)), full((tq, 3 * LANES))],
        full((tq, NSA_HEADS * LANES)),
        jax.ShapeDtypeStruct((b, tq, NSA_HEADS * LANES), BF16),
        _state_scratch(NSA_KV_HEADS, NSA_GROUP, tq)
        + [pltpu.VMEM((LANES, npg * psz), BF16)] * 2, "nsa_sample", pages=PAGES)


def _diff_sample_kernel(pt_ref, *refs, npg, pos0, n_new, lam_init):
    pages = refs[:npg]
    q_ref, new_ref, dl_ref, gain_ref, o_ref, m_ref, l_ref, acc_ref, k16, v16 = refs[npg:]
    st = (m_ref, l_ref, acc_ref)
    j = pl.program_id(1)
    tq = q_ref.shape[1]
    psz = pages[0].shape[-1]
    tk = npg * psz
    qpos = pos0 + jnp.minimum(lax.broadcasted_iota(jnp.int32, (tq, 1), 0), n_new - 1)

    @pl.when(j == 0)
    def _():
        _state_init(st)

    kpos = j * tk + lax.broadcasted_iota(jnp.int32, (1, tk), 1)
    bias = _mask_bias(kpos <= qpos)[None]
    qs = []
    for kb in range(2):
        for k in range(npg):
            k16[kb, :, k * psz:(k + 1) * psz] = pages[k][0, 0, kb * LANES:(kb + 1) * LANES, :].astype(BF16)
            v16[kb, :, k * psz:(k + 1) * psz] = pages[k][0, 0, (2 + kb) * LANES:(3 + kb) * LANES, :].astype(BF16)
        q = _stack_rows(q_ref.at[:, :, kb * 4 * LANES:(kb + 1) * 4 * LANES], 4, DIFF_HALF ** -0.5)
        qs.append(q)
        s = _dot(q, k16[kb]).reshape(4, tq, tk) + bias
        _state_update(st, kb, functools.partial(_online_update_fast, s, v16[kb], vt=True))

    @pl.when(j == pl.num_programs(1) - 1)
    def _():
        nrow = new_ref.shape[1]
        r_i = lax.broadcasted_iota(jnp.int32, (1, nrow), 1)
        m_new = ((r_i < n_new) & (pos0 + r_i <= qpos))[None]
        lam = _diff_lambda(dl_ref, lam_init)
        for kb in range(2):
            kn = new_ref[0, :, kb * LANES:(kb + 1) * LANES].astype(BF16)
            vn = new_ref[0, :, (2 + kb) * LANES:(3 + kb) * LANES].astype(BF16)
            s = _dot_nt(qs[kb], kn).reshape(4, tq, nrow)
            _state_update(st, kb, functools.partial(_online_update, s, m_new, vn))
            o = _online_final((m_ref[kb], l_ref[kb], acc_ref[kb]))
            _diff_finish(o, lam, gain_ref, lam_init, o_ref.at[:, :, kb * 2 * LANES:(kb + 1) * 2 * LANES])


def _diff_sample(pt, layer, pool, dq, new, dl, gain, lam_init, pos0, n_new):
    b, n_pages = pt.shape
    psz = pool.shape[-1]
    npg = _pages_per_step(n_pages, PAGES)
    tq = dq.shape[1]
    full = lambda shp: pl.BlockSpec((1,) + shp, lambda i, j, pt: (i,) + (0,) * len(shp))
    const = lambda shp: pl.BlockSpec(shp, lambda i, j, pt: (0,) * len(shp))
    return _paged_call(
        functools.partial(_diff_sample_kernel, pos0=pos0, n_new=n_new, lam_init=lam_init), pt, layer, [pool],
        (512, psz), [dq, new, dl, _gain_lanes(gain)],
        [full((tq, 8 * LANES)), full((tq, 512)), const((4, DIFF_HALF)), const((1, LANES))],
        full((tq, 4 * LANES)),
        jax.ShapeDtypeStruct((b, tq, 4 * LANES), BF16),
        _state_scratch(2, 4, tq) + [pltpu.VMEM((2, LANES, npg * psz), BF16)] * 2, "diff_sample", pages=PAGES)


def _fox_sample_kernel(pt_ref, *refs, npg, pos0, n_new):
    pages = refs[:npg]
    q_ref, new_ref, cr_ref, tot_ref, lf_ref, o_ref, m_ref, l_ref, acc_ref, k16, v16 = refs[npg:]
    st = (m_ref, l_ref, acc_ref)
    j = pl.program_id(1)
    tq = q_ref.shape[1]
    psz = pages[0].shape[-1]
    tk = npg * psz
    qpos = pos0 + jnp.minimum(lax.broadcasted_iota(jnp.int32, (tq, 1), 0), n_new - 1)

    @pl.when(j == 0)
    def _():
        _state_init(st)

    lf = lf_ref[0]
    row = lax.broadcasted_iota(jnp.int32, (tq, 1), 0)
    cnew = jnp.zeros_like(lf)
    for t in range(n_new):
        cnew = cnew + jnp.where(row >= t, lf[t:t + 1, :], 0.0)
    psz_l = tot_ref.shape[-1]
    kpos = j * tk + lax.broadcasted_iota(jnp.int32, (1, tk), 1)
    bias = _mask_bias(kpos <= qpos)[None]
    qs = []
    for kb in range(2):
        for k in range(npg):
            k16[kb, :, k * psz:(k + 1) * psz] = pages[k][0, 0, kb * LANES:(kb + 1) * LANES, :].astype(BF16)
            v16[kb, :, k * psz:(k + 1) * psz] = pages[k][0, 0, (2 + kb) * LANES:(3 + kb) * LANES, :].astype(BF16)
        q = _stack_rows(q_ref.at[:, :, kb * 2 * LANES:(kb + 1) * 2 * LANES], 2, HEAD_DIM ** -0.5)
        qs.append(q)
        cq = jnp.stack([tot_ref[0, 2 * kb + hh:2 * kb + hh + 1, psz_l - 1:psz_l]
                        + cnew[:, 2 * kb + hh:2 * kb + hh + 1] for hh in range(2)])
        ck = cr_ref[0, 2 * kb:2 * kb + 2, :][:, None, :]
        s = _dot(q, k16[kb]).reshape(2, tq, tk) + cq - ck + bias
        _state_update(st, kb, functools.partial(_online_update_fast, s, v16[kb], vt=True))

    @pl.when(j == pl.num_programs(1) - 1)
    def _():
        nrow = new_ref.shape[1]
        r_i = lax.broadcasted_iota(jnp.int32, (1, nrow), 1)
        m_new = ((r_i < n_new) & (pos0 + r_i <= qpos))[None]
        for kb in range(2):
            kn = new_ref[0, :, kb * LANES:(kb + 1) * LANES].astype(BF16)
            vn = new_ref[0, :, (2 + kb) * LANES:(3 + kb) * LANES].astype(BF16)
            bias = []
            for hh in range(2):
                h = 2 * kb + hh
                d = jnp.zeros((tq, nrow), F32)
                for t in range(n_new):
                    d = d + jnp.where((row >= t) & (r_i < t), lf[t:t + 1, h:h + 1], 0.0)
                bias.append(d)
            s = _dot_nt(qs[kb], kn).reshape(2, tq, nrow) + jnp.stack(bias)
            _state_update(st, kb, functools.partial(_online_update, s, m_new, vn))
            o = _online_final((m_ref[kb], l_ref[kb], acc_ref[kb]))
            for hh in range(2):
                hblk = 2 * kb + hh
                o_ref[0, :, hblk * LANES:(hblk + 1) * LANES] = o[hh].astype(o_ref.dtype)


def _fox_sample(pt, layer, pool, fq, new, crow, lf, pos0, n_new):
    b, n_pages = pt.shape
    psz = pool.shape[-1]
    npg = _pages_per_step(n_pages, PAGES)
    tq = fq.shape[1]
    past = crow.shape[-1]
    full = lambda shp: pl.BlockSpec((1,) + shp, lambda i, j, pt: (i,) + (0,) * len(shp))
    return _paged_call(
        functools.partial(_fox_sample_kernel, pos0=pos0, n_new=n_new), pt, layer, [pool], (512, psz),
        [fq, new, crow, crow, lf],
        [full((tq, 4 * LANES)), full((tq, 512)),
         pl.BlockSpec((1, 8, npg * psz), lambda i, j, pt: (i, 0, j)),
         pl.BlockSpec((1, 8, LANES), lambda i, j, pt: (i, 0, past // LANES - 1)),
         pl.BlockSpec((1, tq, LANES), lambda i, j, pt: (i, 0, 2))],
        full((tq, 4 * LANES)),
        jax.ShapeDtypeStruct((b, tq, 4 * LANES), BF16),
        _state_scratch(2, 2, tq) + [pltpu.VMEM((2, LANES, npg * psz), BF16)] * 2, "fox_sample", pages=PAGES)


def _wo_rows():
    import numpy as np
    rows = []
    for h in range(NSA_HEADS):
        r = np.full(LANES, -1, np.int64)
        g = h // NSA_GROUP
        r[64 * g:64 * g + 64] = 64 * h + np.arange(64)
        rows.append(r)
    for base, nh in ((NSA_HEADS * HEAD_DIM, DIFF_HEADS), ((NSA_HEADS + DIFF_HEADS) * HEAD_DIM, FOX_HEADS)):
        for h in range(nh):
            r = np.full(LANES, -1, np.int64)
            r[64 * (h % 2):64 * (h % 2) + 64] = base + 64 * h + np.arange(64)
            rows.append(r)
    return np.concatenate(rows)


def _extend_wo(w):
    import numpy as np
    idx = _wo_rows()
    return jnp.where(jnp.asarray(idx >= 0)[:, None], w[jnp.asarray(np.maximum(idx, 0))], 0.0).astype(BF16)


def _mem_and_peer(x2, bsz, kv, lw):
    n, d = x2.shape
    q = _linear(x2, lw["w_mq"], gain=lw["g_mem"])
    t = n // bsz
    if t < TP:
        qp = jnp.pad(q.reshape(bsz, t, d), ((0, 0), (0, TP - t), (0, 0)))
        om = _mem_attn(qp, kv)[:, :t]
    else:
        om = _mem_attn(q.reshape(bsz, t, d), kv)
    x2 = _linear(om.reshape(n, d), lw["w_mo"], res=x2)
    return _peer(x2, lw["g_ffn"], lw["peer_wq"], lw["peer_keys"], lw["peer_u"], lw["peer_vt"])


def _prompt_layer(x2, bsz, mem_prompt, lw, tabs):
    n, d = x2.shape
    t = n // bsz
    outs = _inproj(x2, lw["g_attn"], lw["w_ext"], lw["b_ext"], tabs[0], tabs[1], 256)
    o = {name: v.reshape(bsz, t, -1) for (name, _, _), v in zip(_SEGS, outs)}
    f, s, cb = _compress(o["cmp"].reshape(bsz, t // CMP_STRIDE, CMP_STRIDE * 256), *lw["cmp_w"])
    o_nsa = _nsa_prompt(o["qn"], o["qr"], f, s, cb, o["sel"], o["win"], o["gate"])
    o_diff = _diff_prompt(o["dq"], o["diff"], lw["diff_lambda"], lw["diff_gain"], lw["lam_init"])
    ccol, crow = _cumsum(o["gate"])
    o_fox = _fox_prompt(o["fq"], o["fox"], ccol, crow)
    x2 = _linear_parts([o_nsa.reshape(n, -1), o_diff.reshape(n, -1), o_fox.reshape(n, -1)], lw["w_o_ext"], x2)
    m = mem_prompt.shape[1]
    mkv = _linear(mem_prompt.reshape(bsz * m, d), lw["w_mkv"])
    x2 = _mem_and_peer(x2, bsz, mkv.reshape(bsz, m, 2 * d), lw)
    return x2, o, mkv, cb


def _sample_layer(x2, bsz, pt, layer, pools, wstate, mem_kv, lw, tabs, cb, pos0):
    n, d = x2.shape
    t = n // bsz
    outs = _inproj(x2, lw["g_attn"], lw["w_ext"], lw["b_ext"], tabs[0], tabs[1], n)
    o = {name: v.reshape(bsz, t, -1) for (name, _, _), v in zip(_SEGS, outs)}
    op = {name: jnp.pad(v, ((0, 0), (0, TP - t), (0, 0))) for name, v in o.items()}
    cmp_pool, sel_pool, diff_pool, fox_pool, logf_pool_t = pools
    f, s = _compress_paged(pt, layer, cmp_pool, lw["cmp_w"][0], lw["cmp_w"][1])
    nseg = f.shape[1]
    oc, ps = _nsa_sample_cmp(op["qn"], f, s, cb, pos0, t)
    nb = -(-(pos0 + t) // SEL_BLOCK)
    sel = _select_rows(ps.reshape(bsz * NSA_KV_HEADS * TP, nseg), nb, pos0, TP, t)
    sel = sel.reshape(bsz, NSA_KV_HEADS, TP, -1)
    o_nsa = _nsa_sample(pt, layer, sel_pool, op["qr"], sel, op["sel"], wstate, op["win"], oc, op["gate"], pos0, t)
    o_diff = _diff_sample(pt, layer, diff_pool, op["dq"], op["diff"], lw["diff_lambda"], lw["diff_gain"],
                          lw["lam_init"], pos0, t)
    crow = _cumsum_paged(pt, layer, logf_pool_t)
    o_fox = _fox_sample(pt, layer, fox_pool, op["fq"], op["fox"], crow, op["gate"], pos0, t)
    mixed = jnp.concatenate([o_nsa, o_diff, o_fox], axis=-1)[:, :t].reshape(n, -1)
    x2 = _linear(mixed, lw["w_o_ext"], res=x2)
    x2 = _mem_and_peer(x2, bsz, mem_kv, lw)
    return x2, o


def kernel(x_prompt, x_sample, cache_nsa_cmp_kv, cache_nsa_sel_kv, cache_diff_kv, cache_fox_kv, cache_fox_logf, state_nsa_win_kv, cache_mem_kv, page_table, mem_prompt, g_attn, w_in, b_in, nsa_pe_k, nsa_pe_v, nsa_w_ck, nsa_w_cv, diff_lambda, diff_gain, w_o, g_mem, w_mq, w_mkv, w_mo, g_ffn, peer_wq, peer_keys, peer_u, peer_v, g_final):
    bp, tp, d = x_prompt.shape
    bs, ts, _ = x_sample.shape
    depth = w_in.shape[0]
    n_pool, psz = cache_nsa_cmp_kv.shape[1:3]
    pos0 = page_table.shape[1] * psz
    assert pos0 % CMP_STRIDE == 0 and ts < CMP_STRIDE and ts <= TP
    pos_p = jnp.arange(tp, dtype=jnp.int32)
    pos_s = pos0 + (jnp.arange(bs * ts, dtype=jnp.int32) % ts)
    tabs_p = (_rope_tables(pos_p, HEAD_DIM, ROT_DIM), _rope_tables(pos_p, DIFF_HALF, DIFF_ROT))
    tabs_s = (_rope_tables(pos_s, HEAD_DIM, ROT_DIM), _rope_tables(pos_s, DIFF_HALF, DIFF_ROT))
    xp = x_prompt.reshape(bp * tp, d)
    xs = x_sample.reshape(bs * ts, d)
    names = ("cmp", "sel", "diff", "fox")
    rows_p = {k: [] for k in names + ("logf", "win", "mem")}
    rows_s = {k: [] for k in names + ("logf", "win")}
    pools = tuple(_feature_major(c) for c in
                  (cache_nsa_cmp_kv, cache_nsa_sel_kv, cache_diff_kv, cache_fox_kv, cache_fox_logf))
    wstate_t = _feature_major(state_nsa_win_kv)
    for l in range(depth):
        w_ext, b_ext = _extend_inproj(w_in[l], b_in[l])
        lw = dict(
            g_attn=g_attn[l], w_ext=w_ext, b_ext=b_ext,
            cmp_w=_compress_weights(nsa_w_ck[l], nsa_w_cv[l], nsa_pe_k[l], nsa_pe_v[l]),
            diff_lambda=diff_lambda[l], diff_gain=diff_gain[l], lam_init=0.8 - 0.6 * math.exp(-0.3 * l),
            w_o_ext=_extend_wo(w_o[l]), g_mem=g_mem[l], w_mq=w_mq[l], w_mkv=w_mkv[l], w_mo=w_mo[l],
            g_ffn=g_ffn[l], peer_wq=peer_wq[l], peer_keys=peer_keys[l],
            peer_u=peer_u[l].astype(BF16), peer_vt=peer_v[l].T.astype(BF16))
        xp, o, mkv, cb = _prompt_layer(xp, bp, mem_prompt, lw, tabs_p)
        wstate = state_nsa_win_kv[l].reshape(bs, -1, 256)
        xs, os_ = _sample_layer(xs, bs, page_table, l, pools, wstate_t, cache_mem_kv[l].reshape(bs, -1, 2 * d), lw,
                                tabs_s, cb, pos0)
        for k in names:
            rows_p[k].append(o[k])
            rows_s[k].append(os_[k])
        rows_p["logf"].append(o["gate"][..., 2 * LANES:2 * LANES + FOX_HEADS])
        rows_s["logf"].append(os_["gate"][..., 2 * LANES:2 * LANES + FOX_HEADS])
        rows_p["win"].append(o["win"][:, -min(WINDOW, tp):])
        win_all = jnp.concatenate([wstate, os_["win"]], axis=1)
        rows_s["win"].append(win_all[:, -min(WINDOW, win_all.shape[1]):])
        rows_p["mem"].append(mkv)
    y_p = _final_norm(xp, g_final).reshape(bp, tp, d)
    y_s = _final_norm(xs, g_final).reshape(bs, ts, d)

    def st(lst, tail):
        a = jnp.stack(lst, axis=0)
        return a.reshape(a.shape[:3] + tail)

    kv2 = (2, NSA_KV_HEADS, HEAD_DIM)
    kv4 = (2, DIFF_HEADS, HEAD_DIM)
    return (y_p, y_s,
            st(rows_p["cmp"], kv2), st(rows_s["cmp"], kv2), st(rows_p["sel"], kv2), st(rows_s["sel"], kv2),
            st(rows_p["diff"], kv4), st(rows_s["diff"], kv4), st(rows_p["fox"], kv4), st(rows_s["fox"], kv4),
            st(rows_p["logf"], (FOX_HEADS,)), st(rows_s["logf"], (FOX_HEADS,)),
            st(rows_p["win"], kv2), st(rows_s["win"], kv2),
            jnp.stack(rows_p["mem"], 0).reshape(depth, bp, -1, 2, MEM_HEADS, d // MEM_HEADS))
```

```python
import functools
import math

import jax
import jax.numpy as jnp
from jax import lax
from jax.experimental import pallas as pl
from jax.experimental.pallas import tpu as pltpu

F32 = jnp.float32
BF16 = jnp.bfloat16

HEAD_DIM = 64
ROT_DIM = HEAD_DIM // 4
ROPE_THETA = 500000.0
NSA_HEADS = 8
NSA_KV_HEADS = 2
NSA_GROUP = NSA_HEADS // NSA_KV_HEADS
CMP_STRIDE = 16
CMP_BLOCK = 32
SEL_BLOCK = 64
N_SEL = 16
WINDOW = 512
DIFF_HEADS = 4
DIFF_HALF = HEAD_DIM // 2
DIFF_ROT = DIFF_HALF // 4
FOX_HEADS = 4
MEM_HEADS = 4
PEER_HEADS = 8
PEER_KEYS = 128
PEER_TOPK = 16
PEER_HALF = 128
EPS = 1e-6
NEG_INF = -1e30
FORCE_SCORE = 1e4
LANES = 128
VMEM_LIMIT = 56 * 1024 * 1024


def _params(sem, vmem=VMEM_LIMIT):
    return pltpu.CompilerParams(dimension_semantics=sem, vmem_limit_bytes=vmem)


def _dot(a, b):
    return jnp.dot(a, b, preferred_element_type=F32)


def _dot_nt(a, b):
    return lax.dot_general(a, b, (((1,), (1,)), ((), ())), preferred_element_type=F32)


def _rms(x, g):
    return x * lax.rsqrt(jnp.mean(x * x, axis=-1, keepdims=True) + EPS) * g


def _linear_kernel(*refs, has_gain, has_bias, has_res):
    it = iter(refs)
    x_ref = next(it)
    g_ref = next(it) if has_gain else None
    w_ref = next(it)
    b_ref = next(it) if has_bias else None
    r_ref = next(it) if has_res else None
    o_ref = next(it)
    xb_ref = next(it)

    @pl.when(pl.program_id(1) == 0)
    def _():
        x = x_ref[...].astype(F32)
        if has_gain:
            x = _rms(x, g_ref[...])
        xb_ref[...] = x.astype(BF16)

    y = _dot(xb_ref[...], w_ref[...])
    if has_bias:
        y = y + b_ref[...]
    if has_res:
        y = y + r_ref[...]
    o_ref[...] = y.astype(o_ref.dtype)


def _linear(x, w, gain=None, bias=None, res=None, tm=512, tn=512, out_dtype=F32):
    m, k = x.shape
    n = w.shape[1]
    tm = min(tm, m)
    tn = min(tn, n)
    assert m % tm == 0 and n % tn == 0
    args = [x]
    specs = [pl.BlockSpec((tm, k), lambda i, j: (i, 0))]
    if gain is not None:
        args.append(gain.reshape(1, k))
        specs.append(pl.BlockSpec((1, k), lambda i, j: (0, 0)))
    args.append(w.astype(BF16))
    specs.append(pl.BlockSpec((k, tn), lambda i, j: (0, j)))
    if bias is not None:
        args.append(bias.reshape(1, n))
        specs.append(pl.BlockSpec((1, tn), lambda i, j: (0, j)))
    if res is not None:
        args.append(res)
        specs.append(pl.BlockSpec((tm, tn), lambda i, j: (i, j)))
    return pl.pallas_call(
        functools.partial(_linear_kernel, has_gain=gain is not None, has_bias=bias is not None,
                          has_res=res is not None),
        grid=(m // tm, n // tn),
        in_specs=specs,
        out_specs=pl.BlockSpec((tm, tn), lambda i, j: (i, j)),
        out_shape=jax.ShapeDtypeStruct((m, n), out_dtype),
        scratch_shapes=[pltpu.VMEM((tm, k), BF16)],
        compiler_params=_params(("parallel", "arbitrary")),
        name="linear",
    )(*args)


def _linear_parts_kernel(*refs, widths):
    xs = refs[:len(widths)]
    w_ref, r_ref, o_ref = refs[len(widths):]
    y = r_ref[...]
    k0 = 0
    for x_ref, wd in zip(xs, widths):
        y = y + _dot(x_ref[...], w_ref[k0:k0 + wd, :])
        k0 += wd
    o_ref[...] = y


def _linear_parts(parts, w, res, tm=512, tn=512):
    m = parts[0].shape[0]
    widths = tuple(p.shape[1] for p in parts)
    k, n = w.shape
    assert sum(widths) == k and m % tm == 0 and n % tn == 0
    return pl.pallas_call(
        functools.partial(_linear_parts_kernel, widths=widths),
        grid=(m // tm, n // tn),
        in_specs=[pl.BlockSpec((tm, wd), lambda i, j: (i, 0)) for wd in widths]
        + [pl.BlockSpec((k, tn), lambda i, j: (0, j)), pl.BlockSpec((tm, tn), lambda i, j: (i, j))],
        out_specs=pl.BlockSpec((tm, tn), lambda i, j: (i, j)),
        out_shape=jax.ShapeDtypeStruct((m, n), F32),
        compiler_params=_params(("parallel", "arbitrary")),
        name="linear_parts",
    )(*parts, w, res)


def _peer_scores_kernel(x_ref, g_ref, wq_ref, keys_ref, xn_ref, st_ref):
    hb = _rms(x_ref[...], g_ref[...]).astype(BF16)
    xn_ref[...] = hb
    qb = _dot(hb, wq_ref[...]).astype(BF16)
    for hc in range(2 * PEER_HEADS):
        st_ref[hc] = _dot_nt(keys_ref[hc], qb[:, hc * PEER_HALF:(hc + 1) * PEER_HALF])


def _peer_scores(x, g, wq, keys):
    n, d = x.shape
    tm = min(256, n)
    nq = wq.shape[1]
    hc = 2 * PEER_HEADS
    return pl.pallas_call(
        _peer_scores_kernel,
        grid=(n // tm,),
        in_specs=[pl.BlockSpec((tm, d), lambda i: (i, 0)),
                  pl.BlockSpec((1, d), lambda i: (0, 0)),
                  pl.BlockSpec((d, nq), lambda i: (0, 0)),
                  pl.BlockSpec((hc, PEER_KEYS, PEER_HALF), lambda i: (0, 0, 0))],
        out_specs=[pl.BlockSpec((tm, d), lambda i: (i, 0)),
                   pl.BlockSpec((hc, PEER_KEYS, tm), lambda i: (0, 0, i))],
        out_shape=[jax.ShapeDtypeStruct((n, d), BF16),
                   jax.ShapeDtypeStruct((hc, PEER_KEYS, n), F32)],
        compiler_params=_params(("parallel",)),
        name="peer_scores",
    )(x, g.reshape(1, d), wq.astype(BF16), keys.reshape(hc, PEER_KEYS, PEER_HALF).astype(BF16))


_PEER_PAIRS = [(a, b) for a in range(PEER_TOPK) for b in range(PEER_TOPK) if (a + 1) * (b + 1) <= PEER_TOPK]


def _peer_topk_kernel(st_ref, t_ref, r_ref, a_ref, c_ref, rk_ref, sv_ref, av_ref, cn_ref):
    tn = st_ref.shape[-1]
    row = lax.broadcasted_iota(jnp.int32, (PEER_KEYS, tn), 0).astype(F32)
    big = float(PEER_KEYS)

    rk_ref[...] = jnp.full(rk_ref.shape, big, F32)
    nch = 2
    for h0 in range(0, 2 * PEER_HEADS, nch):

        def extract(t, carry, h0=h0):
            out = []
            for k, s in enumerate(carry):
                h, c = divmod(h0 + k, 2)
                m = jnp.max(s, axis=0, keepdims=True)
                idx = jnp.min(jnp.where(s == m, row, big), axis=0, keepdims=True)
                hit = row == idx
                sv_ref[c, t, h:h + 1, :] = m
                rk_ref[h0 + k] = jnp.where(hit, jnp.asarray(t).astype(F32), rk_ref[h0 + k])
                out.append(jnp.where(hit, -jnp.inf, s))
            return tuple(out)

        lax.fori_loop(0, PEER_TOPK, extract, tuple(st_ref[h0 + k] for k in range(nch)))

    cand = [sv_ref[0, a] + sv_ref[1, b] for a, b in _PEER_PAIRS]
    top = sv_ref[0, 0] + sv_ref[1, 0]
    sel = []
    for ia, (a, b) in enumerate(_PEER_PAIRS):
        cnt = jnp.zeros_like(top)
        for ib, (a2, b2) in enumerate(_PEER_PAIRS):
            if ib == ia:
                continue
            ahead = (cand[ib] >= cand[ia]) if (a2 * PEER_TOPK + b2) < (a * PEER_TOPK + b) else (cand[ib] > cand[ia])
            cnt = cnt + jnp.where(ahead, 1.0, 0.0)
        sel.append(jnp.where(cnt < float(PEER_TOPK), 1.0, 0.0))
    z = jnp.zeros_like(top)
    counts = [jnp.zeros_like(top) for _ in range(PEER_TOPK)]
    for ia, (a, b) in enumerate(_PEER_PAIRS):
        z = z + sel[ia] * jnp.exp(cand[ia] - top)
        counts[a] = counts[a] + sel[ia]
    for a in range(PEER_TOPK):
        av_ref[a] = jnp.exp(sv_ref[0, a] - sv_ref[0, 0]) / z
        cn_ref[a] = counts[a]

    for h in range(PEER_HEADS):
        rank1 = rk_ref[2 * h]
        rank2 = rk_ref[2 * h + 1]

        def scatter(a, carry, h=h, rank1=rank1):
            wa, ca = carry
            hit = rank1 == jnp.asarray(a).astype(F32)
            wa = jnp.where(hit, av_ref[a, h:h + 1, :], wa)
            ca = jnp.where(hit, cn_ref[a, h:h + 1, :], ca)
            return wa, ca

        zero = jnp.zeros((PEER_KEYS, tn), F32)
        wa, ca = lax.fori_loop(0, PEER_TOPK, scatter, (zero, zero))
        a_ref[h] = wa.astype(a_ref.dtype)
        c_ref[h] = ca.astype(c_ref.dtype)
        r_ref[h] = rank2.astype(r_ref.dtype)
        t_ref[h] = jnp.where(rank2 < float(PEER_TOPK), jnp.exp(st_ref[2 * h + 1] - sv_ref[1, 0, h:h + 1, :]),
                             0.0).astype(t_ref.dtype)


def _peer_topk(st):
    hc, nk, n = st.shape
    tn = LANES
    spec_h = pl.BlockSpec((PEER_HEADS, nk, tn), lambda i: (0, 0, i))
    shape_h = jax.ShapeDtypeStruct((PEER_HEADS, nk, n), F32)
    shape_b = jax.ShapeDtypeStruct((PEER_HEADS, nk, n), BF16)
    return pl.pallas_call(
        _peer_topk_kernel,
        grid=(n // tn,),
        in_specs=[pl.BlockSpec((hc, nk, tn), lambda i: (0, 0, i))],
        out_specs=[spec_h] * 4,
        out_shape=[shape_b, shape_b, shape_h, shape_h],
        scratch_shapes=[pltpu.VMEM((hc, nk, tn), F32),
                        pltpu.VMEM((2, PEER_TOPK, PEER_HEADS, tn), F32),
                        pltpu.VMEM((PEER_TOPK, PEER_HEADS, tn), F32),
                        pltpu.VMEM((PEER_TOPK, PEER_HEADS, tn), F32)],
        compiler_params=_params(("parallel",)),
        name="peer_topk",
    )(st)


def _gelu(a):
    return 0.5 * a * (1.0 + lax.erf(a * math.sqrt(0.5)))


def _peer_dense_kernel(xn_ref, u_ref, vt_ref, t_ref, r_ref, a_ref, c_ref, res_ref, o_ref, acc_ref, h_ref):
    e = pl.program_id(1)
    te = u_ref.shape[0]
    nib = te // PEER_KEYS
    tm = xn_ref.shape[0]
    sub = 16

    @pl.when(e == 0)
    def _():
        acc_ref[...] = jnp.zeros_like(acc_ref)

    at = _dot_nt(u_ref[...], xn_ref[...])
    for ib in range(nib):
        i = e * nib + ib
        g = jnp.zeros((PEER_KEYS // sub, sub, tm), BF16)
        for h in range(PEER_HEADS):
            cnt = jnp.broadcast_to(c_ref[h, pl.ds(i, 1), :], (sub, tm)).astype(BF16)[None]
            wa = jnp.broadcast_to(a_ref[h, pl.ds(i, 1), :], (sub, tm)).astype(BF16)[None]
            r3 = r_ref[h].reshape(PEER_KEYS // sub, sub, tm)
            t3 = t_ref[h].reshape(PEER_KEYS // sub, sub, tm)
            g = g + jnp.where(r3 < cnt, t3 * wa, 0.0)
        rows = slice(ib * PEER_KEYS, (ib + 1) * PEER_KEYS)
        h_ref[rows, :] = g.reshape(PEER_KEYS, tm) * _gelu(at[rows, :]).astype(BF16)
    acc_ref[...] += _dot(vt_ref[...], h_ref[...])

    @pl.when(e == pl.num_programs(1) - 1)
    def _():
        o_ref[...] = res_ref[...] + acc_ref[...].T


def _peer_dense(xn, u, vt, t, r, a, c, res, te=2048):
    n, d = xn.shape
    ne = u.shape[0]
    tm = min(512, n)
    spec_h = pl.BlockSpec((PEER_HEADS, PEER_KEYS, tm), lambda i, e: (0, 0, i))
    return pl.pallas_call(
        _peer_dense_kernel,
        grid=(n // tm, ne // te),
        in_specs=[pl.BlockSpec((tm, d), lambda i, e: (i, 0)),
                  pl.BlockSpec((te, d), lambda i, e: (e, 0)),
                  pl.BlockSpec((d, te), lambda i, e: (0, e)),
                  spec_h, spec_h, spec_h, spec_h,
                  pl.BlockSpec((tm, d), lambda i, e: (i, 0))],
        out_specs=pl.BlockSpec((tm, d), lambda i, e: (i, 0)),
        out_shape=jax.ShapeDtypeStruct((n, d), F32),
        scratch_shapes=[pltpu.VMEM((d, tm), F32), pltpu.VMEM((te, tm), BF16)],
        compiler_params=_params(("parallel", "arbitrary")),
        name="peer_dense",
    )(xn, u, vt, t, r, a, c, res)


def _peer(x, g, wq, keys, u_b, vt_b):
    xn, st = _peer_scores(x, g, wq, keys)
    t, r, a, c = _peer_topk(st)
    return _peer_dense(xn, u_b, vt_b, t, r, a, c, x)


_SEGS = (
    ("qn", 1024, "p" * 8),
    ("qr", 1024, "a" * 8),
    ("cmp", 256, "pp"),
    ("sel", 256, "ap"),
    ("win", 256, "ap"),
    ("dq", 1024, "b" * 8),
    ("diff", 512, "bbpp"),
    ("fq", 512, "pppp"),
    ("fox", 512, "pppp"),
    ("gate", 384, "ssl"),
)
_OFF = dict(nq=0, kc=512, vc=640, ks=768, vs=896, kw=1024, vw=1152, ng=1280, dq=1304, dk=1560, dv=1816,
            fq=2072, fk=2328, fv=2584, ff=2840)


def _inproj_columns():
    import numpy as np
    cols = []

    def blocks(n):
        return [np.full(LANES, -1, np.int64) for _ in range(n)]

    for base in ("nq", "nq"):
        bl = blocks(NSA_HEADS)
        for h in range(NSA_HEADS):
            g = h // NSA_GROUP
            bl[h][64 * g:64 * g + 64] = _OFF[base] + 64 * h + np.arange(64)
        cols += bl
    cols.append(np.arange(_OFF["kc"], _OFF["kc"] + 256))
    cols.append(np.arange(_OFF["ks"], _OFF["ks"] + 256))
    cols.append(np.arange(_OFF["kw"], _OFF["kw"] + 256))
    bl = blocks(2 * DIFF_HEADS)
    for h in range(DIFF_HEADS):
        for c in range(2):
            o = 32 * (2 * (h % 2) + c)
            bl[2 * h + c][o:o + 32] = _OFF["dq"] + 64 * h + 32 * c + np.arange(32)
    cols += bl
    cols.append(np.arange(_OFF["dk"], _OFF["dk"] + 512))
    bl = blocks(FOX_HEADS)
    for h in range(FOX_HEADS):
        o = 64 * (h % 2)
        bl[h][o:o + 64] = _OFF["fq"] + 64 * h + np.arange(64)
    cols += bl
    cols.append(np.arange(_OFF["fk"], _OFF["fk"] + 512))
    bl = blocks(3)
    for g in range(NSA_KV_HEADS):
        bl[g][0:3 * NSA_GROUP] = _OFF["ng"] + 3 * NSA_GROUP * g + np.arange(3 * NSA_GROUP)
    bl[2][0:FOX_HEADS] = _OFF["ff"] + np.arange(FOX_HEADS)
    cols += bl
    return np.concatenate(cols)


def _extend_inproj(w, b):
    import numpy as np
    idx = _inproj_columns()
    keep = jnp.asarray(idx >= 0)
    src = jnp.asarray(np.maximum(idx, 0))
    return (jnp.where(keep[None, :], w[:, src], 0.0).astype(BF16),
            jnp.where(keep, b[src], 0.0).reshape(1, -1))


def _rope_tables(pos, period, rot):
    half = rot // 2
    inv = jnp.power(jnp.float32(ROPE_THETA), -jnp.arange(half, dtype=F32) / half)
    ang = pos.astype(F32)[:, None] * inv[None, :]
    cos, sin = jnp.cos(ang), jnp.sin(ang)
    n = pos.shape[0]
    reps = LANES // period
    pad = jnp.zeros((n, period - rot), F32)
    c = jnp.concatenate([cos, cos, pad + 1.0], axis=1)
    s_up = jnp.concatenate([jnp.zeros((n, half), F32), sin, pad], axis=1)
    s_dn = jnp.concatenate([-sin, jnp.zeros((n, half), F32), pad], axis=1)
    return jnp.stack([jnp.tile(c, (1, reps)), jnp.tile(s_up, (1, reps)), jnp.tile(s_dn, (1, reps))])


def _log_sigmoid(x):
    return -(jnp.maximum(-x, 0.0) + jnp.log1p(jnp.exp(-jnp.abs(x))))


def _inproj_kernel(x_ref, g_ref, w_ref, b_ref, ta_ref, tb_ref, *out_refs):
    hb = _rms(x_ref[...], g_ref[...]).astype(BF16)
    c0 = 0
    for (name, width, kinds), o_ref in zip(_SEGS, out_refs):
        z = _dot(hb, w_ref[:, c0:c0 + width]) + b_ref[:, c0:c0 + width]
        for k, kind in enumerate(kinds):
            zk = z[:, k * LANES:(k + 1) * LANES]
            if kind == "a":
                zk = (zk * ta_ref[0] + pltpu.roll(zk, ROT_DIM // 2, 1) * ta_ref[1]
                      + pltpu.roll(zk, LANES - ROT_DIM // 2, 1) * ta_ref[2])
            elif kind == "b":
                zk = (zk * tb_ref[0] + pltpu.roll(zk, DIFF_ROT // 2, 1) * tb_ref[1]
                      + pltpu.roll(zk, LANES - DIFF_ROT // 2, 1) * tb_ref[2])
            elif kind == "s":
                zk = jax.nn.sigmoid(zk)
            elif kind == "l":
                zk = _log_sigmoid(zk)
            o_ref[:, k * LANES:(k + 1) * LANES] = zk
        c0 += width


def _inproj(x, g, w_ext, b_ext, tab_a, tab_b, tm):
    n, d = x.shape
    npos = tab_a.shape[1]
    tm = min(tm, npos)
    nt = npos // tm
    ctot = w_ext.shape[1]
    return pl.pallas_call(
        _inproj_kernel,
        grid=(n // tm,),
        in_specs=[pl.BlockSpec((tm, d), lambda i: (i, 0)),
                  pl.BlockSpec((1, d), lambda i: (0, 0)),
                  pl.BlockSpec((d, ctot), lambda i: (0, 0)),
                  pl.BlockSpec((1, ctot), lambda i: (0, 0)),
                  pl.BlockSpec((3, tm, LANES), lambda i: (0, i % nt, 0)),
                  pl.BlockSpec((3, tm, LANES), lambda i: (0, i % nt, 0))],
        out_specs=[pl.BlockSpec((tm, wd), lambda i: (i, 0)) for _, wd, _ in _SEGS],
        out_shape=[jax.ShapeDtypeStruct((n, wd), F32) for _, wd, _ in _SEGS],
        compiler_params=_params(("parallel",)),
        name="inproj",
    )(x, g.reshape(1, d), w_ext, b_ext, tab_a, tab_b)


def _softmax_masked(s, mask):
    s = jnp.where(mask, s, NEG_INF)
    e = jnp.where(mask, jnp.exp(s - jnp.max(s, axis=-1, keepdims=True)), 0.0)
    return e / jnp.maximum(jnp.sum(e, axis=-1, keepdims=True), 1e-30)


def _online_update(s, mask, v, carry, vt=False):
    m, l, acc = carry
    r, tq, tk = s.shape
    s = jnp.where(mask, s, NEG_INF)
    m_new = jnp.maximum(m, jnp.max(s, axis=-1, keepdims=True))
    alpha = jnp.exp(m - m_new)
    p = jnp.where(mask, jnp.exp(s - m_new), 0.0)
    l = alpha * l + jnp.sum(p, axis=-1, keepdims=True)
    pb = p.reshape(r * tq, tk).astype(BF16)
    pv = (_dot_nt(pb, v) if vt else _dot(pb, v)).reshape(r, tq, LANES)
    return m_new, l, alpha * acc + pv


def _mask_bias(mask):
    return jnp.where(mask, 0.0, NEG_INF)


def _online_update_fast(s, v, carry, vt=False):
    m, l, acc = carry
    r, tq, tk = s.shape
    m_new = jnp.maximum(m, jnp.max(s, axis=-1, keepdims=True))
    alpha = jnp.exp(m - m_new)
    p = jnp.exp(s - m_new)
    l = alpha * l + jnp.sum(p, axis=-1, keepdims=True)
    pb = p.reshape(r * tq, tk).astype(BF16)
    pv = (_dot_nt(pb, v) if vt else _dot(pb, v)).reshape(r, tq, LANES)
    return m_new, l, alpha * acc + pv


def _online_init(r, tq, dv=LANES):
    return (jnp.full((r, tq, 1), NEG_INF, F32), jnp.zeros((r, tq, 1), F32), jnp.zeros((r, tq, dv), F32))


def _online_final(carry):
    _, l, acc = carry
    return acc / jnp.maximum(l, 1e-30)


def _stack_rows(ref, nblk, scale):
    parts = [ref[0, :, i * LANES:(i + 1) * LANES] for i in range(nblk)]
    return (jnp.concatenate(parts, axis=0) * scale).astype(BF16)


def _split3(hi):
    a = hi.astype(BF16)
    r1 = hi - a.astype(F32)
    b = r1.astype(BF16)
    c = (r1 - b.astype(F32)).astype(BF16)
    return a, b, c


def _compress_weights(w_ck, w_cv, pe_k, pe_v):
    d = HEAD_DIM
    wk = w_ck.reshape(CMP_BLOCK, d, d)
    wv = w_cv.reshape(CMP_BLOCK, d, d)
    wst = jnp.stack([wk, wk, wv, wv])
    eye = jnp.eye(4, dtype=F32)
    big = jnp.einsum("cpde,cf->pcdfe", wst, eye).reshape(CMP_BLOCK, 4 * d, 4 * d)
    w1 = big[:CMP_STRIDE].reshape(CMP_STRIDE * 4 * d, 4 * d).astype(BF16)
    w2 = big[CMP_STRIDE:].reshape(CMP_STRIDE * 4 * d, 4 * d).astype(BF16)
    pst = jnp.stack([pe_k, pe_k, pe_v, pe_v], axis=1)
    pe1 = jnp.broadcast_to(pst[:CMP_STRIDE].reshape(1, -1), (8, CMP_STRIDE * 4 * d)).astype(BF16)
    pe2 = jnp.broadcast_to(pst[CMP_STRIDE:].reshape(1, -1), (8, CMP_STRIDE * 4 * d)).astype(BF16)
    return w1, w2, pe1, pe2


def _compress_kernel(a_ref, w1_ref, w2_ref, pe1_ref, pe2_ref, f_ref, s_ref, b_ref):
    a = a_ref[0].astype(BF16)
    f_ref[0] = _dot(a, w1_ref[...])
    s_ref[0] = _dot(a, w2_ref[...])
    b_ref[...] = _dot(pe1_ref[...], w1_ref[...]) + _dot(pe2_ref[...], w2_ref[...])


def _compress(a, w1, w2, pe1, pe2):
    b, nseg, ka = a.shape
    ts = min(256, nseg)
    wspec = pl.BlockSpec((ka, 256), lambda i, j: (0, 0))
    pspec = pl.BlockSpec((8, ka), lambda i, j: (0, 0))
    ospec = pl.BlockSpec((1, ts, 256), lambda i, j: (i, j, 0))
    return pl.pallas_call(
        _compress_kernel,
        grid=(b, nseg // ts),
        in_specs=[pl.BlockSpec((1, ts, ka), lambda i, j: (i, j, 0)), wspec, wspec, pspec, pspec],
        out_specs=[ospec, ospec, pl.BlockSpec((8, 256), lambda i, j: (0, 0))],
        out_shape=[jax.ShapeDtypeStruct((b, nseg, 256), F32), jax.ShapeDtypeStruct((b, nseg, 256), F32),
                   jax.ShapeDtypeStruct((8, 256), F32)],
        compiler_params=_params(("arbitrary", "arbitrary")),
        name="compress",
    )(a, w1, w2, pe1, pe2)


def _select_blocks(psum, qpos, nb, sc_ref):
    tq, ncmp = psum.shape
    nbp = sc_ref.shape[0]
    n_i = lax.broadcasted_iota(jnp.int32, (ncmp, nbp), 0)
    j_i = lax.broadcasted_iota(jnp.int32, (ncmp, nbp), 1)
    dlt = n_i - (SEL_BLOCK // CMP_STRIDE) * j_i
    wmat = jnp.where((dlt == -1) | (dlt == 3), 1.0, jnp.where((dlt >= 0) & (dlt <= 2), 2.0, 0.0)).astype(BF16)
    p_hi = psum.astype(BF16)
    p_lo = (psum - p_hi.astype(F32)).astype(BF16)
    imp = _dot(p_hi, wmat) + _dot(p_lo, wmat)
    blk = lax.broadcasted_iota(jnp.int32, (tq, nbp), 1)
    cur = qpos // SEL_BLOCK
    forced = (blk == 0) | (blk == cur) | (blk == cur - 1)
    score = jnp.where(forced, FORCE_SCORE, jnp.where(blk <= cur, imp, NEG_INF))
    nbr = -(-nb // 8) * 8
    sc = score.T[0:nbr, :]
    rowf = lax.broadcasted_iota(jnp.int32, (nbr, tq), 0).astype(F32)

    def take_max(_, carry):
        s, sel = carry
        m = jnp.max(s, axis=0, keepdims=True)
        idx = jnp.min(jnp.where(s == m, rowf, float(nbr)), axis=0, keepdims=True)
        hit = rowf == idx
        return jnp.where(hit, -jnp.inf, s), jnp.where(hit, 1.0, sel)

    _, sel = lax.fori_loop(0, min(N_SEL, nb), take_max, (sc, jnp.zeros((nbr, tq), F32)))
    if nbr < nbp:
        sel = jnp.concatenate([sel, jnp.zeros((nbp - nbr, tq), F32)], axis=0)
    return sel.T


def _expand_blocks(sel, k0, tk, as_float=False):
    nbp = sel.shape[1]
    j_i = lax.broadcasted_iota(jnp.int32, (nbp, tk), 0)
    s_i = lax.broadcasted_iota(jnp.int32, (nbp, tk), 1) + k0
    e = jnp.where(lax.shift_right_logical(s_i, 6) == j_i, 1.0, 0.0).astype(BF16)
    keep = _dot(sel.astype(BF16), e)
    return keep if as_float else keep > 0.5


def _cmp_branch(qn, kcmp, vcmp, qpos, nseg, ncmp_valid, pos0=0):
    tq = qpos.shape[0]
    n_i = lax.broadcasted_iota(jnp.int32, (1, nseg), 1)
    m_c = ((n_i * CMP_STRIDE + (CMP_BLOCK - 1) + pos0) <= qpos) & (n_i < ncmp_valid)
    s_c = _dot_nt(qn, kcmp).reshape(NSA_GROUP, tq, nseg)
    p_c = _softmax_masked(s_c, m_c[None])
    o_c = _dot(p_c.reshape(NSA_GROUP * tq, nseg).astype(BF16), vcmp).reshape(NSA_GROUP, tq, LANES)
    return o_c, jnp.sum(p_c, axis=0)


def _nsa_prompt_kernel(qn_ref, qr_ref, f_ref, s_ref, cb_ref, sel_ref, win_ref, gate_ref, o_ref,
                       ks16, vs16, kw16, vw16, sc_ref, *, tq, tk, nseg, nb):
    qi = pl.program_id(1)
    q0 = qi * tq

    @pl.when(qi == 0)
    def _():
        ks16[...] = sel_ref[0, :, 0:LANES].astype(BF16)
        vs16[...] = sel_ref[0, :, LANES:2 * LANES].astype(BF16)
        kw16[...] = win_ref[0, :, 0:LANES].astype(BF16)
        vw16[...] = win_ref[0, :, LANES:2 * LANES].astype(BF16)

    scale = HEAD_DIM ** -0.5
    qpos = q0 + lax.broadcasted_iota(jnp.int32, (tq, 1), 0)
    kv = f_ref[0] + pltpu.roll(s_ref[0], nseg - 1, 0) + cb_ref[0:1, :]
    kcmp = kv[:, 0:LANES].astype(BF16)
    vcmp = kv[:, LANES:2 * LANES].astype(BF16)
    wspan = WINDOW + tq
    kstart = pl.multiple_of(jnp.maximum(q0 - WINDOW, 0), tq)
    nkt = (q0 + tq + tk - 1) // tk

    for g in range(NSA_KV_HEADS):
        qn = _stack_rows(qn_ref.at[:, :, g * NSA_GROUP * LANES:(g + 1) * NSA_GROUP * LANES], NSA_GROUP, scale)
        o_c, psum = _cmp_branch(qn, kcmp, vcmp, qpos, nseg, nseg - 1)
        sel = _select_blocks(psum, qpos, nb, sc_ref)
        qr = _stack_rows(qr_ref.at[:, :, g * NSA_GROUP * LANES:(g + 1) * NSA_GROUP * LANES], NSA_GROUP, scale)

        def body(kt, carry, qr=qr, sel=sel):
            k0 = pl.multiple_of(kt * tk, tk)
            s = _dot_nt(qr, ks16[pl.ds(k0, tk), :]).reshape(NSA_GROUP, tq, tk)
            kpos = k0 + lax.broadcasted_iota(jnp.int32, (1, tk), 1)
            bias = jnp.where(_expand_blocks(sel, k0, tk), _mask_bias(kpos <= qpos), NEG_INF)
            return _online_update_fast(s + bias[None], vs16[pl.ds(k0, tk), :], carry)

        o_s = _online_final(lax.fori_loop(0, nkt, body, _online_init(NSA_GROUP, tq)))

        kpos = kstart + lax.broadcasted_iota(jnp.int32, (1, wspan), 1)
        m_w = (kpos <= qpos) & (kpos > qpos - WINDOW)
        s_w = _dot_nt(qr, kw16[pl.ds(kstart, wspan), :]).reshape(NSA_GROUP, tq, wspan)
        p_w = _softmax_masked(s_w, m_w[None])
        o_w = _dot(p_w.reshape(NSA_GROUP * tq, wspan).astype(BF16), vw16[pl.ds(kstart, wspan), :])
        o_w = o_w.reshape(NSA_GROUP, tq, LANES)

        gates = gate_ref[0, :, g * LANES:(g + 1) * LANES]
        for i in range(NSA_GROUP):
            o = (gates[:, 3 * i:3 * i + 1] * o_c[i] + gates[:, 3 * i + 1:3 * i + 2] * o_s[i]
                 + gates[:, 3 * i + 2:3 * i + 3] * o_w[i])
            hblk = g * NSA_GROUP + i
            o_ref[0, :, hblk * LANES:(hblk + 1) * LANES] = o.astype(o_ref.dtype)


def _nsa_prompt(qn, qr, f, s, cb, sel, win, gate, tq=256, tk=512):
    b, t, _ = qn.shape
    nseg = f.shape[1]
    nb = -(-t // SEL_BLOCK)
    nbp = -(-nb // LANES) * LANES
    tk = min(tk, t)
    assert t % tk == 0 and t % tq == 0 and t >= WINDOW + tq
    qspec = pl.BlockSpec((1, tq, NSA_HEADS * LANES), lambda i, j: (i, j, 0))
    fspec = pl.BlockSpec((1, nseg, 256), lambda i, j: (i, 0, 0))
    kspec = pl.BlockSpec((1, t, 256), lambda i, j: (i, 0, 0))
    return pl.pallas_call(
        functools.partial(_nsa_prompt_kernel, tq=tq, tk=tk, nseg=nseg, nb=nb),
        grid=(b, t // tq),
        in_specs=[qspec, qspec, fspec, fspec, pl.BlockSpec((8, 256), lambda i, j: (0, 0)), kspec, kspec,
                  pl.BlockSpec((1, tq, 3 * LANES), lambda i, j: (i, j, 0))],
        out_specs=qspec,
        out_shape=jax.ShapeDtypeStruct((b, t, NSA_HEADS * LANES), BF16),
        scratch_shapes=[pltpu.VMEM((t, LANES), BF16)] * 4 + [pltpu.VMEM((nbp, tq), F32)],
        compiler_params=_params(("arbitrary", "arbitrary")),
        name="nsa_prompt",
    )(qn, qr, f, s, cb, sel, win, gate)


def _cumsum_kernel(lf_ref, col_ref, row_ref, carry_ref, *, tm):
    j = pl.program_id(1)

    @pl.when(j == 0)
    def _():
        carry_ref[...] = jnp.zeros_like(carry_ref)

    r_i = lax.broadcasted_iota(jnp.int32, (tm, tm), 0)
    c_i = lax.broadcasted_iota(jnp.int32, (tm, tm), 1)
    tri = jnp.where(c_i <= r_i, 1.0, 0.0).astype(BF16)
    a, b, c = _split3(lf_ref[0])
    cs = _dot(tri, a) + _dot(tri, b) + _dot(tri, c) + carry_ref[0:1, :]
    carry_ref[...] = jnp.broadcast_to(cs[tm - 1:tm, :], carry_ref.shape)
    sh = pltpu.roll(cs, LANES - 2, 1)
    col_ref[0, :, 0:LANES] = cs
    col_ref[0, :, LANES:2 * LANES] = sh
    row_ref[0, 0:8, :] = cs.T[0:8, :]
    row_ref[0, 8:16, :] = sh.T[0:8, :]


def _cumsum(gate, tm=512):
    b, t, _ = gate.shape
    tm = min(tm, t)
    return pl.pallas_call(
        functools.partial(_cumsum_kernel, tm=tm),
        grid=(b, t // tm),
        in_specs=[pl.BlockSpec((1, tm, LANES), lambda i, j: (i, j, 2))],
        out_specs=[pl.BlockSpec((1, tm, 2 * LANES), lambda i, j: (i, j, 0)),
                   pl.BlockSpec((1, 16, tm), lambda i, j: (i, 0, j))],
        out_shape=[jax.ShapeDtypeStruct((b, t, 2 * LANES), F32), jax.ShapeDtypeStruct((b, 16, t), F32)],
        scratch_shapes=[pltpu.VMEM((8, LANES), F32)],
        compiler_params=_params(("arbitrary", "arbitrary")),
        name="cumsum",
    )(gate)


def _diff_lambda(dl_ref, lam_init):
    dl = dl_ref[...]
    a = jnp.sum(dl[0:1] * dl[1:2], axis=-1, keepdims=True)
    b = jnp.sum(dl[2:3] * dl[3:4], axis=-1, keepdims=True)
    return jnp.exp(a) - jnp.exp(b) + lam_init


def _diff_finish(o, lam, gain_ref, lam_init, o_ref):
    lane = lax.broadcasted_iota(jnp.int32, (1, LANES), 1)
    for hh in range(2):
        w = o[2 * hh] - lam * o[2 * hh + 1]
        keep = jnp.where((lane >= hh * HEAD_DIM) & (lane < (hh + 1) * HEAD_DIM), 1.0, 0.0)
        w = w * keep
        ms = jnp.sum(w * w, axis=-1, keepdims=True) * (1.0 / HEAD_DIM)
        y = w * lax.rsqrt(ms + EPS) * gain_ref[...] * (1.0 - lam_init)
        o_ref[0, :, hh * LANES:(hh + 1) * LANES] = y.astype(o_ref.dtype)


def _diff_prompt_kernel(q_ref, k_ref, v_ref, dl_ref, gain_ref, o_ref, k16, v16, *, tq, tk, lam_init):
    qi = pl.program_id(2)
    q0 = qi * tq

    @pl.when(qi == 0)
    def _():
        k16[...] = k_ref[0].astype(BF16)
        v16[...] = v_ref[0].astype(BF16)

    qpos = q0 + lax.broadcasted_iota(jnp.int32, (tq, 1), 0)
    q = _stack_rows(q_ref, 4, DIFF_HALF ** -0.5)

    def body(kt, carry):
        k0 = pl.multiple_of(kt * tk, tk)
        s = _dot_nt(q, k16[pl.ds(k0, tk), :]).reshape(4, tq, tk)
        kpos = k0 + lax.broadcasted_iota(jnp.int32, (1, tk), 1)
        return _online_update_fast(s + _mask_bias(kpos <= qpos)[None], v16[pl.ds(k0, tk), :], carry)

    o = _online_final(lax.fori_loop(0, (q0 + tq + tk - 1) // tk, body, _online_init(4, tq)))
    _diff_finish(o, _diff_lambda(dl_ref, lam_init), gain_ref, lam_init, o_ref)


def _gain_lanes(gain):
    return jnp.tile(gain.reshape(1, HEAD_DIM), (1, LANES // HEAD_DIM))


def _diff_prompt(dq, rows, dl, gain, lam_init, tq=256, tk=512):
    b, t, _ = dq.shape
    tk = min(tk, t)
    return pl.pallas_call(
        functools.partial(_diff_prompt_kernel, tq=tq, tk=tk, lam_init=lam_init),
        grid=(b, 2, t // tq),
        in_specs=[pl.BlockSpec((1, tq, 4 * LANES), lambda i, kb, j: (i, j, kb)),
                  pl.BlockSpec((1, t, LANES), lambda i, kb, j: (i, 0, kb)),
                  pl.BlockSpec((1, t, LANES), lambda i, kb, j: (i, 0, 2 + kb)),
                  pl.BlockSpec((4, DIFF_HALF), lambda i, kb, j: (0, 0)),
                  pl.BlockSpec((1, LANES), lambda i, kb, j: (0, 0))],
        out_specs=pl.BlockSpec((1, tq, 2 * LANES), lambda i, kb, j: (i, j, kb)),
        out_shape=jax.ShapeDtypeStruct((b, t, 4 * LANES), BF16),
        scratch_shapes=[pltpu.VMEM((t, LANES), BF16)] * 2,
        compiler_params=_params(("arbitrary", "arbitrary", "arbitrary")),
        name="diff_prompt",
    )(dq, rows, rows, dl, _gain_lanes(gain))


def _fox_prompt_kernel(q_ref, k_ref, v_ref, cc_ref, cr_ref, o_ref, k16, v16, *, tq, tk):
    qi = pl.program_id(2)
    q0 = qi * tq

    @pl.when(qi == 0)
    def _():
        k16[...] = k_ref[0].astype(BF16)
        v16[...] = v_ref[0].astype(BF16)

    qpos = q0 + lax.broadcasted_iota(jnp.int32, (tq, 1), 0)
    q = _stack_rows(q_ref, 2, HEAD_DIM ** -0.5)
    cq = jnp.stack([cc_ref[0, :, 0:1], cc_ref[0, :, 1:2]])

    def body(kt, carry):
        k0 = pl.multiple_of(kt * tk, tk)
        ck = cr_ref[0, 0:2, pl.ds(k0, tk)][:, None, :]
        kpos = k0 + lax.broadcasted_iota(jnp.int32, (1, tk), 1)
        s = _dot_nt(q, k16[pl.ds(k0, tk), :]).reshape(2, tq, tk) + cq - ck + _mask_bias(kpos <= qpos)[None]
        return _online_update_fast(s, v16[pl.ds(k0, tk), :], carry)

    o = _online_final(lax.fori_loop(0, (q0 + tq + tk - 1) // tk, body, _online_init(2, tq)))
    for hh in range(2):
        o_ref[0, :, hh * LANES:(hh + 1) * LANES] = o[hh].astype(o_ref.dtype)


def _fox_prompt(fq, rows, ccol, crow, tq=256, tk=512):
    b, t, _ = fq.shape
    tk = min(tk, t)
    tq = min(tq, t)
    return pl.pallas_call(
        functools.partial(_fox_prompt_kernel, tq=tq, tk=tk),
        grid=(b, 2, t // tq),
        in_specs=[pl.BlockSpec((1, tq, 2 * LANES), lambda i, kb, j: (i, j, kb)),
                  pl.BlockSpec((1, t, LANES), lambda i, kb, j: (i, 0, kb)),
                  pl.BlockSpec((1, t, LANES), lambda i, kb, j: (i, 0, 2 + kb)),
                  pl.BlockSpec((1, tq, LANES), lambda i, kb, j: (i, j, kb)),
                  pl.BlockSpec((1, 8, t), lambda i, kb, j: (i, kb, 0))],
        out_specs=pl.BlockSpec((1, tq, 2 * LANES), lambda i, kb, j: (i, j, kb)),
        out_shape=jax.ShapeDtypeStruct((b, t, 4 * LANES), BF16),
        scratch_shapes=[pltpu.VMEM((t, LANES), BF16)] * 2,
        compiler_params=_params(("arbitrary", "arbitrary", "arbitrary")),
        name="fox_prompt",
    )(fq, rows, rows, ccol, crow)


def _mem_attn_kernel(q_ref, kv_ref, o_ref, *, dh):
    kv = kv_ref[0].astype(BF16)
    nh = q_ref.shape[-1] // dh
    for h in range(nh):
        q = (q_ref[0, :, h * dh:(h + 1) * dh] * dh ** -0.5).astype(BF16)
        s = _dot_nt(q, kv[:, h * dh:(h + 1) * dh])
        e = jnp.exp(s - jnp.max(s, axis=-1, keepdims=True))
        p = e / jnp.sum(e, axis=-1, keepdims=True)
        o = _dot(p.astype(BF16), kv[:, (nh + h) * dh:(nh + h + 1) * dh])
        o_ref[0, :, h * dh:(h + 1) * dh] = o.astype(o_ref.dtype)


def _mem_attn(q, kv, tq=512):
    b, t, d = q.shape
    m = kv.shape[1]
    tq = min(tq, t)
    return pl.pallas_call(
        functools.partial(_mem_attn_kernel, dh=d // MEM_HEADS),
        grid=(b, t // tq),
        in_specs=[pl.BlockSpec((1, tq, d), lambda i, j: (i, j, 0)),
                  pl.BlockSpec((1, m, 2 * d), lambda i, j: (i, 0, 0))],
        out_specs=pl.BlockSpec((1, tq, d), lambda i, j: (i, j, 0)),
        out_shape=jax.ShapeDtypeStruct((b, t, d), BF16),
        compiler_params=_params(("parallel", "arbitrary")),
        name="mem_attn",
    )(q, kv)


def _final_norm_kernel(x_ref, g_ref, o_ref):
    o_ref[...] = _rms(x_ref[...], g_ref[...])


def _final_norm(x, g):
    n, d = x.shape
    tm = min(512, n)
    return pl.pallas_call(
        _final_norm_kernel,
        grid=(n // tm,),
        in_specs=[pl.BlockSpec((tm, d), lambda i: (i, 0)), pl.BlockSpec((1, d), lambda i: (0, 0))],
        out_specs=pl.BlockSpec((tm, d), lambda i: (i, 0)),
        out_shape=jax.ShapeDtypeStruct((n, d), F32),
        compiler_params=_params(("parallel",)),
        name="final_norm",
    )(x, g.reshape(1, d))


PAGES = 32
PAGES_CMP = 32
PAGES_LOGF = 64
TP = 8


def _pages_per_step(n_pages, want):
    npg = min(want, n_pages)
    while n_pages % npg:
        npg //= 2
    return npg


def _page_specs(block, npg, layer):
    def imap(i, j, pt, k):
        return (layer, pt[i, j * npg + k]) + (0,) * len(block)
    return [pl.BlockSpec((1, 1) + block, functools.partial(imap, k=k)) for k in range(npg)]


def _feature_major(cache):
    nd = cache.ndim
    t = jnp.transpose(cache, (0, 1) + tuple(range(3, nd)) + (2,))
    return t.reshape(t.shape[0], t.shape[1], -1, t.shape[-1])


def _paged_call(kern, pt, layer, pools, pool_block, others, other_specs, out_specs, out_shape, scratch, name,
                pages=PAGES):
    b, n_pages = pt.shape
    npg = _pages_per_step(n_pages, pages)
    in_specs = []
    args = []
    for pool in pools:
        in_specs += _page_specs(pool_block, npg, layer)
        args += [pool] * npg
    in_specs += other_specs
    args += others
    gs = pltpu.PrefetchScalarGridSpec(num_scalar_prefetch=1, grid=(b, n_pages // npg), in_specs=in_specs,
                                      out_specs=out_specs, scratch_shapes=scratch)
    return pl.pallas_call(functools.partial(kern, npg=npg), grid_spec=gs, out_shape=out_shape,
                          compiler_params=_params(("arbitrary", "arbitrary")), name=name)(pt, *args)


def _cumsum_paged_kernel(pt_ref, *refs, npg):
    pages = refs[:npg]
    row_ref, carry_ref = refs[npg], refs[npg + 1]
    j = pl.program_id(1)

    @pl.when(j == 0)
    def _():
        carry_ref[...] = jnp.zeros_like(carry_ref)

    psz = pages[0].shape[-1]
    r_i = lax.broadcasted_iota(jnp.int32, (psz, psz), 0)
    c_i = lax.broadcasted_iota(jnp.int32, (psz, psz), 1)
    tri = jnp.where(r_i <= c_i, 1.0, 0.0).astype(BF16)
    rows = 8 * npg
    zpad = jnp.zeros((8 - FOX_HEADS, psz), F32)
    x = jnp.concatenate([blk for k in range(npg) for blk in (pages[k][0, 0], zpad)], axis=0)
    a, b, c = _split3(x)
    local = _dot(a, tri) + _dot(b, tri) + _dot(c, tri)
    tot = jnp.broadcast_to(local[:, psz - 1:psz], (rows, psz))
    p_r = lax.broadcasted_iota(jnp.int32, (rows, rows), 0)
    p_c = lax.broadcasted_iota(jnp.int32, (rows, rows), 1)
    earlier = jnp.where((p_c < p_r) & ((p_r - p_c) % 8 == 0), 1.0, 0.0).astype(BF16)
    ta, tb, tc = _split3(tot)
    offs = _dot(earlier, ta) + _dot(earlier, tb) + _dot(earlier, tc)
    cs = local + offs + jnp.tile(carry_ref[...], (npg, 1))
    for k in range(npg):
        row_ref[0, :, k * psz:(k + 1) * psz] = cs[8 * k:8 * (k + 1), :]
    carry_ref[...] = jnp.broadcast_to(cs[rows - 8:rows, psz - 1:psz], carry_ref.shape)


def _cumsum_paged(pt, layer, logf_pool_t):
    b, n_pages = pt.shape
    psz = logf_pool_t.shape[-1]
    npg = _pages_per_step(n_pages, PAGES_LOGF)
    return _paged_call(
        _cumsum_paged_kernel, pt, layer, [logf_pool_t], (FOX_HEADS, psz), [], [],
        pl.BlockSpec((1, 8, npg * psz), lambda i, j, pt: (i, 0, j)),
        jax.ShapeDtypeStruct((b, 8, n_pages * psz), F32),
        [pltpu.VMEM((8, psz), F32)], "cumsum_paged", pages=PAGES_LOGF)


def _compress_paged_kernel(pt_ref, *refs, npg):
    pages = refs[:npg]
    w1_ref, w2_ref, f_ref, s_ref, a_ref = refs[npg:npg + 5]
    psz = pages[0].shape[-1]
    spp = psz // CMP_STRIDE
    r_i = lax.broadcasted_iota(jnp.int32, (psz, psz), 0)
    c_i = lax.broadcasted_iota(jnp.int32, (psz, psz), 1)
    perm = jnp.where(c_i == CMP_STRIDE * (r_i % spp) + r_i // spp, 1.0, 0.0).astype(BF16)
    for k in range(npg):
        xp = _dot_nt(perm, pages[k][0, 0].astype(BF16)).astype(BF16)
        for p in range(CMP_STRIDE):
            a_ref[k * spp:(k + 1) * spp, p * 256:(p + 1) * 256] = xp[p * spp:(p + 1) * spp, :]
    a = a_ref[...]
    f_ref[0] = _dot(a, w1_ref[...])
    s_ref[0] = _dot(a, w2_ref[...])


def _compress_paged(pt, layer, pool, w1, w2):
    b, n_pages = pt.shape
    psz = pool.shape[-1]
    spp = psz // CMP_STRIDE
    npg = _pages_per_step(n_pages, PAGES_CMP)
    nseg = n_pages * spp
    wspec = pl.BlockSpec(w1.shape, lambda i, j, pt: (0, 0))
    ospec = pl.BlockSpec((1, npg * spp, 256), lambda i, j, pt: (i, j, 0))
    return _paged_call(
        _compress_paged_kernel, pt, layer, [pool], (256, psz), [w1, w2], [wspec, wspec],
        [ospec, ospec],
        [jax.ShapeDtypeStruct((b, nseg, 256), F32)] * 2,
        [pltpu.VMEM((npg * spp, CMP_STRIDE * 256), BF16)], "compress_paged", pages=PAGES_CMP)


def _nsa_sample_cmp_kernel(qn_ref, f_ref, s_ref, cb_ref, oc_ref, ps_ref, *, nseg, pos0, n_new):
    scale = HEAD_DIM ** -0.5
    tq = qn_ref.shape[1]
    qpos = pos0 + jnp.minimum(lax.broadcasted_iota(jnp.int32, (tq, 1), 0), n_new - 1)
    kv = f_ref[0] + pltpu.roll(s_ref[0], nseg - 1, 0) + cb_ref[0:1, :]
    kcmp = kv[:, 0:LANES].astype(BF16)
    vcmp = kv[:, LANES:2 * LANES].astype(BF16)
    for g in range(NSA_KV_HEADS):
        qn = _stack_rows(qn_ref.at[:, :, g * NSA_GROUP * LANES:(g + 1) * NSA_GROUP * LANES], NSA_GROUP, scale)
        o_c, psum = _cmp_branch(qn, kcmp, vcmp, qpos, nseg, nseg - 1)
        for i in range(NSA_GROUP):
            hblk = g * NSA_GROUP + i
            oc_ref[0, :, hblk * LANES:(hblk + 1) * LANES] = o_c[i]
        ps_ref[0, g] = psum


def _nsa_sample_cmp(qn, f, s, cb, pos0, n_new):
    b, tq, _ = qn.shape
    nseg = f.shape[1]
    fspec = pl.BlockSpec((1, nseg, 256), lambda i: (i, 0, 0))
    return pl.pallas_call(
        functools.partial(_nsa_sample_cmp_kernel, nseg=nseg, pos0=pos0, n_new=n_new),
        grid=(b,),
        in_specs=[pl.BlockSpec((1, tq, NSA_HEADS * LANES), lambda i: (i, 0, 0)), fspec, fspec,
                  pl.BlockSpec((8, 256), lambda i: (0, 0))],
        out_specs=[pl.BlockSpec((1, tq, NSA_HEADS * LANES), lambda i: (i, 0, 0)),
                   pl.BlockSpec((1, NSA_KV_HEADS, tq, nseg), lambda i: (i, 0, 0, 0))],
        out_shape=[jax.ShapeDtypeStruct((b, tq, NSA_HEADS * LANES), F32),
                   jax.ShapeDtypeStruct((b, NSA_KV_HEADS, tq, nseg), F32)],
        compiler_params=_params(("parallel",)),
        name="nsa_sample_cmp",
    )(qn, f, s, cb)


def _select_rows_kernel(ps_ref, sel_ref, sc_ref, *, nb, pos0, tq, n_new):
    rows = ps_ref.shape[0]
    qpos = pos0 + jnp.minimum(lax.broadcasted_iota(jnp.int32, (rows, 1), 0) % tq, n_new - 1)
    sel_ref[...] = _select_blocks(ps_ref[...], qpos, nb, sc_ref)


def _select_rows(psum, nb, pos0, tq, n_new):
    rows, ncmp = psum.shape
    nbp = -(-nb // LANES) * LANES
    return pl.pallas_call(
        functools.partial(_select_rows_kernel, nb=nb, pos0=pos0, tq=tq, n_new=n_new),
        grid=(1,),
        in_specs=[pl.BlockSpec((rows, ncmp), lambda i: (0, 0))],
        out_specs=pl.BlockSpec((rows, nbp), lambda i: (0, 0)),
        out_shape=jax.ShapeDtypeStruct((rows, nbp), F32),
        scratch_shapes=[pltpu.VMEM((nbp, rows), F32)],
        compiler_params=_params(("arbitrary",)),
        name="select_rows",
    )(psum)


def _state_update(st_refs, g, carry_fn):
    m_ref, l_ref, acc_ref = st_refs
    m, l, acc = carry_fn((m_ref[g], l_ref[g], acc_ref[g]))
    m_ref[g] = m
    l_ref[g] = l
    acc_ref[g] = acc


def _state_init(st_refs):
    m_ref, l_ref, acc_ref = st_refs
    m_ref[...] = jnp.full(m_ref.shape, NEG_INF, F32)
    l_ref[...] = jnp.zeros(l_ref.shape, F32)
    acc_ref[...] = jnp.zeros(acc_ref.shape, F32)


def _state_scratch(groups, r, tq):
    return [pltpu.VMEM((groups, r, tq, 1), F32), pltpu.VMEM((groups, r, tq, 1), F32),
            pltpu.VMEM((groups, r, tq, LANES), F32)]


def _nsa_sample_kernel(pt_ref, *refs, npg, pos0, n_new):
    pages = refs[:npg]
    (qr_ref, sel_ref, new_ref, wst_ref, wnew_ref, oc_ref, gate_ref, o_ref,
     m_ref, l_ref, acc_ref, k16, v16) = refs[npg:]
    st = (m_ref, l_ref, acc_ref)
    j = pl.program_id(1)
    tq = qr_ref.shape[1]
    psz = pages[0].shape[-1]
    tk = npg * psz
    scale = HEAD_DIM ** -0.5
    qpos = pos0 + jnp.minimum(lax.broadcasted_iota(jnp.int32, (tq, 1), 0), n_new - 1)

    @pl.when(j == 0)
    def _():
        _state_init(st)

    for k in range(npg):
        k16[:, k * psz:(k + 1) * psz] = pages[k][0, 0, 0:LANES, :].astype(BF16)
        v16[:, k * psz:(k + 1) * psz] = pages[k][0, 0, LANES:2 * LANES, :].astype(BF16)
    k0 = j * tk
    kpos = k0 + lax.broadcasted_iota(jnp.int32, (1, tk), 1)
    qrs = []
    keep = _expand_blocks(jnp.concatenate([sel_ref[0, g] for g in range(NSA_KV_HEADS)], axis=0), k0, tk,
                          as_float=True)
    causal = _mask_bias(kpos <= qpos)
    for g in range(NSA_KV_HEADS):
        qr = _stack_rows(qr_ref.at[:, :, g * NSA_GROUP * LANES:(g + 1) * NSA_GROUP * LANES], NSA_GROUP, scale)
        qrs.append(qr)
        bias = jnp.where(keep[g * tq:(g + 1) * tq] > 0.5, causal, NEG_INF)
        s = _dot(qr, k16[...]).reshape(NSA_GROUP, tq, tk) + bias[None]
        _state_update(st, g, functools.partial(_online_update_fast, s, v16[...], vt=True))

    @pl.when(j == pl.num_programs(1) - 1)
    def _():
        nrow = new_ref.shape[1]
        r_i = lax.broadcasted_iota(jnp.int32, (1, nrow), 1)
        npos = pos0 + r_i
        m_new = (r_i < n_new) & (npos <= qpos)
        wlen = wst_ref.shape[-1]
        wpos = pos0 - wlen + lax.broadcasted_iota(jnp.int32, (1, wlen), 1)
        m_old = (wpos > qpos - WINDOW) & (wpos >= 0)
        m_wnew = m_new & (npos > qpos - WINDOW)
        kn = new_ref[0, :, 0:LANES].astype(BF16)
        vn = new_ref[0, :, LANES:2 * LANES].astype(BF16)
        kwo = wst_ref[0, 0, 0:LANES, :].astype(BF16)
        vwo = wst_ref[0, 0, LANES:2 * LANES, :].astype(BF16)
        kwn = wnew_ref[0, :, 0:LANES].astype(BF16)
        vwn = wnew_ref[0, :, LANES:2 * LANES].astype(BF16)
        for g in range(NSA_KV_HEADS):
            qr = qrs[g]
            blk_ok = _expand_blocks(sel_ref[0, g], pos0, nrow)
            s = _dot_nt(qr, kn).reshape(NSA_GROUP, tq, nrow)
            _state_update(st, g, functools.partial(_online_update, s, (m_new & blk_ok)[None], vn))
            o_s = _online_final((m_ref[g], l_ref[g], acc_ref[g]))
            cw = _online_init(NSA_GROUP, tq)
            cw = _online_update(_dot(qr, kwo).reshape(NSA_GROUP, tq, wlen), m_old[None], vwo, cw, vt=True)
            cw = _online_update(_dot_nt(qr, kwn).reshape(NSA_GROUP, tq, nrow), m_wnew[None], vwn, cw)
            o_w = _online_final(cw)
            gates = gate_ref[0, :, g * LANES:(g + 1) * LANES]
            for i in range(NSA_GROUP):
                hblk = g * NSA_GROUP + i
                o_c = oc_ref[0, :, hblk * LANES:(hblk + 1) * LANES]
                o = (gates[:, 3 * i:3 * i + 1] * o_c + gates[:, 3 * i + 1:3 * i + 2] * o_s[i]
                     + gates[:, 3 * i + 2:3 * i + 3] * o_w[i])
                o_ref[0, :, hblk * LANES:(hblk + 1) * LANES] = o.astype(o_ref.dtype)


def _nsa_sample(pt, layer, pool, qr, sel, new, wst, wnew, oc, gate, pos0, n_new):
    b, n_pages = pt.shape
    psz = pool.shape[-1]
    npg = _pages_per_step(n_pages, PAGES)
    tq = qr.shape[1]
    nbp = sel.shape[-1]
    wlen = wst.shape[-1]
    full = lambda shp: pl.BlockSpec((1,) + shp, lambda i, j, pt: (i,) + (0,) * len(shp))
    return _paged_call(
        functools.partial(_nsa_sample_kernel, pos0=pos0, n_new=n_new), pt, layer, [pool], (256, psz),
        [qr, sel, new, wst, wnew, oc, gate],
        [full((tq, NSA_HEADS * LANES)), full((NSA_KV_HEADS, tq, nbp)), full((tq, 256)),
         pl.BlockSpec((1, 1, 256, wlen), lambda i, j, pt: (layer, i, 0, 0)),
         full((tq, 256)), full((tq, NSA_HEADS * LANES)), full((tq, 3 * LANES))],
        full((tq, NSA_HEADS * LANES)),
        jax.ShapeDtypeStruct((b, tq, NSA_HEADS * LANES), BF16),
        _state_scratch(NSA_KV_HEADS, NSA_GROUP, tq)
        + [pltpu.VMEM((LANES, npg * psz), BF16)] * 2, "nsa_sample", pages=PAGES)


def _diff_sample_kernel(pt_ref, *refs, npg, pos0, n_new, lam_init):
    pages = refs[:npg]
    q_ref, new_ref, dl_ref, gain_ref, o_ref, m_ref, l_ref, acc_ref, k16, v16 = refs[npg:]
    st = (m_ref, l_ref, acc_ref)
    j = pl.program_id(1)
    tq = q_ref.shape[1]
    psz = pages[0].shape[-1]
    tk = npg * psz
    qpos = pos0 + jnp.minimum(lax.broadcasted_iota(jnp.int32, (tq, 1), 0), n_new - 1)

    @pl.when(j == 0)
    def _():
        _state_init(st)

    kpos = j * tk + lax.broadcasted_iota(jnp.int32, (1, tk), 1)
    bias = _mask_bias(kpos <= qpos)[None]
    qs = []
    for kb in range(2):
        for k in range(npg):
            k16[kb, :, k * psz:(k + 1) * psz] = pages[k][0, 0, kb * LANES:(kb + 1) * LANES, :].astype(BF16)
            v16[kb, :, k * psz:(k + 1) * psz] = pages[k][0, 0, (2 + kb) * LANES:(3 + kb) * LANES, :].astype(BF16)
        q = _stack_rows(q_ref.at[:, :, kb * 4 * LANES:(kb + 1) * 4 * LANES], 4, DIFF_HALF ** -0.5)
        qs.append(q)
        s = _dot(q, k16[kb]).reshape(4, tq, tk) + bias
        _state_update(st, kb, functools.partial(_online_update_fast, s, v16[kb], vt=True))

    @pl.when(j == pl.num_programs(1) - 1)
    def _():
        nrow = new_ref.shape[1]
        r_i = lax.broadcasted_iota(jnp.int32, (1, nrow), 1)
        m_new = ((r_i < n_new) & (pos0 + r_i <= qpos))[None]
        lam = _diff_lambda(dl_ref, lam_init)
        for kb in range(2):
            kn = new_ref[0, :, kb * LANES:(kb + 1) * LANES].astype(BF16)
            vn = new_ref[0, :, (2 + kb) * LANES:(3 + kb) * LANES].astype(BF16)
            s = _dot_nt(qs[kb], kn).reshape(4, tq, nrow)
            _state_update(st, kb, functools.partial(_online_update, s, m_new, vn))
            o = _online_final((m_ref[kb], l_ref[kb], acc_ref[kb]))
            _diff_finish(o, lam, gain_ref, lam_init, o_ref.at[:, :, kb * 2 * LANES:(kb + 1) * 2 * LANES])


def _diff_sample(pt, layer, pool, dq, new, dl, gain, lam_init, pos0, n_new):
    b, n_pages = pt.shape
    psz = pool.shape[-1]
    npg = _pages_per_step(n_pages, PAGES)
    tq = dq.shape[1]
    full = lambda shp: pl.BlockSpec((1,) + shp, lambda i, j, pt: (i,) + (0,) * len(shp))
    const = lambda shp: pl.BlockSpec(shp, lambda i, j, pt: (0,) * len(shp))
    return _paged_call(
        functools.partial(_diff_sample_kernel, pos0=pos0, n_new=n_new, lam_init=lam_init), pt, layer, [pool],
        (512, psz), [dq, new, dl, _gain_lanes(gain)],
        [full((tq, 8 * LANES)), full((tq, 512)), const((4, DIFF_HALF)), const((1, LANES))],
        full((tq, 4 * LANES)),
        jax.ShapeDtypeStruct((b, tq, 4 * LANES), BF16),
        _state_scratch(2, 4, tq) + [pltpu.VMEM((2, LANES, npg * psz), BF16)] * 2, "diff_sample", pages=PAGES)


def _fox_sample_kernel(pt_ref, *refs, npg, pos0, n_new):
    pages = refs[:npg]
    q_ref, new_ref, cr_ref, tot_ref, lf_ref, o_ref, m_ref, l_ref, acc_ref, k16, v16 = refs[npg:]
    st = (m_ref, l_ref, acc_ref)
    j = pl.program_id(1)
    tq = q_ref.shape[1]
    psz = pages[0].shape[-1]
    tk = npg * psz
    qpos = pos0 + jnp.minimum(lax.broadcasted_iota(jnp.int32, (tq, 1), 0), n_new - 1)

    @pl.when(j == 0)
    def _():
        _state_init(st)

    lf = lf_ref[0]
    row = lax.broadcasted_iota(jnp.int32, (tq, 1), 0)
    cnew = jnp.zeros_like(lf)
    for t in range(n_new):
        cnew = cnew + jnp.where(row >= t, lf[t:t + 1, :], 0.0)
    psz_l = tot_ref.shape[-1]
    kpos = j * tk + lax.broadcasted_iota(jnp.int32, (1, tk), 1)
    bias = _mask_bias(kpos <= qpos)[None]
    qs = []
    for kb in range(2):
        for k in range(npg):
            k16[kb, :, k * psz:(k + 1) * psz] = pages[k][0, 0, kb * LANES:(kb + 1) * LANES, :].astype(BF16)
            v16[kb, :, k * psz:(k + 1) * psz] = pages[k][0, 0, (2 + kb) * LANES:(3 + kb) * LANES, :].astype(BF16)
        q = _stack_rows(q_ref.at[:, :, kb * 2 * LANES:(kb + 1) * 2 * LANES], 2, HEAD_DIM ** -0.5)
        qs.append(q)
        cq = jnp.stack([tot_ref[0, 2 * kb + hh:2 * kb + hh + 1, psz_l - 1:psz_l]
                        + cnew[:, 2 * kb + hh:2 * kb + hh + 1] for hh in range(2)])
        ck = cr_ref[0, 2 * kb:2 * kb + 2, :][:, None, :]
        s = _dot(q, k16[kb]).reshape(2, tq, tk) + cq - ck + bias
        _state_update(st, kb, functools.partial(_online_update_fast, s, v16[kb], vt=True))

    @pl.when(j == pl.num_programs(1) - 1)
    def _():
        nrow = new_ref.shape[1]
        r_i = lax.broadcasted_iota(jnp.int32, (1, nrow), 1)
        m_new = ((r_i < n_new) & (pos0 + r_i <= qpos))[None]
        for kb in range(2):
            kn = new_ref[0, :, kb * LANES:(kb + 1) * LANES].astype(BF16)
            vn = new_ref[0, :, (2 + kb) * LANES:(3 + kb) * LANES].astype(BF16)
            bias = []
            for hh in range(2):
                h = 2 * kb + hh
                d = jnp.zeros((tq, nrow), F32)
                for t in range(n_new):
                    d = d + jnp.where((row >= t) & (r_i < t), lf[t:t + 1, h:h + 1], 0.0)
                bias.append(d)
            s = _dot_nt(qs[kb], kn).reshape(2, tq, nrow) + jnp.stack(bias)
            _state_update(st, kb, functools.partial(_online_update, s, m_new, vn))
            o = _online_final((m_ref[kb], l_ref[kb], acc_ref[kb]))
            for hh in range(2):
                hblk = 2 * kb + hh
                o_ref[0, :, hblk * LANES:(hblk + 1) * LANES] = o[hh].astype(o_ref.dtype)


def _fox_sample(pt, layer, pool, fq, new, crow, lf, pos0, n_new):
    b, n_pages = pt.shape
    psz = pool.shape[-1]
    npg = _pages_per_step(n_pages, PAGES)
    tq = fq.shape[1]
    past = crow.shape[-1]
    full = lambda shp: pl.BlockSpec((1,) + shp, lambda i, j, pt: (i,) + (0,) * len(shp))
    return _paged_call(
        functools.partial(_fox_sample_kernel, pos0=pos0, n_new=n_new), pt, layer, [pool], (512, psz),
        [fq, new, crow, crow, lf],
        [full((tq, 4 * LANES)), full((tq, 512)),
         pl.BlockSpec((1, 8, npg * psz), lambda i, j, pt: (i, 0, j)),
         pl.BlockSpec((1, 8, LANES), lambda i, j, pt: (i, 0, past // LANES - 1)),
         pl.BlockSpec((1, tq, LANES), lambda i, j, pt: (i, 0, 2))],
        full((tq, 4 * LANES)),
        jax.ShapeDtypeStruct((b, tq, 4 * LANES), BF16),
        _state_scratch(2, 2, tq) + [pltpu.VMEM((2, LANES, npg * psz), BF16)] * 2, "fox_sample", pages=PAGES)


def _wo_rows():
    import numpy as np
    rows = []
    for h in range(NSA_HEADS):
        r = np.full(LANES, -1, np.int64)
        g = h // NSA_GROUP
        r[64 * g:64 * g + 64] = 64 * h + np.arange(64)
        rows.append(r)
    for base, nh in ((NSA_HEADS * HEAD_DIM, DIFF_HEADS), ((NSA_HEADS + DIFF_HEADS) * HEAD_DIM, FOX_HEADS)):
        for h in range(nh):
            r = np.full(LANES, -1, np.int64)
            r[64 * (h % 2):64 * (h % 2) + 64] = base + 64 * h + np.arange(64)
            rows.append(r)
    return np.concatenate(rows)


def _extend_wo(w):
    import numpy as np
    idx = _wo_rows()
    return jnp.where(jnp.asarray(idx >= 0)[:, None], w[jnp.asarray(np.maximum(idx, 0))], 0.0).astype(BF16)


def _mem_and_peer(x2, bsz, kv, lw):
    n, d = x2.shape
    q = _linear(x2, lw["w_mq"], gain=lw["g_mem"])
    t = n // bsz
    if t < TP:
        qp = jnp.pad(q.reshape(bsz, t, d), ((0, 0), (0, TP - t), (0, 0)))
        om = _mem_attn(qp, kv)[:, :t]
    else:
        om = _mem_attn(q.reshape(bsz, t, d), kv)
    x2 = _linear(om.reshape(n, d), lw["w_mo"], res=x2)
    return _peer(x2, lw["g_ffn"], lw["peer_wq"], lw["peer_keys"], lw["peer_u"], lw["peer_vt"])


def _prompt_layer(x2, bsz, mem_prompt, lw, tabs):
    n, d = x2.shape
    t = n // bsz
    outs = _inproj(x2, lw["g_attn"], lw["w_ext"], lw["b_ext"], tabs[0], tabs[1], 256)
    o = {name: v.reshape(bsz, t, -1) for (name, _, _), v in zip(_SEGS, outs)}
    f, s, cb = _compress(o["cmp"].reshape(bsz, t // CMP_STRIDE, CMP_STRIDE * 256), *lw["cmp_w"])
    o_nsa = _nsa_prompt(o["qn"], o["qr"], f, s, cb, o["sel"], o["win"], o["gate"])
    o_diff = _diff_prompt(o["dq"], o["diff"], lw["diff_lambda"], lw["diff_gain"], lw["lam_init"])
    ccol, crow = _cumsum(o["gate"])
    o_fox = _fox_prompt(o["fq"], o["fox"], ccol, crow)
    x2 = _linear_parts([o_nsa.reshape(n, -1), o_diff.reshape(n, -1), o_fox.reshape(n, -1)], lw["w_o_ext"], x2)
    m = mem_prompt.shape[1]
    mkv = _linear(mem_prompt.reshape(bsz * m, d), lw["w_mkv"])
    x2 = _mem_and_peer(x2, bsz, mkv.reshape(bsz, m, 2 * d), lw)
    return x2, o, mkv, cb


def _sample_layer(x2, bsz, pt, layer, pools, wstate, mem_kv, lw, tabs, cb, pos0):
    n, d = x2.shape
    t = n // bsz
    outs = _inproj(x2, lw["g_attn"], lw["w_ext"], lw["b_ext"], tabs[0], tabs[1], n)
    o = {name: v.reshape(bsz, t, -1) for (name, _, _), v in zip(_SEGS, outs)}
    op = {name: jnp.pad(v, ((0, 0), (0, TP - t), (0, 0))) for name, v in o.items()}
    cmp_pool, sel_pool, diff_pool, fox_pool, logf_pool_t = pools
    f, s = _compress_paged(pt, layer, cmp_pool, lw["cmp_w"][0], lw["cmp_w"][1])
    nseg = f.shape[1]
    oc, ps = _nsa_sample_cmp(op["qn"], f, s, cb, pos0, t)
    nb = -(-(pos0 + t) // SEL_BLOCK)
    sel = _select_rows(ps.reshape(bsz * NSA_KV_HEADS * TP, nseg), nb, pos0, TP, t)
    sel = sel.reshape(bsz, NSA_KV_HEADS, TP, -1)
    o_nsa = _nsa_sample(pt, layer, sel_pool, op["qr"], sel, op["sel"], wstate, op["win"], oc, op["gate"], pos0, t)
    o_diff = _diff_sample(pt, layer, diff_pool, op["dq"], op["diff"], lw["diff_lambda"], lw["diff_gain"],
                          lw["lam_init"], pos0, t)
    crow = _cumsum_paged(pt, layer, logf_pool_t)
    o_fox = _fox_sample(pt, layer, fox_pool, op["fq"], op["fox"], crow, op["gate"], pos0, t)
    mixed = jnp.concatenate([o_nsa, o_diff, o_fox], axis=-1)[:, :t].reshape(n, -1)
    x2 = _linear(mixed, lw["w_o_ext"], res=x2)
    x2 = _mem_and_peer(x2, bsz, mem_kv, lw)
    return x2, o


def kernel(x_prompt, x_sample, cache_nsa_cmp_kv, cache_nsa_sel_kv, cache_diff_kv, cache_fox_kv, cache_fox_logf, state_nsa_win_kv, cache_mem_kv, page_table, mem_prompt, g_attn, w_in, b_in, nsa_pe_k, nsa_pe_v, nsa_w_ck, nsa_w_cv, diff_lambda, diff_gain, w_o, g_mem, w_mq, w_mkv, w_mo, g_ffn, peer_wq, peer_keys, peer_u, peer_v, g_final):
    bp, tp, d = x_prompt.shape
    bs, ts, _ = x_sample.shape
    depth = w_in.shape[0]
    n_pool, psz = cache_nsa_cmp_kv.shape[1:3]
    pos0 = page_table.shape[1] * psz
    assert pos0 % CMP_STRIDE == 0 and ts < CMP_STRIDE and ts <= TP
    pos_p = jnp.arange(tp, dtype=jnp.int32)
    pos_s = pos0 + (jnp.arange(bs * ts, dtype=jnp.int32) % ts)
    tabs_p = (_rope_tables(pos_p, HEAD_DIM, ROT_DIM), _rope_tables(pos_p, DIFF_HALF, DIFF_ROT))
    tabs_s = (_rope_tables(pos_s, HEAD_DIM, ROT_DIM), _rope_tables(pos_s, DIFF_HALF, DIFF_ROT))
    xp = x_prompt.reshape(bp * tp, d)
    xs = x_sample.reshape(bs * ts, d)
    names = ("cmp", "sel", "diff", "fox")
    rows_p = {k: [] for k in names + ("logf", "win", "mem")}
    rows_s = {k: [] for k in names + ("logf", "win")}
    pools = tuple(_feature_major(c) for c in
                  (cache_nsa_cmp_kv, cache_nsa_sel_kv, cache_diff_kv, cache_fox_kv, cache_fox_logf))
    wstate_t = _feature_major(state_nsa_win_kv)
    for l in range(depth):
        w_ext, b_ext = _extend_inproj(w_in[l], b_in[l])
        lw = dict(
            g_attn=g_attn[l], w_ext=w_ext, b_ext=b_ext,
            cmp_w=_compress_weights(nsa_w_ck[l], nsa_w_cv[l], nsa_pe_k[l], nsa_pe_v[l]),
            diff_lambda=diff_lambda[l], diff_gain=diff_gain[l], lam_init=0.8 - 0.6 * math.exp(-0.3 * l),
            w_o_ext=_extend_wo(w_o[l]), g_mem=g_mem[l], w_mq=w_mq[l], w_mkv=w_mkv[l], w_mo=w_mo[l],
            g_ffn=g_ffn[l], peer_wq=peer_wq[l], peer_keys=peer_keys[l],
            peer_u=peer_u[l].astype(BF16), peer_vt=peer_v[l].T.astype(BF16))
        xp, o, mkv, cb = _prompt_layer(xp, bp, mem_prompt, lw, tabs_p)
        wstate = state_nsa_win_kv[l].reshape(bs, -1, 256)
        xs, os_ = _sample_layer(xs, bs, page_table, l, pools, wstate_t, cache_mem_kv[l].reshape(bs, -1, 2 * d), lw,
                                tabs_s, cb, pos0)
        for k in names:
            rows_p[k].append(o[k])
            rows_s[k].append(os_[k])
        rows_p["logf"].append(o["gate"][..., 2 * LANES:2 * LANES + FOX_HEADS])
        rows_s["logf"].append(os_["gate"][..., 2 * LANES:2 * LANES + FOX_HEADS])
        rows_p["win"].append(o["win"][:, -min(WINDOW, tp):])
        win_all = jnp.concatenate([wstate, os_["win"]], axis=1)
        rows_s["win"].append(win_all[:, -min(WINDOW, win_all.shape[1]):])
        rows_p["mem"].append(mkv)
    y_p = _final_norm(xp, g_final).reshape(bp, tp, d)
    y_s = _final_norm(xs, g_final).reshape(bs, ts, d)

    def st(lst, tail):
        a = jnp.stack(lst, axis=0)
        return a.reshape(a.shape[:3] + tail)

    kv2 = (2, NSA_KV_HEADS, HEAD_DIM)
    kv4 = (2, DIFF_HEADS, HEAD_DIM)
    return (y_p, y_s,
            st(rows_p["cmp"], kv2), st(rows_s["cmp"], kv2), st(rows_p["sel"], kv2), st(rows_s["sel"], kv2),
            st(rows_p["diff"], kv4), st(rows_s["diff"], kv4), st(rows_p["fox"], kv4), st(rows_s["fox"], kv4),
            st(rows_p["logf"], (FOX_HEADS,)), st(rows_s["logf"], (FOX_HEADS,)),
            st(rows_p["win"], kv2), st(rows_s["win"], kv2),
            jnp.stack(rows_p["mem"], 0).reshape(depth, bp, -1, 2, MEM_HEADS, d // MEM_HEADS))
```
